```python
import math
import jax, jax.numpy as jnp
from jax import lax
import numpy as np

D_MODEL = 1024
BATCH = 4
SEQ = 4096
DEPTH = 1
DEC_BATCH = 128
DEC_SEQ = 4
PAST_LEN = 8192
PAGE_SIZE = 128

W_MIX = D_MODEL
W_ATTN = D_MODEL // 2
W_HGRN = W_MIX - W_ATTN
D_HA = 64
H_A = W_ATTN // D_HA
H_I = 8
D_I = 64
TOPK_MAX = 256
HGRN_EXPAND = 128
H_B = W_HGRN // HGRN_EXPAND
DK_B = HGRN_EXPAND
DV_B = W_HGRN // H_B
HGRN_CHUNK = 64
D_FF = 2816
T5_BUCKETS = 32
T5_MAX_DIST = 128
Q_BLOCK = 128
RMS_EPS = 1e-6
N_MOD = 9
IN_SIZES = (W_ATTN, W_ATTN, W_ATTN, H_I * D_I, D_I, H_I, W_HGRN, W_HGRN, W_HGRN, W_HGRN)
N_IN = sum(IN_SIZES)

kernel_name = "hymba_dsa_hgrn2_macaron_step"


def rms_norm(x, gain):
    xf = x.astype(jnp.float32)
    y = xf * lax.rsqrt(jnp.mean(xf * xf, axis=-1, keepdims=True) + RMS_EPS)
    return y * gain.astype(jnp.float32)


def t5_bucket(dist):
    n = jnp.maximum(dist, 0)
    max_exact = T5_BUCKETS // 2
    nf = jnp.maximum(n, 1).astype(jnp.float32)
    large = max_exact + (jnp.log(nf / max_exact) / math.log(T5_MAX_DIST / max_exact)
                         * (T5_BUCKETS - max_exact)).astype(jnp.int32)
    large = jnp.minimum(large, T5_BUCKETS - 1)
    return jnp.where(n < max_exact, n, large)


def swiglu(h, wg, wu, wd):
    return (jax.nn.silu(h @ wg) * (h @ wu)) @ wd


def ada_params(c, w_ada, b_ada):
    m = jax.nn.silu(c.astype(jnp.float32)) @ w_ada + b_ada
    return jnp.split(m[:, None, :], N_MOD, axis=-1)


def mix_project(h, w_in, g_q, g_k, lb):
    B, T, _ = h.shape
    p = h @ w_in
    offs = np.cumsum(IN_SIZES)[:-1].tolist()
    qa, ka, va, qi, ki, wi, qb, fb, ib, gb = jnp.split(p, offs, axis=-1)
    qa = rms_norm(qa.reshape(B, T, H_A, D_HA), g_q)
    ka = rms_norm(ka.reshape(B, T, H_A, D_HA), g_k)
    va = va.reshape(B, T, H_A, D_HA)
    qi = qi.reshape(B, T, H_I, D_I)
    wi = wi * (H_I ** -0.5)
    qb = jax.nn.silu(qb).reshape(B, T, H_B, DK_B)
    f = lb + (1.0 - lb) * jax.nn.sigmoid(fb.astype(jnp.float32))
    logf = jnp.log(f).reshape(B, T, H_B, DK_B)
    kb = (1.0 - f).reshape(B, T, H_B, DK_B)
    vb = ib.reshape(B, T, H_B, DV_B)
    return qa, ka, va, qi, ki, wi, qb, kb, vb, logf, gb


def indexer_topk(qi, wi, ki_all, qpos, n_sel):
    L = ki_all.shape[1]
    s = jnp.einsum('bthd,bsd->bths', qi.astype(jnp.float32), ki_all.astype(jnp.float32)) * (D_I ** -0.5)
    score = jnp.einsum('bths,bth->bts', jax.nn.relu(s), wi.astype(jnp.float32))
    kpos = jnp.arange(L, dtype=jnp.int32)
    admissible = kpos[None, :] <= qpos[:, None]
    score = jnp.where(admissible[None], score, -jnp.inf)
    _, idx = lax.top_k(score, n_sel)
    valid = idx <= qpos[None, :, None]
    return idx, valid


def gathered_attention(qa, k_sel, v_sel, idx, valid, qpos, rel_bias):
    logits = jnp.einsum('bthd,btkhd->bthk', qa.astype(jnp.float32), k_sel.astype(jnp.float32)) * (D_HA ** -0.5)
    bias = rel_bias.astype(jnp.float32)[t5_bucket(qpos[None, :, None] - idx)]
    logits = logits + jnp.moveaxis(bias, -1, 2)
    logits = jnp.where(valid[:, :, None, :], logits, -jnp.inf)
    p = jax.nn.softmax(logits, axis=-1)
    return jnp.einsum('bthk,btkhd->bthd', p, v_sel.astype(jnp.float32))


def gather_rows(rows, ix):
    return jax.vmap(lambda r, i: r[i])(rows, ix)


def prompt_sparse_attention(qa, ka, va, qi, ki, wi, rel_bias):
    B, S = qa.shape[:2]
    nb = S // Q_BLOCK
    n_sel = min(TOPK_MAX, S // 4)

    def to_blocks(t):
        return jnp.moveaxis(t.reshape((B, nb, Q_BLOCK) + t.shape[2:]), 1, 0)

    def block(args):
        qa_b, qi_b, wi_b, blk = args
        qpos = blk * Q_BLOCK + jnp.arange(Q_BLOCK, dtype=jnp.int32)
        idx, valid = indexer_topk(qi_b, wi_b, ki, qpos, n_sel)
        return gathered_attention(qa_b, gather_rows(ka, idx), gather_rows(va, idx), idx, valid, qpos, rel_bias)

    out = lax.map(block, (to_blocks(qa), to_blocks(qi), to_blocks(wi), jnp.arange(nb, dtype=jnp.int32)))
    return jnp.moveaxis(out, 0, 1).reshape(B, S, H_A, D_HA)


def sample_sparse_attention(qa, ka, va, qi, ki, wi, cache_k, cache_v, cache_kidx, layer, page_table, rel_bias):
    Bd, T = qa.shape[:2]
    L = PAST_LEN + T
    n_sel = min(TOPK_MAX, L // 4)
    qpos = PAST_LEN + jnp.arange(T, dtype=jnp.int32)
    ki_past = cache_kidx[layer, page_table].reshape(Bd, PAST_LEN, D_I)
    ki_all = jnp.concatenate([ki_past.astype(jnp.float32), ki.astype(jnp.float32)], axis=1)
    idx, valid = indexer_topk(qi, wi, ki_all, qpos, n_sel)
    in_past = (idx < PAST_LEN)[..., None, None]
    pidx = jnp.minimum(idx, PAST_LEN - 1)
    phys = jnp.take_along_axis(page_table, (pidx // PAGE_SIZE).reshape(Bd, -1), axis=1).reshape(idx.shape)
    off = pidx % PAGE_SIZE
    nidx = jnp.clip(idx - PAST_LEN, 0, T - 1)
    k_sel = jnp.where(in_past, cache_k[layer, phys, off].astype(jnp.float32), gather_rows(ka, nidx))
    v_sel = jnp.where(in_past, cache_v[layer, phys, off].astype(jnp.float32), gather_rows(va, nidx).astype(jnp.float32))
    return gathered_attention(qa, k_sel, v_sel, idx, valid, qpos, rel_bias)


def hgrn2_chunked(q, k, v, logf, state0):
    B, T, H = q.shape[:3]
    C = math.gcd(T, HGRN_CHUNK)
    nc = T // C

    def to_chunks(t):
        return jnp.transpose(t.astype(jnp.float32).reshape(B, nc, C, H, t.shape[3]), (1, 0, 3, 2, 4))

    causal = jnp.tril(jnp.ones((C, C), dtype=bool))

    def step(S, xs):
        qc, kc, vc, gc = xs
        G = jnp.cumsum(gc, axis=2)
        o_inter = jnp.einsum('bhtd,bhde->bhte', qc * jnp.exp(G), S)
        diff = G[:, :, :, None, :] - G[:, :, None, :, :]
        decay = jnp.exp(jnp.where(causal[:, :, None], diff, -jnp.inf))
        A = jnp.einsum('bhtd,bhsd,bhtsd->bhts', qc, kc, decay)
        o_intra = jnp.einsum('bhts,bhse->bhte', A, vc)
        G_end = G[:, :, -1, :]
        S = jnp.exp(G_end)[..., None] * S + jnp.einsum('bhsd,bhse->bhde', kc * jnp.exp(G_end[:, :, None, :] - G), vc)
        return S, o_inter + o_intra

    S, o = lax.scan(step, state0.astype(jnp.float32), (to_chunks(q), to_chunks(k), to_chunks(v), to_chunks(logf)))
    o = jnp.transpose(o, (1, 0, 3, 2, 4)).reshape(B, T, H, v.shape[3])
    return o, S


def merge_heads(o_attn, o_hgrn, gb, beta_attn, g_hgrn, w_out):
    B, T = o_attn.shape[:2]
    a = rms_norm(o_attn, beta_attn.reshape(H_A, D_HA)).reshape(B, T, W_ATTN)
    r = rms_norm(o_hgrn, g_hgrn.reshape(H_B, DV_B)).reshape(B, T, W_HGRN) * jax.nn.silu(gb.astype(jnp.float32))
    return jnp.concatenate([a, r], axis=-1) @ w_out


def run_layer(x, c, lw, lb, mixer):
    (w_ada, b_ada, g_ffn1, f1_wg, f1_wu, f1_wd, g_mix, w_in, g_q, g_k,
     beta_attn, g_hgrn, w_out, g_ffn2, f2_wg, f2_wu, f2_wd) = lw
    sh1, sc1, gt1, sh2, sc2, gt2, sh3, sc3, gt3 = ada_params(c, w_ada, b_ada)
    h = rms_norm(x, g_ffn1) * (1.0 + sc1) + sh1
    x = x + 0.5 * gt1 * swiglu(h, f1_wg, f1_wu, f1_wd)
    h = rms_norm(x, g_mix) * (1.0 + sc2) + sh2
    qa, ka, va, qi, ki, wi, qb, kb, vb, logf, gb = mix_project(h, w_in, g_q, g_k, lb)
    o_attn, o_hgrn, new_state = mixer(qa, ka, va, qi, ki, wi, qb, kb, vb, logf)
    x = x + gt2 * merge_heads(o_attn, o_hgrn, gb, beta_attn, g_hgrn, w_out)
    h = rms_norm(x, g_ffn2) * (1.0 + sc3) + sh3
    x = x + 0.5 * gt3 * swiglu(h, f2_wg, f2_wu, f2_wd)
    return x, new_state


def setup_inputs(seed: int = 0) -> dict:
    key = jax.random.key(seed)
    keys = iter(jax.random.split(key, 40))

    def nrm(shape, scale):
        return jax.random.normal(next(keys), shape, jnp.float32) * scale

    def gain(shape):
        return 1.0 + nrm(shape, 0.02)

    n_pages = PAST_LEN // PAGE_SIZE
    n_phys = (DEC_BATCH * n_pages * 5) // 4
    page_table = jax.random.permutation(next(keys), n_phys)[: DEC_BATCH * n_pages].reshape(DEC_BATCH, n_pages).astype(jnp.int32)
    return {
        "x_prompt": nrm((BATCH, SEQ, D_MODEL), 1.0),
        "x_sample": nrm((DEC_BATCH, DEC_SEQ, D_MODEL), 1.0),
        "cache_k": nrm((DEPTH, n_phys, PAGE_SIZE, H_A, D_HA), 1.0),
        "cache_v": nrm((DEPTH, n_phys, PAGE_SIZE, H_A, D_HA), 1.0),
        "cache_kidx": nrm((DEPTH, n_phys, PAGE_SIZE, D_I), 1.0),
        "state_hgrn": nrm((DEPTH, DEC_BATCH, H_B, DK_B, DV_B), 0.5),
        "page_table": page_table,
        "c_prompt": nrm((BATCH, D_MODEL), 1.0),
        "c_sample": nrm((DEC_BATCH, D_MODEL), 1.0),
        "w_ada": nrm((DEPTH, D_MODEL, N_MOD * D_MODEL), 0.3 * D_MODEL ** -0.5),
        "b_ada": nrm((DEPTH, N_MOD * D_MODEL), 0.02),
        "g_ffn1": gain((DEPTH, D_MODEL)),
        "ffn1_w_gate": nrm((DEPTH, D_MODEL, D_FF), D_MODEL ** -0.5),
        "ffn1_w_up": nrm((DEPTH, D_MODEL, D_FF), D_MODEL ** -0.5),
        "ffn1_w_down": nrm((DEPTH, D_FF, D_MODEL), D_FF ** -0.5),
        "g_mix": gain((DEPTH, D_MODEL)),
        "w_in": nrm((DEPTH, D_MODEL, N_IN), D_MODEL ** -0.5),
        "g_q": gain((DEPTH, D_HA)),
        "g_k": gain((DEPTH, D_HA)),
        "beta_attn": gain((DEPTH, W_ATTN)),
        "g_hgrn": gain((DEPTH, W_HGRN)),
        "w_out": nrm((DEPTH, W_MIX, D_MODEL), W_MIX ** -0.5),
        "g_ffn2": gain((DEPTH, D_MODEL)),
        "ffn2_w_gate": nrm((DEPTH, D_MODEL, D_FF), D_MODEL ** -0.5),
        "ffn2_w_up": nrm((DEPTH, D_MODEL, D_FF), D_MODEL ** -0.5),
        "ffn2_w_down": nrm((DEPTH, D_FF, D_MODEL), D_FF ** -0.5),
        "rel_bias": nrm((T5_BUCKETS, H_A), 0.5),
        "lb_logits": nrm((DEPTH + 1, W_HGRN), 0.5),
    }


def reference(x_prompt, x_sample, cache_k, cache_v, cache_kidx, state_hgrn, page_table, c_prompt, c_sample,
              w_ada, b_ada, g_ffn1, ffn1_w_gate, ffn1_w_up, ffn1_w_down, g_mix, w_in, g_q, g_k,
              beta_attn, g_hgrn, w_out, g_ffn2, ffn2_w_gate, ffn2_w_up, ffn2_w_down, rel_bias, lb_logits):
    lower_bounds = jnp.cumsum(jax.nn.softmax(lb_logits.astype(jnp.float32), axis=0), axis=0)
    hp = x_prompt.astype(jnp.float32)
    hs = x_sample.astype(jnp.float32)
    kp, vp, ip, sp = [], [], [], []
    ks, vs, is_, ss = [], [], [], []
    for layer in range(DEPTH):
        lw = (w_ada[layer], b_ada[layer], g_ffn1[layer], ffn1_w_gate[layer], ffn1_w_up[layer], ffn1_w_down[layer],
              g_mix[layer], w_in[layer], g_q[layer], g_k[layer], beta_attn[layer], g_hgrn[layer], w_out[layer],
              g_ffn2[layer], ffn2_w_gate[layer], ffn2_w_up[layer], ffn2_w_down[layer])
        lb = lower_bounds[layer]

        def prompt_mixer(qa, ka, va, qi, ki, wi, qb, kb, vb, logf):
            o_a = prompt_sparse_attention(qa, ka, va, qi, ki, wi, rel_bias)
            s0 = jnp.zeros((qb.shape[0], H_B, DK_B, DV_B), jnp.float32)
            o_b, s_fin = hgrn2_chunked(qb, kb, vb, logf, s0)
            return o_a, o_b, (ka, va, ki, s_fin)

        def sample_mixer(qa, ka, va, qi, ki, wi, qb, kb, vb, logf):
            o_a = sample_sparse_attention(qa, ka, va, qi, ki, wi, cache_k, cache_v, cache_kidx, layer, page_table, rel_bias)
            o_b, s_fin = hgrn2_chunked(qb, kb, vb, logf, state_hgrn[layer])
            return o_a, o_b, (ka, va, ki, s_fin)

        hp, (k1, v1, i1, s1) = run_layer(hp, c_prompt, lw, lb, prompt_mixer)
        hs, (k2, v2, i2, s2) = run_layer(hs, c_sample, lw, lb, sample_mixer)
        kp.append(k1); vp.append(v1); ip.append(i1); sp.append(s1)
        ks.append(k2); vs.append(v2); is_.append(i2); ss.append(s2)
    return (hp.astype(x_prompt.dtype), hs.astype(x_sample.dtype),
            jnp.stack(kp), jnp.stack(vp), jnp.stack(ip), jnp.stack(sp),
            jnp.stack(ks), jnp.stack(vs), jnp.stack(is_), jnp.stack(ss))
```

```python
import functools
import math

import jax
import jax.numpy as jnp
from jax import lax
from jax.experimental import pallas as pl
from jax.experimental.pallas import tpu as pltpu

F32 = jnp.float32
BF16 = jnp.bfloat16
I32 = jnp.int32

D_MODEL = 1024
W_ATTN = 512
W_HGRN = 512
D_HA = 64
H_A = 8
H_I = 8
D_I = 64
TOPK_MAX = 256
H_B = 4
DK_B = 128
DV_B = 128
D_FF = 2816
T5_BUCKETS = 32
T5_MAX_DIST = 128
RMS_EPS = 1e-6
N_MOD = 9
PAGE = 128

LANES = 128
SUBLANES = 8
NEG = -1e30
INT_MIN = -(2 ** 31)
VMEM_LIMIT = 56 * 1024 * 1024

_NT = (((1,), (1,)), ((), ()))
_TN = (((0,), (0,)), ((), ()))


def _cp(sem):
    return pltpu.CompilerParams(dimension_semantics=sem, vmem_limit_bytes=VMEM_LIMIT)


def _silu(x):
    return x / (1.0 + jnp.exp(-x))


def _sigmoid(x):
    return 1.0 / (1.0 + jnp.exp(-x))


def _rms(x, g):
    return x * lax.rsqrt(jnp.mean(x * x, axis=-1, keepdims=True) + RMS_EPS) * g


def _group_mean_sq(x, bd):
    xx = x * x
    hi = xx.astype(BF16)
    lo = (xx - hi.astype(F32)).astype(BF16)
    return (jnp.dot(hi, bd, preferred_element_type=F32)
            + jnp.dot(lo, bd, preferred_element_type=F32))


def _ada_kernel(c_ref, w_ref, b_ref, o_ref):
    a = _silu(c_ref[...]).astype(BF16)
    o_ref[...] = jnp.dot(a, w_ref[...], preferred_element_type=F32) + b_ref[...]


def _ada(c_all, w16, b):
    rows = c_all.shape[0]
    return pl.pallas_call(
        _ada_kernel,
        out_shape=jax.ShapeDtypeStruct((N_MOD, rows, D_MODEL), F32),
        grid=(N_MOD,),
        in_specs=[pl.BlockSpec((rows, D_MODEL), lambda j: (0, 0)),
                  pl.BlockSpec((D_MODEL, D_MODEL), lambda j: (0, j)),
                  pl.BlockSpec((1, D_MODEL), lambda j: (0, j))],
        out_specs=pl.BlockSpec((None, rows, D_MODEL), lambda j: (j, 0, 0)),
        compiler_params=_cp(("arbitrary",)),
        name="ada",
    )(c_all, w16, b)


class _Mod:
    def __init__(self, arr, per_token, tiles_per_seq=1):
        self.arr = arr
        self.per_token = per_token
        self.tiles_per_seq = tiles_per_seq

    def spec(self, j, tm, ngrid):
        if self.per_token:
            if ngrid == 1:
                return pl.BlockSpec((None, tm, D_MODEL), lambda i: (j, i, 0))
            return pl.BlockSpec((None, tm, D_MODEL), lambda i, k: (j, i, 0))
        tps = self.tiles_per_seq
        if ngrid == 1:
            return pl.BlockSpec((None, None, 1, D_MODEL), lambda i: (j, i // tps, 0, 0))
        return pl.BlockSpec((None, None, 1, D_MODEL), lambda i, k: (j, i // tps, 0, 0))


def _ffn_kernel(x_ref, sh_ref, sc_ref, gt_ref, g_ref, wg_ref, wu_ref, wd_ref, o_ref, h_sc, acc_sc):
    k = pl.program_id(1)

    @pl.when(k == 0)
    def _():
        h = _rms(x_ref[...], g_ref[...]) * (1.0 + sc_ref[...]) + sh_ref[...]
        h_sc[...] = h.astype(BF16)
        acc_sc[...] = jnp.zeros_like(acc_sc)

    h = h_sc[...]
    a = jnp.dot(h, wg_ref[...], preferred_element_type=F32)
    b = jnp.dot(h, wu_ref[...], preferred_element_type=F32)
    act = (_silu(a) * b).astype(BF16)
    acc_sc[...] += jnp.dot(act, wd_ref[...], preferred_element_type=F32)

    @pl.when(k == pl.num_programs(1) - 1)
    def _():
        o_ref[...] = x_ref[...] + 0.5 * gt_ref[...] * acc_sc[...]


def _ffn(x, mod, j0, g, wg, wu, wd, tm, tf):
    n = x.shape[0]
    nk = D_FF // tf
    return pl.pallas_call(
        _ffn_kernel,
        out_shape=jax.ShapeDtypeStruct((n, D_MODEL), F32),
        grid=(n // tm, nk),
        in_specs=[pl.BlockSpec((tm, D_MODEL), lambda i, k: (i, 0)),
                  mod.spec(j0, tm, 2), mod.spec(j0 + 1, tm, 2), mod.spec(j0 + 2, tm, 2),
                  pl.BlockSpec((1, D_MODEL), lambda i, k: (0, 0)),
                  pl.BlockSpec((D_MODEL, tf), lambda i, k: (0, k)),
                  pl.BlockSpec((D_MODEL, tf), lambda i, k: (0, k)),
                  pl.BlockSpec((tf, D_MODEL), lambda i, k: (k, 0))],
        out_specs=pl.BlockSpec((tm, D_MODEL), lambda i, k: (i, 0)),
        scratch_shapes=[pltpu.VMEM((tm, D_MODEL), BF16), pltpu.VMEM((tm, D_MODEL), F32)],
        compiler_params=_cp(("arbitrary", "arbitrary")),
        name="ffn",
    )(x, mod.arr, mod.arr, mod.arr, g, wg, wu, wd)


_C_QA, _C_KA, _C_VA, _C_QI = 0, 512, 1024, 1536
_C_KI, _C_WI = 2048, 2176
_C_QB, _C_FB, _C_IB, _C_GB = 2304, 2816, 3328, 3840
_N_PACK = 4352


def _mix_kernel(x_ref, sh_ref, sc_ref, g_ref, w_ref, gq_ref, gk_ref, lbl_ref, bd_ref,
                qa_o, ka_o, ka16_o, va_o, va16_o, qi_o, ki_o, ki16_o, wi_o, qb_o, f_o, vb_o, sg_o):
    h = (_rms(x_ref[...], g_ref[...]) * (1.0 + sc_ref[...]) + sh_ref[...]).astype(BF16)

    def proj(c0, width):
        return jnp.dot(h, w_ref[:, c0:c0 + width], preferred_element_type=F32)

    bd = bd_ref[...]
    qa = proj(_C_QA, W_ATTN)
    qa = qa * lax.rsqrt(_group_mean_sq(qa, bd) + RMS_EPS) * gq_ref[...]
    qa_o[...] = (qa * (D_HA ** -0.5)).astype(BF16)
    ka = proj(_C_KA, W_ATTN)
    ka = ka * lax.rsqrt(_group_mean_sq(ka, bd) + RMS_EPS) * gk_ref[...]
    ka_o[...] = ka
    ka16_o[...] = ka.astype(BF16)
    va = proj(_C_VA, W_ATTN)
    va_o[...] = va
    va16_o[...] = va.astype(BF16)
    qi_o[...] = (proj(_C_QI, H_I * D_I) * (D_I ** -0.5)).astype(BF16)
    ki = proj(_C_KI, LANES)
    ki_o[...] = ki
    ki16_o[...] = ki.astype(BF16)
    wi_o[...] = proj(_C_WI, LANES) * (H_I ** -0.5)
    qb_o[...] = _silu(proj(_C_QB, W_HGRN))
    l0 = lbl_ref[0:1, :]
    l1 = lbl_ref[1:2, :]
    mx = jnp.maximum(l0, l1)
    e0 = jnp.exp(l0 - mx)
    e1 = jnp.exp(l1 - mx)
    lb = e0 / (e0 + e1)
    f_o[...] = lb + (1.0 - lb) * _sigmoid(proj(_C_FB, W_HGRN))
    vb_o[...] = proj(_C_IB, W_HGRN)
    sg_o[...] = _silu(proj(_C_GB, W_HGRN))


def _mix(x, mod, g, wpack, gq, gk, lbl, bd, tm):
    n = x.shape[0]
    row = lambda w: pl.BlockSpec((tm, w), lambda i: (i, 0))
    full = lambda a: pl.BlockSpec(a.shape, lambda i: (0,) * a.ndim)
    outs = [(W_ATTN, BF16), (W_ATTN, F32), (W_ATTN, BF16), (W_ATTN, F32), (W_ATTN, BF16),
            (H_I * D_I, BF16), (LANES, F32), (LANES, BF16), (LANES, F32),
            (W_HGRN, F32), (W_HGRN, F32), (W_HGRN, F32), (W_HGRN, F32)]
    return pl.pallas_call(
        _mix_kernel,
        out_shape=[jax.ShapeDtypeStruct((n, w), dt) for w, dt in outs],
        grid=(n // tm,),
        in_specs=[row(D_MODEL), mod.spec(3, tm, 1), mod.spec(4, tm, 1), full(g), full(wpack),
                  full(gq), full(gk), full(lbl), full(bd)],
        out_specs=[row(w) for w, _ in outs],
        compiler_params=_cp(("arbitrary",)),
        name="mix",
    )(x, mod.arr, mod.arr, g, wpack, gq, gk, lbl, bd)


def _t5_bucket(dist):
    n = jnp.maximum(dist, 0)
    max_exact = T5_BUCKETS // 2
    nf = jnp.maximum(n, 1).astype(F32)
    large = max_exact + (jnp.log(nf / max_exact) / math.log(T5_MAX_DIST / max_exact)
                         * (T5_BUCKETS - max_exact)).astype(I32)
    large = jnp.minimum(large, T5_BUCKETS - 1)
    return jnp.where(n < max_exact, n, large)


def _bias_kernel(rb_ref, bp_ref, bs_ref):
    def lookup(bucket, h):
        val = jnp.zeros(bucket.shape, F32)
        for b in range(T5_BUCKETS):
            val = jnp.where(bucket == b, rb_ref[b, h], val)
        return val

    r = lax.broadcasted_iota(I32, (LANES, LANES), 0)
    c = lax.broadcasted_iota(I32, (LANES, LANES), 1)
    dists = (r - c, LANES + r - c, jnp.full((LANES, LANES), 2 * LANES, I32))
    for m in range(3):
        bucket = _t5_bucket(dists[m])
        for h in range(H_A):
            bp_ref[m, h // 2, (h % 2) * LANES:(h % 2 + 1) * LANES, :] = lookup(bucket, h)

    t = lax.broadcasted_iota(I32, (SUBLANES, LANES), 0)
    u = lax.broadcasted_iota(I32, (SUBLANES, LANES), 1)
    sdists = (PAGE + t - u, t - u, jnp.full((SUBLANES, LANES), 2 * LANES, I32))
    for m in range(3):
        bucket = _t5_bucket(sdists[m])
        for h in range(H_A):
            bs_ref[m, h * SUBLANES:(h + 1) * SUBLANES, :] = lookup(bucket, h)


def _bias_tiles(rel_bias):
    return pl.pallas_call(
        _bias_kernel,
        out_shape=[jax.ShapeDtypeStruct((3, H_A // 2, 2 * LANES, LANES), F32),
                   jax.ShapeDtypeStruct((3, H_A * SUBLANES, LANES), F32)],
        in_specs=[pl.BlockSpec(memory_space=pltpu.SMEM)],
        out_specs=[pl.BlockSpec(memory_space=pltpu.VMEM), pl.BlockSpec(memory_space=pltpu.VMEM)],
        name="t5_bias",
    )(rel_bias)


def _sortable_key(score):
    bits = pltpu.bitcast(score, I32)
    return jnp.where(bits < 0, INT_MIN - bits, bits)


def _topk_maskbias(keys_ref, maskb_ref, nch, rows, cw, n_sel, idx_bits):
    nt = cw // LANES
    lane = lax.broadcasted_iota(I32, (rows, LANES), 1)

    def count(indicator):
        def body(c, acc):
            k0 = pl.multiple_of(c * cw, cw)
            for j in range(nt):
                kk = keys_ref[:, pl.ds(k0 + j * LANES, LANES)]
                acc = acc + indicator(kk, k0 + j * LANES)
            return acc
        acc = lax.fori_loop(0, nch, body, jnp.zeros((rows, LANES), F32))
        return jnp.broadcast_to(jnp.sum(acc, axis=1, keepdims=True), (rows, LANES))

    def bisect(it, ans):
        cand = ans | jnp.left_shift(jnp.int32(1), 31 - it)
        cs = cand ^ INT_MIN
        cnt = count(lambda kk, p0: jnp.where(kk >= cs, 1.0, 0.0))
        return jnp.where(cnt >= n_sel, cand, ans)

    thr = lax.fori_loop(0, 32, bisect, jnp.zeros((rows, LANES), I32)) ^ INT_MIN
    need = n_sel - count(lambda kk, p0: jnp.where(kk > thr, 1.0, 0.0))
    n_eq = count(lambda kk, p0: jnp.where(kk == thr, 1.0, 0.0))
    has_tie = jnp.max(jnp.where(n_eq > need, 1.0, 0.0)) > 0.0

    def tie_search(_):
        def step(it, aj):
            cand = aj | jnp.left_shift(jnp.int32(1), idx_bits - 1 - it)
            cnt = count(lambda kk, p0: jnp.where(kk == thr, jnp.where((p0 + lane) < cand, 1.0, 0.0), 0.0))
            return jnp.where(cnt < need, cand, aj)
        return lax.fori_loop(0, idx_bits, step, jnp.zeros((rows, LANES), I32))

    cut = lax.cond(has_tie, tie_search, lambda _: jnp.full((rows, LANES), 2 ** 30, I32), 0)
    cut = jnp.where(thr == INT_MIN, -1, cut)

    def fin(c, _):
        k0 = pl.multiple_of(c * cw, cw)
        for j in range(nt):
            sl = pl.ds(k0 + j * LANES, LANES)
            kk = keys_ref[:, sl]
            tie = jnp.where((k0 + j * LANES + lane) <= cut, 0.0, NEG)
            maskb_ref[:, sl] = jnp.where(kk > thr, 0.0, jnp.where(kk == thr, tie, NEG))
        return 0

    lax.fori_loop(0, nch, fin, 0)


def _softmax_step(carry, s, v16):
    m, l, acc = carry
    mn = jnp.maximum(m, jnp.max(s, axis=1, keepdims=True))
    alpha = jnp.exp(m - mn)
    p = jnp.exp(s - mn)
    l = alpha * l + jnp.sum(p, axis=1, keepdims=True)
    acc = alpha * acc + jnp.dot(p.astype(BF16), v16, preferred_element_type=F32)
    return mn, l, acc


_QB = 128
_KC = 512


def _attn_prompt_kernel(qa_ref, qi_ref, wi_ref, ki_ref, ka_ref, va_ref, bias_ref, o_ref,
                        keys_sc, maskb_sc, qs_sc, wb_sc):
    i = pl.program_id(1)
    lane = lax.broadcasted_iota(I32, (_QB, LANES), 1)
    lo = lane < D_I
    nch = i // (_KC // _QB) + 1

    for h in range(H_I):
        pair = qi_ref[:, (h // 2) * LANES:(h // 2 + 1) * LANES].astype(F32)
        keep = jnp.where(lo, pair, 0.0) if h % 2 == 0 else jnp.where(lo, 0.0, pair)
        qs_sc[h * _QB:(h + 1) * _QB, :] = keep.astype(BF16)
        wb_sc[h * _QB:(h + 1) * _QB, :] = jnp.broadcast_to(wi_ref[:, h:h + 1], (_QB, LANES))

    qpos = i * _QB + lax.broadcasted_iota(I32, (_QB, _KC), 0)

    def scores(c, _):
        k0 = pl.multiple_of(c * _KC, _KC)
        s = lax.dot_general(qs_sc[...], ki_ref[pl.ds(k0, _KC), :], _NT, preferred_element_type=F32)
        sc = jnp.zeros((_QB, _KC), F32)
        for h in range(H_I):
            wbh = wb_sc[h * _QB:(h + 1) * _QB, :]
            sc = sc + jnp.maximum(s[h * _QB:(h + 1) * _QB, :], 0.0) * jnp.concatenate([wbh] * (_KC // LANES), axis=1)
        kpos = k0 + lax.broadcasted_iota(I32, (_QB, _KC), 1)
        keys_sc[:, pl.ds(k0, _KC)] = jnp.where(kpos <= qpos, _sortable_key(sc), INT_MIN)
        return 0

    lax.fori_loop(0, nch, scores, 0)
    _topk_maskbias(keys_sc, maskb_sc, nch, _QB, _KC, TOPK_MAX, 12)

    nfar = jnp.maximum(i - 1, 0) // (_KC // _QB)
    tail0 = nfar * (_KC // _QB)
    for p in range(H_A // 2):
        qp = qa_ref[:, p * LANES:(p + 1) * LANES].astype(F32)
        q2 = jnp.concatenate([jnp.where(lo, qp, 0.0), jnp.where(lo, 0.0, qp)], axis=0).astype(BF16)
        far_bias = jnp.concatenate([bias_ref[2, p]] * (_KC // LANES), axis=1)

        def far(c, carry):
            k0 = pl.multiple_of(c * _KC, _KC)
            s = lax.dot_general(q2, ka_ref[pl.ds(k0, _KC), p * LANES:(p + 1) * LANES], _NT,
                                preferred_element_type=F32)
            mb = maskb_sc[:, pl.ds(k0, _KC)]
            s = s + jnp.concatenate([mb, mb], axis=0) + far_bias
            return _softmax_step(carry, s, va_ref[pl.ds(k0, _KC), p * LANES:(p + 1) * LANES])

        def near(jj, carry):
            j = tail0 + jj
            k0 = pl.multiple_of(j * _QB, _QB)
            s = lax.dot_general(q2, ka_ref[pl.ds(k0, _QB), p * LANES:(p + 1) * LANES], _NT,
                                preferred_element_type=F32)
            mb = maskb_sc[:, pl.ds(k0, _QB)]
            s = s + jnp.concatenate([mb, mb], axis=0) + bias_ref[jnp.minimum(i - j, 2), p]
            return _softmax_step(carry, s, va_ref[pl.ds(k0, _QB), p * LANES:(p + 1) * LANES])

        carry = (jnp.full((2 * _QB, 1), NEG, F32), jnp.zeros((2 * _QB, 1), F32), jnp.zeros((2 * _QB, LANES), F32))
        carry = lax.fori_loop(0, nfar, far, carry)
        _, l, acc = lax.fori_loop(0, i + 1 - tail0, near, carry)
        o = acc / l
        o_ref[:, p * LANES:(p + 1) * LANES] = jnp.where(lo, o[:_QB], o[_QB:])


def _attn_prompt(qa16, qi16, wi, ki16, ka16, va16, bias_p, nbatch, seq):
    nq = seq // _QB
    blk = lambda w: pl.BlockSpec((_QB, w), lambda b, i: (b * nq + i, 0))
    per_b = lambda w: pl.BlockSpec((seq, w), lambda b, i: (b, 0))
    return pl.pallas_call(
        _attn_prompt_kernel,
        out_shape=jax.ShapeDtypeStruct((nbatch * seq, W_ATTN), F32),
        grid=(nbatch, nq),
        in_specs=[blk(W_ATTN), blk(H_I * D_I), blk(LANES), per_b(LANES), per_b(W_ATTN), per_b(W_ATTN),
                  pl.BlockSpec(bias_p.shape, lambda b, i: (0, 0, 0, 0))],
        out_specs=blk(W_ATTN),
        scratch_shapes=[pltpu.VMEM((_QB, seq), I32), pltpu.VMEM((_QB, seq), F32),
                        pltpu.VMEM((H_I * _QB, LANES), BF16), pltpu.VMEM((H_I * _QB, LANES), F32)],
        compiler_params=_cp(("arbitrary", "arbitrary")),
        name="attn_prompt",
    )(qa16, qi16, wi, ki16, ka16, va16, bias_p)


_PPS = 16
_TP = SUBLANES
_TK = 2 * SUBLANES


def _sample_scores_kernel(pt_ref, q_ref, w_ref, kn_ref, *rest):
    pages, o_ref = rest[:_PPS], rest[_PPS]
    s_id = pl.program_id(1)
    q = q_ref[...]

    def head_sum(s):
        n = s.shape[1]
        out = jnp.zeros((_TP, n), F32)
        for h in range(H_I):
            wh = w_ref[h * _TP:(h + 1) * _TP, :]
            out = out + jnp.maximum(s[h * _TP:(h + 1) * _TP, :], 0.0) * jnp.concatenate([wh] * (n // LANES), axis=1)
        return out

    kc = jnp.concatenate([pg[...].astype(BF16) for pg in pages], axis=0)
    s = lax.dot_general(q, kc, _NT, preferred_element_type=F32)
    o_ref[:, pl.ds(pl.multiple_of(s_id * (_PPS * PAGE), _PPS * PAGE), _PPS * PAGE)] = head_sum(s)

    @pl.when(s_id == pl.num_programs(1) - 1)
    def _():
        kn = jnp.concatenate([kn_ref[...], jnp.zeros((LANES - _TK, D_I), BF16)], axis=0)
        sn = lax.dot_general(q, kn, _NT, preferred_element_type=F32)
        o_ref[:, o_ref.shape[1] - LANES:] = head_sum(sn)


def _sample_scores(page_table, q_st, w_b, ki_new, cache_kidx, n_pages):
    nseq = q_st.shape[0]
    steps = n_pages // _PPS
    lpad = n_pages * PAGE + LANES
    page_specs = [pl.BlockSpec((None, PAGE, D_I), functools.partial(
        lambda b, s, pt, p: (pt[b, s * _PPS + p], 0, 0), p=p)) for p in range(_PPS)]
    return pl.pallas_call(
        _sample_scores_kernel,
        out_shape=jax.ShapeDtypeStruct((nseq, _TP, lpad), F32),
        grid_spec=pltpu.PrefetchScalarGridSpec(
            num_scalar_prefetch=1, grid=(nseq, steps),
            in_specs=[pl.BlockSpec((None, H_I * _TP, D_I), lambda b, s, pt: (b, 0, 0)),
                      pl.BlockSpec((None, H_I * _TP, LANES), lambda b, s, pt: (b, 0, 0)),
                      pl.BlockSpec((None, _TK, D_I), lambda b, s, pt: (b, 0, 0))] + page_specs,
            out_specs=pl.BlockSpec((None, _TP, lpad), lambda b, s, pt: (b, 0, 0))),
        compiler_params=_cp(("arbitrary", "arbitrary")),
        name="sample_scores",
    )(page_table, q_st, w_b, ki_new, *([cache_kidx] * _PPS))


_SEL_ROWS = 64
_SEL_CW = 640


def _sample_select_kernel(s_ref, o_ref, keys_sc, *, past):
    lpad = s_ref.shape[1]
    t = lax.broadcasted_iota(I32, (_SEL_ROWS, _SEL_CW), 0) % _TP
    for c in range(lpad // _SEL_CW):
        sl = slice(c * _SEL_CW, (c + 1) * _SEL_CW)
        kpos = c * _SEL_CW + lax.broadcasted_iota(I32, (_SEL_ROWS, _SEL_CW), 1)
        keys_sc[:, sl] = jnp.where(kpos <= past + t, _sortable_key(s_ref[:, sl]), INT_MIN)
    _topk_maskbias(keys_sc, o_ref, lpad // _SEL_CW, _SEL_ROWS, _SEL_CW, TOPK_MAX, 14)


def _sample_select(scores2d, past):
    rows, lpad = scores2d.shape
    return pl.pallas_call(
        functools.partial(_sample_select_kernel, past=past),
        out_shape=jax.ShapeDtypeStruct((rows, lpad), F32),
        grid=(rows // _SEL_ROWS,),
        in_specs=[pl.BlockSpec((_SEL_ROWS, lpad), lambda g: (g, 0))],
        out_specs=pl.BlockSpec((_SEL_ROWS, lpad), lambda g: (g, 0)),
        scratch_shapes=[pltpu.VMEM((_SEL_ROWS, lpad), I32)],
        compiler_params=_cp(("arbitrary",)),
        name="sample_select",
    )(scores2d)


def _sample_attn_kernel(pt_ref, q_ref, mb_ref, kn_ref, vn_ref, bias_ref, *rest):
    kp, vp = rest[:_PPS], rest[_PPS:2 * _PPS]
    o_ref, qblk_sc, m_sc, l_sc, acc_sc = rest[2 * _PPS:]
    s_id = pl.program_id(1)
    last = pl.num_programs(1) - 1
    rows = H_A * _TP
    span = _PPS * PAGE

    @pl.when(s_id == 0)
    def _():
        q = jnp.concatenate([q_ref[...]] * H_A, axis=0)
        r = lax.broadcasted_iota(I32, (rows, W_ATTN), 0) // _TP
        c = lax.broadcasted_iota(I32, (rows, W_ATTN), 1) // D_HA
        qblk_sc[...] = jnp.where(r == c, q, 0.0).astype(BF16)
        m_sc[...] = jnp.full(m_sc.shape, NEG, F32)
        l_sc[...] = jnp.zeros_like(l_sc)
        acc_sc[...] = jnp.zeros_like(acc_sc)

    def update(s, v16):
        carry = (m_sc[:, 0:1], l_sc[:, 0:1], acc_sc[...])
        m, l, acc = _softmax_step(carry, s, v16)
        m_sc[...] = jnp.broadcast_to(m, m_sc.shape)
        l_sc[...] = jnp.broadcast_to(l, l_sc.shape)
        acc_sc[...] = acc

    k16 = jnp.concatenate([pg[...].astype(BF16) for pg in kp], axis=0)
    v16 = jnp.concatenate([pg[...].astype(BF16) for pg in vp], axis=0)
    s = lax.dot_general(qblk_sc[...], k16, _NT, preferred_element_type=F32)
    mb = mb_ref[:, pl.ds(pl.multiple_of(s_id * span, span), span)]
    last_tile = jnp.where(s_id == last, bias_ref[0], bias_ref[2])
    bias = jnp.concatenate([bias_ref[2]] * (_PPS - 1) + [last_tile], axis=1)
    update(s + jnp.concatenate([mb] * H_A, axis=0) + bias, v16)

    @pl.when(s_id == last)
    def _():
        pad = jnp.zeros((LANES - _TK, W_ATTN), BF16)
        kn = jnp.concatenate([kn_ref[...], pad], axis=0)
        vn = jnp.concatenate([vn_ref[...], pad], axis=0)
        sn = lax.dot_general(qblk_sc[...], kn, _NT, preferred_element_type=F32)
        mbn = mb_ref[:, mb_ref.shape[1] - LANES:]
        update(sn + jnp.concatenate([mbn] * H_A, axis=0) + bias_ref[1], vn)
        o = acc_sc[...] / l_sc[:, 0:1]
        c = lax.broadcasted_iota(I32, (_TP, W_ATTN), 1) // D_HA
        out = jnp.zeros((_TP, W_ATTN), F32)
        for h in range(H_A):
            out = jnp.where(c == h, o[h * _TP:(h + 1) * _TP, :], out)
        o_ref[...] = out


def _sample_attn(page_table, q8, maskb, k_new, v_new, bias_s, cache_k, cache_v, n_pages):
    nseq = q8.shape[0]
    steps = n_pages // _PPS
    lpad = maskb.shape[1]
    pspec = lambda: [pl.BlockSpec((None, PAGE, W_ATTN), functools.partial(
        lambda b, s, pt, p: (pt[b, s * _PPS + p], 0, 0), p=p)) for p in range(_PPS)]
    seq3 = lambda t: pl.BlockSpec((None, t, W_ATTN), lambda b, s, pt: (b, 0, 0))
    return pl.pallas_call(
        _sample_attn_kernel,
        out_shape=jax.ShapeDtypeStruct((nseq, _TP, W_ATTN), F32),
        grid_spec=pltpu.PrefetchScalarGridSpec(
            num_scalar_prefetch=1, grid=(nseq, steps),
            in_specs=[seq3(_TP), pl.BlockSpec((_TP, lpad), lambda b, s, pt: (b, 0)),
                      seq3(_TK), seq3(_TK),
                      pl.BlockSpec(bias_s.shape, lambda b, s, pt: (0, 0, 0))] + pspec() + pspec(),
            out_specs=seq3(_TP),
            scratch_shapes=[pltpu.VMEM((H_A * _TP, W_ATTN), BF16), pltpu.VMEM((H_A * _TP, LANES), F32),
                            pltpu.VMEM((H_A * _TP, LANES), F32), pltpu.VMEM((H_A * _TP, W_ATTN), F32)]),
        compiler_params=_cp(("arbitrary", "arbitrary")),
        name="sample_attn",
    )(page_table, q8, maskb, k_new, v_new, bias_s, *([cache_k] * _PPS), *([cache_v] * _PPS))


def _hgrn_chunk(q, k, v, logf, st, c):
    row = lax.broadcasted_iota(I32, (c, DK_B), 0)
    g = logf
    sh = 1
    while sh < c:
        g = g + jnp.where(row >= sh, pltpu.roll(g, sh, 0), 0.0)
        sh *= 2

    o = lax.dot_general((q * jnp.exp(g)).astype(BF16), st.astype(BF16), _NT, preferred_element_type=F32)

    if c > SUBLANES:
        rr = lax.broadcasted_iota(I32, (c, c), 0)
        cc = lax.broadcasted_iota(I32, (c, c), 1)
        a = jnp.zeros((c, c), F32)
        w = SUBLANES
        while w < c:
            nb = c // (2 * w)
            gb = jnp.broadcast_to(g.reshape(nb, 2 * w, DK_B)[:, w - 1:w, :], (nb, 2 * w, DK_B)).reshape(c, DK_B)
            right = (row % (2 * w)) >= w
            qt = jnp.where(right, q * jnp.exp(jnp.where(right, g - gb, 0.0)), 0.0)
            kt = jnp.where(right, 0.0, k * jnp.exp(jnp.where(right, 0.0, gb - g)))
            aw = lax.dot_general(qt.astype(BF16), kt.astype(BF16), _NT, preferred_element_type=F32)
            a = a + jnp.where((rr // (2 * w)) == (cc // (2 * w)), aw, 0.0)
            w *= 2
        o = o + jnp.dot(a.astype(BF16), v.astype(BF16), preferred_element_type=F32)

    o = o + jnp.sum(q * k, axis=1, keepdims=True) * v
    for d in range(1, SUBLANES):
        ok = (row % SUBLANES) >= d
        e = jnp.exp(jnp.where(ok, g - pltpu.roll(g, d, 0), 0.0))
        coef = jnp.sum(jnp.where(ok, q * pltpu.roll(k, d, 0) * e, 0.0), axis=1, keepdims=True)
        o = o + coef * pltpu.roll(v, d, 0)

    g_end = g[c - 1:c, :]
    kd = k * jnp.exp(g_end - g)
    st = st * jnp.exp(g_end) + lax.dot_general(v.astype(BF16), kd.astype(BF16), _TN, preferred_element_type=F32)
    return o, st


def _hgrn_kernel(q_ref, f_ref, v_ref, s0_ref, o_ref, s_ref, *, nb, t_len, c):
    for j in range(nb):
        def body(ci, st):
            r = pl.ds(pl.multiple_of(ci * c, c), c)
            f = f_ref[j, r, :]
            o, st = _hgrn_chunk(q_ref[j, r, :], 1.0 - f, v_ref[j, r, :], jnp.log(f), st, c)
            o_ref[j, r, :] = o
            return st
        st = lax.fori_loop(0, t_len // c, body, s0_ref[j].T)
        s_ref[j] = st.T


def _hgrn(q, f, v, s0, nb, c):
    nseq, t_len, _ = q.shape
    tok = pl.BlockSpec((nb, t_len, DK_B), lambda b, h: (b, 0, h))
    st = pl.BlockSpec((nb, None, DK_B, DV_B), lambda b, h: (b, h, 0, 0))
    return pl.pallas_call(
        functools.partial(_hgrn_kernel, nb=nb, t_len=t_len, c=c),
        out_shape=[jax.ShapeDtypeStruct((nseq, t_len, W_HGRN), F32),
                   jax.ShapeDtypeStruct((nseq, H_B, DK_B, DV_B), F32)],
        grid=(nseq // nb, H_B),
        in_specs=[tok, tok, tok, st],
        out_specs=[tok, st],
        compiler_params=_cp(("arbitrary", "arbitrary")),
        name="hgrn",
    )(q, f, v, s0)


def _merge_kernel(oa_ref, oh_ref, sg_ref, x_ref, gt_ref, beta_ref, gh_ref, bd_ref, w_ref, o_ref):
    oa = oa_ref[...]
    a = oa * lax.rsqrt(_group_mean_sq(oa, bd_ref[...]) + RMS_EPS) * beta_ref[...]
    oh = oh_ref[...]
    parts = []
    for h in range(H_B):
        sl = slice(h * DV_B, (h + 1) * DV_B)
        parts.append(_rms(oh[:, sl], gh_ref[:, sl]))
    r = jnp.concatenate(parts, axis=1) * sg_ref[...]
    y = (jnp.dot(a.astype(BF16), w_ref[:W_ATTN, :], preferred_element_type=F32)
         + jnp.dot(r.astype(BF16), w_ref[W_ATTN:, :], preferred_element_type=F32))
    o_ref[...] = x_ref[...] + gt_ref[...] * y


def _merge(oa, oh, sg, x, mod, beta, gh, bd, w16, tm):
    n = x.shape[0]
    row = lambda w: pl.BlockSpec((tm, w), lambda i: (i, 0))
    full = lambda a: pl.BlockSpec(a.shape, lambda i: (0,) * a.ndim)
    return pl.pallas_call(
        _merge_kernel,
        out_shape=jax.ShapeDtypeStruct((n, D_MODEL), F32),
        grid=(n // tm,),
        in_specs=[row(W_ATTN), row(W_HGRN), row(W_HGRN), row(D_MODEL), mod.spec(5, tm, 1),
                  full(beta), full(gh), full(bd), full(w16)],
        out_specs=row(D_MODEL),
        compiler_params=_cp(("arbitrary",)),
        name="merge",
    )(oa, oh, sg, x, mod.arr, beta, gh, bd, w16)


def _pad_tokens(a, value=0.0, to=_TP):
    return jnp.pad(a, ((0, 0), (0, to - a.shape[1]), (0, 0)), constant_values=value)


def kernel(x_prompt, x_sample, cache_k, cache_v, cache_kidx, state_hgrn, page_table, c_prompt, c_sample,
           w_ada, b_ada, g_ffn1, ffn1_w_gate, ffn1_w_up, ffn1_w_down, g_mix, w_in, g_q, g_k,
           beta_attn, g_hgrn, w_out, g_ffn2, ffn2_w_gate, ffn2_w_up, ffn2_w_down, rel_bias, lb_logits):
    assert w_ada.shape[0] == 1, "single-layer problem"
    nb_p, seq, _ = x_prompt.shape
    nb_s, t_dec, _ = x_sample.shape
    n_phys = cache_k.shape[1]
    n_pages = page_table.shape[1]
    past = n_pages * PAGE
    assert t_dec <= _TP and n_pages % _PPS == 0 and (past + LANES) % _SEL_CW == 0

    row2 = lambda a: a.reshape(1, -1)
    w_in0 = w_in[0]
    pad_w = jnp.zeros((D_MODEL, LANES - H_I), F32)
    wpack = jnp.concatenate(
        [w_in0[:, :2048], w_in0[:, 2048:2112], w_in0[:, 2048:2112], w_in0[:, 2112:2120], pad_w,
         w_in0[:, 2120:]], axis=1).astype(BF16)
    assert wpack.shape[1] == _N_PACK
    gq = row2(jnp.tile(g_q[0], H_A))
    gk = row2(jnp.tile(g_k[0], H_A))
    gidx = jnp.arange(W_ATTN) // D_HA
    bd = jnp.where(gidx[:, None] == gidx[None, :], 1.0 / D_HA, 0.0).astype(BF16)
    ffn1 = (row2(g_ffn1[0]), ffn1_w_gate[0].astype(BF16), ffn1_w_up[0].astype(BF16), ffn1_w_down[0].astype(BF16))
    ffn2 = (row2(g_ffn2[0]), ffn2_w_gate[0].astype(BF16), ffn2_w_up[0].astype(BF16), ffn2_w_down[0].astype(BF16))
    w_out16 = w_out[0].astype(BF16)

    n_c = nb_p + nb_s
    c_pad = (-n_c) % SUBLANES
    c_all = jnp.concatenate([c_prompt, c_sample, jnp.zeros((c_pad, D_MODEL), F32)], axis=0)
    mods = _ada(c_all, w_ada[0].astype(BF16), row2(b_ada[0]))
    bias_p, bias_s = _bias_tiles(rel_bias)

    def layer(x, mod, tm, tf, mixer):
        x1 = _ffn(x, mod, 0, *ffn1, tm, tf)
        proj = _mix(x1, mod, row2(g_mix[0]), wpack, gq, gk, lb_logits, bd, tm)
        oa, oh, sg, extras = mixer(proj)
        x2 = _merge(oa, oh, sg, x1, mod, row2(beta_attn[0]), row2(g_hgrn[0]), bd, w_out16, tm)
        return _ffn(x2, mod, 6, *ffn2, tm, tf), extras

    tm_p = 512
    mod_p = _Mod(mods[:, :nb_p].reshape(N_MOD, nb_p, 1, D_MODEL), False, seq // tm_p)

    def prompt_mixer(proj):
        qa16, ka, ka16, va, va16, qi16, ki, ki16, wi, qb, f, vb, sg = proj
        oa = _attn_prompt(qa16, qi16, wi, ki16, ka16, va16, bias_p, nb_p, seq)
        r3 = lambda a: a.reshape(nb_p, seq, W_HGRN)
        oh, st = _hgrn(r3(qb), r3(f), r3(vb), jnp.zeros((nb_p, H_B, DK_B, DV_B), F32), 1, 64)
        return oa, oh.reshape(nb_p * seq, W_HGRN), sg, (ka, va, ki[:, :D_I], st)

    yp, (kp, vp, ip, sp) = layer(x_prompt.reshape(nb_p * seq, D_MODEL), mod_p, tm_p, D_FF // 2, prompt_mixer)

    n_tok = nb_s * t_dec
    mod_s = _Mod(jnp.repeat(mods[:, nb_p:nb_p + nb_s], t_dec, axis=1), True)
    ck = cache_k[0].reshape(n_phys, PAGE, W_ATTN)
    cv = cache_v[0].reshape(n_phys, PAGE, W_ATTN)
    cki = cache_kidx[0]

    def sample_mixer(proj):
        qa16, ka, ka16, va, va16, qi16, ki, ki16, wi, qb, f, vb, sg = proj
        seq3 = lambda a: a.reshape(nb_s, t_dec, a.shape[-1])
        q_st = _pad_tokens(seq3(qi16)).reshape(nb_s, _TP, H_I, D_I).transpose(0, 2, 1, 3).reshape(nb_s, H_I * _TP, D_I)
        w_st = _pad_tokens(seq3(wi[:, :H_I])).transpose(0, 2, 1).reshape(nb_s, H_I * _TP, 1)
        w_b = jnp.broadcast_to(w_st, (nb_s, H_I * _TP, LANES))
        ki_new = _pad_tokens(seq3(ki16[:, :D_I]), to=_TK)
        scores = _sample_scores(page_table, q_st, w_b, ki_new, cki, n_pages)
        maskb = _sample_select(scores.reshape(nb_s * _TP, -1), past)
        oa8 = _sample_attn(page_table, _pad_tokens(seq3(qa16.astype(F32))), maskb,
                           _pad_tokens(seq3(ka16), to=_TK), _pad_tokens(seq3(va16), to=_TK), bias_s, ck, cv, n_pages)
        oa = oa8[:, :t_dec].reshape(n_tok, W_ATTN)
        oh8, st = _hgrn(_pad_tokens(seq3(qb)), _pad_tokens(seq3(f), 1.0), _pad_tokens(seq3(vb)),
                        state_hgrn[0], 8, _TP)
        return oa, oh8[:, :t_dec].reshape(n_tok, W_HGRN), sg, (ka, va, ki[:, :D_I], st)

    ys, (ks, vs, is_, ss) = layer(x_sample.reshape(n_tok, D_MODEL), mod_s, n_tok, D_FF // 2, sample_mixer)

    return (yp.reshape(nb_p, seq, D_MODEL).astype(x_prompt.dtype),
            ys.reshape(nb_s, t_dec, D_MODEL).astype(x_sample.dtype),
            kp.reshape(1, nb_p, seq, H_A, D_HA), vp.reshape(1, nb_p, seq, H_A, D_HA),
            ip.reshape(1, nb_p, seq, D_I), sp[None],
            ks.reshape(1, nb_s, t_dec, H_A, D_HA), vs.reshape(1, nb_s, t_dec, H_A, D_HA),
            is_.reshape(1, nb_s, t_dec, D_I), ss[None])
```

```python
import functools
import math

import jax
import jax.numpy as jnp
from jax import lax
from jax.experimental import pallas as pl
from jax.experimental.pallas import tpu as pltpu

F32 = jnp.float32
BF16 = jnp.bfloat16
I32 = jnp.int32

D_MODEL = 1024
W_ATTN = 512
W_HGRN = 512
D_HA = 64
H_A = 8
H_I = 8
D_I = 64
TOPK_MAX = 256
H_B = 4
DK_B = 128
DV_B = 128
D_FF = 2816
T5_BUCKETS = 32
T5_MAX_DIST = 128
RMS_EPS = 1e-6
N_MOD = 9
PAGE = 128

LANES = 128
SUBLANES = 8
NEG = -1e30
INT_MIN = -(2 ** 31)
VMEM_LIMIT = 56 * 1024 * 1024

_NT = (((1,), (1,)), ((), ()))
_TN = (((0,), (0,)), ((), ()))


def _cp(sem):
    return pltpu.CompilerParams(dimension_semantics=sem, vmem_limit_bytes=VMEM_LIMIT)


def _silu(x):
    return x / (1.0 + jnp.exp(-x))


def _sigmoid(x):
    return 1.0 / (1.0 + jnp.exp(-x))


def _rms(x, g):
    return x * lax.rsqrt(jnp.mean(x * x, axis=-1, keepdims=True) + RMS_EPS) * g


def _group_mean_sq(x, bd):
    xx = x * x
    hi = xx.astype(BF16)
    lo = (xx - hi.astype(F32)).astype(BF16)
    return (jnp.dot(hi, bd, preferred_element_type=F32)
            + jnp.dot(lo, bd, preferred_element_type=F32))


def _ada_kernel(c_ref, w_ref, b_ref, o_ref):
    a = _silu(c_ref[...]).astype(BF16)
    o_ref[...] = jnp.dot(a, w_ref[...], preferred_element_type=F32) + b_ref[...]


def _ada(c_all, w16, b):
    rows = c_all.shape[0]
    return pl.pallas_call(
        _ada_kernel,
        out_shape=jax.ShapeDtypeStruct((N_MOD, rows, D_MODEL), F32),
        grid=(N_MOD,),
        in_specs=[pl.BlockSpec((rows, D_MODEL), lambda j: (0, 0)),
                  pl.BlockSpec((D_MODEL, D_MODEL), lambda j: (0, j)),
                  pl.BlockSpec((1, D_MODEL), lambda j: (0, j))],
        out_specs=pl.BlockSpec((None, rows, D_MODEL), lambda j: (j, 0, 0)),
        compiler_params=_cp(("arbitrary",)),
        name="ada",
    )(c_all, w16, b)


class _Mod:
    def __init__(self, arr, per_token, tiles_per_seq=1):
        self.arr = arr
        self.per_token = per_token
        self.tiles_per_seq = tiles_per_seq

    def spec(self, j, tm, ngrid):
        if self.per_token:
            if ngrid == 1:
                return pl.BlockSpec((None, tm, D_MODEL), lambda i: (j, i, 0))
            return pl.BlockSpec((None, tm, D_MODEL), lambda i, k: (j, i, 0))
        tps = self.tiles_per_seq
        if ngrid == 1:
            return pl.BlockSpec((None, None, 1, D_MODEL), lambda i: (j, i // tps, 0, 0))
        return pl.BlockSpec((None, None, 1, D_MODEL), lambda i, k: (j, i // tps, 0, 0))


def _ffn_kernel(x_ref, sh_ref, sc_ref, gt_ref, g_ref, wg_ref, wu_ref, wd_ref, o_ref, h_sc, acc_sc):
    k = pl.program_id(1)

    @pl.when(k == 0)
    def _():
        h = _rms(x_ref[...], g_ref[...]) * (1.0 + sc_ref[...]) + sh_ref[...]
        h_sc[...] = h.astype(BF16)
        acc_sc[...] = jnp.zeros_like(acc_sc)

    h = h_sc[...]
    a = jnp.dot(h, wg_ref[...], preferred_element_type=F32)
    b = jnp.dot(h, wu_ref[...], preferred_element_type=F32)
    act = (_silu(a) * b).astype(BF16)
    acc_sc[...] += jnp.dot(act, wd_ref[...], preferred_element_type=F32)

    @pl.when(k == pl.num_programs(1) - 1)
    def _():
        o_ref[...] = x_ref[...] + 0.5 * gt_ref[...] * acc_sc[...]


def _ffn(x, mod, j0, g, wg, wu, wd, tm, tf):
    n = x.shape[0]
    nk = D_FF // tf
    return pl.pallas_call(
        _ffn_kernel,
        out_shape=jax.ShapeDtypeStruct((n, D_MODEL), F32),
        grid=(n // tm, nk),
        in_specs=[pl.BlockSpec((tm, D_MODEL), lambda i, k: (i, 0)),
                  mod.spec(j0, tm, 2), mod.spec(j0 + 1, tm, 2), mod.spec(j0 + 2, tm, 2),
                  pl.BlockSpec((1, D_MODEL), lambda i, k: (0, 0)),
                  pl.BlockSpec((D_MODEL, tf), lambda i, k: (0, k)),
                  pl.BlockSpec((D_MODEL, tf), lambda i, k: (0, k)),
                  pl.BlockSpec((tf, D_MODEL), lambda i, k: (k, 0))],
        out_specs=pl.BlockSpec((tm, D_MODEL), lambda i, k: (i, 0)),
        scratch_shapes=[pltpu.VMEM((tm, D_MODEL), BF16), pltpu.VMEM((tm, D_MODEL), F32)],
        compiler_params=_cp(("arbitrary", "arbitrary")),
        name="ffn",
    )(x, mod.arr, mod.arr, mod.arr, g, wg, wu, wd)


_C_QA, _C_KA, _C_VA, _C_QI = 0, 512, 1024, 1536
_C_KI, _C_WI = 2048, 2176
_C_QB, _C_FB, _C_IB, _C_GB = 2304, 2816, 3328, 3840
_N_PACK = 4352


def _mix_kernel(x_ref, sh_ref, sc_ref, g_ref, w_ref, gq_ref, gk_ref, lbl_ref, bd_ref,
                qa_o, ka_o, ka16_o, va_o, va16_o, qi_o, ki_o, ki16_o, wi_o, qb_o, f_o, vb_o, sg_o):
    h = (_rms(x_ref[...], g_ref[...]) * (1.0 + sc_ref[...]) + sh_ref[...]).astype(BF16)

    def proj(c0, width):
        return jnp.dot(h, w_ref[:, c0:c0 + width], preferred_element_type=F32)

    bd = bd_ref[...]
    qa = proj(_C_QA, W_ATTN)
    qa = qa * lax.rsqrt(_group_mean_sq(qa, bd) + RMS_EPS) * gq_ref[...]
    qa_o[...] = (qa * (D_HA ** -0.5)).astype(BF16)
    ka = proj(_C_KA, W_ATTN)
    ka = ka * lax.rsqrt(_group_mean_sq(ka, bd) + RMS_EPS) * gk_ref[...]
    ka_o[...] = ka
    ka16_o[...] = ka.astype(BF16)
    va = proj(_C_VA, W_ATTN)
    va_o[...] = va
    va16_o[...] = va.astype(BF16)
    qi_o[...] = (proj(_C_QI, H_I * D_I) * (D_I ** -0.5)).astype(BF16)
    ki = proj(_C_KI, LANES)
    ki_o[...] = ki
    ki16_o[...] = ki.astype(BF16)
    wi_o[...] = proj(_C_WI, LANES) * (H_I ** -0.5)
    qb_o[...] = _silu(proj(_C_QB, W_HGRN))
    l0 = lbl_ref[0:1, :]
    l1 = lbl_ref[1:2, :]
    mx = jnp.maximum(l0, l1)
    e0 = jnp.exp(l0 - mx)
    e1 = jnp.exp(l1 - mx)
    lb = e0 / (e0 + e1)
    f_o[...] = lb + (1.0 - lb) * _sigmoid(proj(_C_FB, W_HGRN))
    vb_o[...] = proj(_C_IB, W_HGRN)
    sg_o[...] = _silu(proj(_C_GB, W_HGRN))


def _mix(x, mod, g, wpack, gq, gk, lbl, bd, tm):
    n = x.shape[0]
    row = lambda w: pl.BlockSpec((tm, w), lambda i: (i, 0))
    full = lambda a: pl.BlockSpec(a.shape, lambda i: (0,) * a.ndim)
    outs = [(W_ATTN, BF16), (W_ATTN, F32), (W_ATTN, BF16), (W_ATTN, F32), (W_ATTN, BF16),
            (H_I * D_I, BF16), (LANES, F32), (LANES, BF16), (LANES, F32),
            (W_HGRN, F32), (W_HGRN, F32), (W_HGRN, F32), (W_HGRN, F32)]
    return pl.pallas_call(
        _mix_kernel,
        out_shape=[jax.ShapeDtypeStruct((n, w), dt) for w, dt in outs],
        grid=(n // tm,),
        in_specs=[row(D_MODEL), mod.spec(3, tm, 1), mod.spec(4, tm, 1), full(g), full(wpack),
                  full(gq), full(gk), full(lbl), full(bd)],
        out_specs=[row(w) for w, _ in outs],
        compiler_params=_cp(("arbitrary",)),
        name="mix",
    )(x, mod.arr, mod.arr, g, wpack, gq, gk, lbl, bd)


def _t5_bucket(dist):
    n = jnp.maximum(dist, 0)
    max_exact = T5_BUCKETS // 2
    nf = jnp.maximum(n, 1).astype(F32)
    large = max_exact + (jnp.log(nf / max_exact) / math.log(T5_MAX_DIST / max_exact)
                         * (T5_BUCKETS - max_exact)).astype(I32)
    large = jnp.minimum(large, T5_BUCKETS - 1)
    return jnp.where(n < max_exact, n, large)


def _bias_kernel(rb_ref, bp_ref, bs_ref):
    def lookup(bucket, h):
        val = jnp.zeros(bucket.shape, F32)
        for b in range(T5_BUCKETS):
            val = jnp.where(bucket == b, rb_ref[b, h], val)
        return val

    r = lax.broadcasted_iota(I32, (LANES, 2 * LANES), 0)
    c = lax.broadcasted_iota(I32, (LANES, 2 * LANES), 1)
    far_bucket = _t5_bucket(jnp.full((LANES, 2 * LANES), 2 * LANES, I32))
    for v, dist in enumerate((r - c, LANES + r - c)):
        bucket = _t5_bucket(dist)
        for h in range(H_A):
            bp_ref[v, h // 2, (h % 2) * LANES:(h % 2 + 1) * LANES, :] = lookup(bucket, h) - lookup(far_bucket, h)

    t = lax.broadcasted_iota(I32, (SUBLANES, LANES), 0)
    u = lax.broadcasted_iota(I32, (SUBLANES, LANES), 1)
    sdists = (PAGE + t - u, t - u, jnp.full((SUBLANES, LANES), 2 * LANES, I32))
    for m in range(3):
        bucket = _t5_bucket(sdists[m])
        for h in range(H_A):
            bs_ref[m, h * SUBLANES:(h + 1) * SUBLANES, :] = lookup(bucket, h)


def _bias_tiles(rel_bias):
    return pl.pallas_call(
        _bias_kernel,
        out_shape=[jax.ShapeDtypeStruct((2, H_A // 2, 2 * LANES, 2 * LANES), F32),
                   jax.ShapeDtypeStruct((3, H_A * SUBLANES, LANES), F32)],
        in_specs=[pl.BlockSpec(memory_space=pltpu.SMEM)],
        out_specs=[pl.BlockSpec(memory_space=pltpu.VMEM), pl.BlockSpec(memory_space=pltpu.VMEM)],
        name="t5_bias",
    )(rel_bias)


def _sortable_key(score):
    bits = pltpu.bitcast(score, I32)
    return jnp.where(bits < 0, INT_MIN - bits, bits)


def _topk_maskbias(keys_ref, maskb_ref, nch, rows, cw, n_sel, idx_bits, far_ref=None, far_limit=None):
    nt = cw // LANES
    lane = lax.broadcasted_iota(I32, (rows, LANES), 1)

    def count(indicator):
        def body(c, acc):
            k0 = pl.multiple_of(c * cw, cw)
            for j in range(nt):
                kk = keys_ref[:, pl.ds(k0 + j * LANES, LANES)]
                acc = acc + indicator(kk, k0 + j * LANES)
            return acc
        acc = lax.fori_loop(0, nch, body, jnp.zeros((rows, LANES), F32))
        return jnp.broadcast_to(jnp.sum(acc, axis=1, keepdims=True), (rows, LANES))

    def bisect(it, ans):
        cand = ans | jnp.left_shift(jnp.int32(1), 31 - it)
        cs = cand ^ INT_MIN
        cnt = count(lambda kk, p0: jnp.where(kk >= cs, 1.0, 0.0))
        return jnp.where(cnt >= n_sel, cand, ans)

    thr = lax.fori_loop(0, 32, bisect, jnp.zeros((rows, LANES), I32)) ^ INT_MIN
    need = n_sel - count(lambda kk, p0: jnp.where(kk > thr, 1.0, 0.0))
    n_eq = count(lambda kk, p0: jnp.where(kk == thr, 1.0, 0.0))
    has_tie = jnp.max(jnp.where(n_eq > need, 1.0, 0.0)) > 0.0

    def tie_search(_):
        def step(it, aj):
            cand = aj | jnp.left_shift(jnp.int32(1), idx_bits - 1 - it)
            cnt = count(lambda kk, p0: jnp.where(kk == thr, jnp.where((p0 + lane) < cand, 1.0, 0.0), 0.0))
            return jnp.where(cnt < need, cand, aj)
        return lax.fori_loop(0, idx_bits, step, jnp.zeros((rows, LANES), I32))

    cut = lax.cond(has_tie, tie_search, lambda _: jnp.full((rows, LANES), 2 ** 30, I32), 0)
    cut = jnp.where(thr == INT_MIN, -1, cut)

    def fin(c, _):
        k0 = pl.multiple_of(c * cw, cw)
        for j in range(nt):
            sl = pl.ds(k0 + j * LANES, LANES)
            kk = keys_ref[:, sl]
            kpos = k0 + j * LANES + lane
            tie = jnp.where(kpos <= cut, 0.0, NEG)
            mb = jnp.where(kk > thr, 0.0, jnp.where(kk == thr, tie, NEG))
            maskb_ref[:, sl] = mb
            if far_ref is not None:
                far_ref[:, sl] = jnp.where(kpos < far_limit, mb, NEG)
        return 0

    lax.fori_loop(0, nch, fin, 0)


def _softmax_step(carry, s, v16, v_is_transposed=False):
    m, l, acc = carry
    mn = jnp.maximum(m, jnp.max(s, axis=1, keepdims=True))
    alpha = jnp.exp(m - mn)
    p = jnp.exp(s - mn)
    l = alpha * l + jnp.sum(p, axis=1, keepdims=True)
    if v_is_transposed:
        pv = lax.dot_general(p.astype(BF16), v16, _NT, preferred_element_type=F32)
    else:
        pv = jnp.dot(p.astype(BF16), v16, preferred_element_type=F32)
    return mn, l, alpha * acc + pv


_QB = 128
_KC = 512


def _attn_prompt_kernel(qa_ref, qi_ref, wi_ref, ki_ref, ka_ref, va_ref, bias_ref, o_ref,
                        keys_sc, maskb_sc, far_sc, qs_sc, wb_sc, q2_sc, m_sc, l_sc, acc_sc):
    i = pl.program_id(1)
    lane = lax.broadcasted_iota(I32, (_QB, LANES), 1)
    lo = lane < D_I
    nch = i // (_KC // _QB) + 1

    for h in range(H_I):
        pair = qi_ref[:, (h // 2) * LANES:(h // 2 + 1) * LANES].astype(F32)
        keep = jnp.where(lo, pair, 0.0) if h % 2 == 0 else jnp.where(lo, 0.0, pair)
        qs_sc[h * _QB:(h + 1) * _QB, :] = keep.astype(BF16)
        wb_sc[h * _QB:(h + 1) * _QB, :] = jnp.broadcast_to(wi_ref[:, h:h + 1], (_QB, LANES))

    qpos = i * _QB + lax.broadcasted_iota(I32, (_QB, _KC), 0)

    def scores(c, _):
        k0 = pl.multiple_of(c * _KC, _KC)
        s = lax.dot_general(qs_sc[...], ki_ref[pl.ds(k0, _KC), :], _NT, preferred_element_type=F32)
        sc = jnp.zeros((_QB, _KC), F32)
        for h in range(H_I):
            wbh = wb_sc[h * _QB:(h + 1) * _QB, :]
            sc = sc + jnp.maximum(s[h * _QB:(h + 1) * _QB, :], 0.0) * jnp.concatenate([wbh] * (_KC // LANES), axis=1)
        kpos = k0 + lax.broadcasted_iota(I32, (_QB, _KC), 1)
        keys_sc[:, pl.ds(k0, _KC)] = jnp.where(kpos <= qpos, _sortable_key(sc), INT_MIN)
        return 0

    lax.fori_loop(0, nch, scores, 0)
    near0 = jnp.maximum(i - 1, 0) * _QB
    _topk_maskbias(keys_sc, maskb_sc, nch, _QB, _KC, TOPK_MAX, 12, far_ref=far_sc, far_limit=near0)

    npair = H_A // 2
    rows2 = 2 * _QB
    for p in range(npair):
        qp = qa_ref[:, p * LANES:(p + 1) * LANES].astype(F32)
        q2_sc[p * rows2:(p + 1) * rows2, :] = jnp.concatenate(
            [jnp.where(lo, qp, 0.0), jnp.where(lo, 0.0, qp)], axis=0).astype(BF16)
    m_sc[...] = jnp.full(m_sc.shape, NEG, F32)
    l_sc[...] = jnp.zeros_like(l_sc)
    acc_sc[...] = jnp.zeros_like(acc_sc)

    def update(p, k0, width, mask_ref, bias):
        rs = slice(p * rows2, (p + 1) * rows2)
        cs = slice(p * LANES, (p + 1) * LANES)
        s = lax.dot_general(q2_sc[rs, :], ka_ref[pl.ds(k0, width), cs], _NT, preferred_element_type=F32)
        mb = mask_ref[:, pl.ds(k0, width)]
        s = s + jnp.concatenate([mb, mb], axis=0)
        if bias is not None:
            s = s + bias
        m, l, acc = _softmax_step((m_sc[rs, 0:1], l_sc[rs, 0:1], acc_sc[rs, :]), s, va_ref[pl.ds(k0, width), cs])
        m_sc[rs, :] = jnp.broadcast_to(m, (rows2, LANES))
        l_sc[rs, :] = jnp.broadcast_to(l, (rows2, LANES))
        acc_sc[rs, :] = acc

    def far(c, _):
        k0 = pl.multiple_of(c * _KC, _KC)
        for p in range(npair):
            update(p, k0, _KC, far_sc, None)
        return 0

    lax.fori_loop(0, (near0 + _KC - 1) // _KC, far, 0)
    first = jnp.minimum(i, 1)
    for p in range(npair):
        update(p, pl.multiple_of(near0, _QB), 2 * _QB, maskb_sc, bias_ref[first, p])
        rs = slice(p * rows2, (p + 1) * rows2)
        o = acc_sc[rs, :] / l_sc[rs, 0:1]
        o_ref[:, p * LANES:(p + 1) * LANES] = jnp.where(lo, o[:_QB], o[_QB:])


def _attn_prompt(qa16, qi16, wi, ki16, ka16, va16, bias_p, nbatch, seq):
    nq = seq // _QB
    blk = lambda w: pl.BlockSpec((_QB, w), lambda b, i: (b * nq + i, 0))
    per_b = lambda w: pl.BlockSpec((seq, w), lambda b, i: (b, 0))
    return pl.pallas_call(
        _attn_prompt_kernel,
        out_shape=jax.ShapeDtypeStruct((nbatch * seq, W_ATTN), F32),
        grid=(nbatch, nq),
        in_specs=[blk(W_ATTN), blk(H_I * D_I), blk(LANES), per_b(LANES), per_b(W_ATTN), per_b(W_ATTN),
                  pl.BlockSpec(bias_p.shape, lambda b, i: (0, 0, 0, 0))],
        out_specs=blk(W_ATTN),
        scratch_shapes=[pltpu.VMEM((_QB, seq), I32), pltpu.VMEM((_QB, seq), F32), pltpu.VMEM((_QB, seq), F32),
                        pltpu.VMEM((H_I * _QB, LANES), BF16), pltpu.VMEM((H_I * _QB, LANES), F32),
                        pltpu.VMEM((H_A * _QB, LANES), BF16), pltpu.VMEM((H_A * _QB, LANES), F32),
                        pltpu.VMEM((H_A * _QB, LANES), F32), pltpu.VMEM((H_A * _QB, LANES), F32)],
        compiler_params=_cp(("arbitrary", "arbitrary")),
        name="attn_prompt",
    )(qa16, qi16, wi, ki16, ka16, va16, bias_p)


_PPS = 16
_TP = SUBLANES
_TK = 2 * SUBLANES


def _sample_scores_kernel(pt_ref, q_ref, w_ref, kn_ref, *rest):
    pages, o_ref = rest[:_PPS], rest[_PPS]
    s_id = pl.program_id(1)
    q = q_ref[...]

    def head_sum(s):
        n = s.shape[1]
        out = jnp.zeros((_TP, n), F32)
        for h in range(H_I):
            wh = w_ref[h * _TP:(h + 1) * _TP, :]
            out = out + jnp.maximum(s[h * _TP:(h + 1) * _TP, :], 0.0) * jnp.concatenate([wh] * (n // LANES), axis=1)
        return out

    kc = jnp.concatenate([pg[...] for pg in pages], axis=1).astype(BF16)
    s = jnp.dot(q, kc, preferred_element_type=F32)
    o_ref[:, pl.ds(pl.multiple_of(s_id * (_PPS * PAGE), _PPS * PAGE), _PPS * PAGE)] = head_sum(s)

    @pl.when(s_id == pl.num_programs(1) - 1)
    def _():
        kn = jnp.concatenate([kn_ref[...], jnp.zeros((LANES - _TK, D_I), BF16)], axis=0)
        sn = lax.dot_general(q, kn, _NT, preferred_element_type=F32)
        o_ref[:, o_ref.shape[1] - LANES:] = head_sum(sn)


def _sample_scores(page_table, q_st, w_b, ki_new, cache_kidx, n_pages):
    nseq = q_st.shape[0]
    steps = n_pages // _PPS
    lpad = n_pages * PAGE + LANES
    page_specs = [pl.BlockSpec((None, D_I, PAGE), functools.partial(
        lambda b, s, pt, p: (pt[b, s * _PPS + p], 0, 0), p=p)) for p in range(_PPS)]
    return pl.pallas_call(
        _sample_scores_kernel,
        out_shape=jax.ShapeDtypeStruct((nseq, _TP, lpad), F32),
        grid_spec=pltpu.PrefetchScalarGridSpec(
            num_scalar_prefetch=1, grid=(nseq, steps),
            in_specs=[pl.BlockSpec((None, H_I * _TP, D_I), lambda b, s, pt: (b, 0, 0)),
                      pl.BlockSpec((None, H_I * _TP, LANES), lambda b, s, pt: (b, 0, 0)),
                      pl.BlockSpec((None, _TK, D_I), lambda b, s, pt: (b, 0, 0))] + page_specs,
            out_specs=pl.BlockSpec((None, _TP, lpad), lambda b, s, pt: (b, 0, 0))),
        compiler_params=_cp(("arbitrary", "arbitrary")),
        name="sample_scores",
    )(page_table, q_st, w_b, ki_new, *([cache_kidx] * _PPS))


_SEL_ROWS = 128
_SEL_CW = 640


def _sample_select_kernel(s_ref, o_ref, keys_sc, *, past):
    lpad = s_ref.shape[1]
    t = lax.broadcasted_iota(I32, (_SEL_ROWS, _SEL_CW), 0) % _TP
    for c in range(lpad // _SEL_CW):
        sl = slice(c * _SEL_CW, (c + 1) * _SEL_CW)
        kpos = c * _SEL_CW + lax.broadcasted_iota(I32, (_SEL_ROWS, _SEL_CW), 1)
        keys_sc[:, sl] = jnp.where(kpos <= past + t, _sortable_key(s_ref[:, sl]), INT_MIN)
    _topk_maskbias(keys_sc, o_ref, lpad // _SEL_CW, _SEL_ROWS, _SEL_CW, TOPK_MAX, 14)


def _sample_select(scores2d, past):
    rows, lpad = scores2d.shape
    return pl.pallas_call(
        functools.partial(_sample_select_kernel, past=past),
        out_shape=jax.ShapeDtypeStruct((rows, lpad), F32),
        grid=(rows // _SEL_ROWS,),
        in_specs=[pl.BlockSpec((_SEL_ROWS, lpad), lambda g: (g, 0))],
        out_specs=pl.BlockSpec((_SEL_ROWS, lpad), lambda g: (g, 0)),
        scratch_shapes=[pltpu.VMEM((_SEL_ROWS, lpad), I32)],
        compiler_params=_cp(("arbitrary",)),
        name="sample_select",
    )(scores2d)


def _sample_attn_kernel(pt_ref, q_ref, mb_ref, kn_ref, vn_ref, bias_ref, *rest):
    kp, vp = rest[:_PPS], rest[_PPS:2 * _PPS]
    o_ref, qblk_sc, m_sc, l_sc, acc_sc = rest[2 * _PPS:]
    s_id = pl.program_id(1)
    last = pl.num_programs(1) - 1
    rows = H_A * _TP
    span = _PPS * PAGE

    @pl.when(s_id == 0)
    def _():
        q = jnp.concatenate([q_ref[...]] * H_A, axis=0)
        r = lax.broadcasted_iota(I32, (rows, W_ATTN), 0) // _TP
        c = lax.broadcasted_iota(I32, (rows, W_ATTN), 1) // D_HA
        qblk_sc[...] = jnp.where(r == c, q, 0.0).astype(BF16)
        m_sc[...] = jnp.full(m_sc.shape, NEG, F32)
        l_sc[...] = jnp.zeros_like(l_sc)
        acc_sc[...] = jnp.zeros_like(acc_sc)

    def update(s, v16, v_is_transposed):
        carry = (m_sc[:, 0:1], l_sc[:, 0:1], acc_sc[...])
        m, l, acc = _softmax_step(carry, s, v16, v_is_transposed)
        m_sc[...] = jnp.broadcast_to(m, m_sc.shape)
        l_sc[...] = jnp.broadcast_to(l, l_sc.shape)
        acc_sc[...] = acc

    k16 = jnp.concatenate([pg[...] for pg in kp], axis=1).astype(BF16)
    v16 = jnp.concatenate([pg[...] for pg in vp], axis=1).astype(BF16)
    s = jnp.dot(qblk_sc[...], k16, preferred_element_type=F32)
    mb = mb_ref[:, pl.ds(pl.multiple_of(s_id * span, span), span)]
    last_tile = jnp.where(s_id == last, bias_ref[0], bias_ref[2])
    bias = jnp.concatenate([bias_ref[2]] * (_PPS - 1) + [last_tile], axis=1)
    update(s + jnp.concatenate([mb] * H_A, axis=0) + bias, v16, True)

    @pl.when(s_id == last)
    def _():
        pad = jnp.zeros((LANES - _TK, W_ATTN), BF16)
        kn = jnp.concatenate([kn_ref[...], pad], axis=0)
        vn = jnp.concatenate([vn_ref[...], pad], axis=0)
        sn = lax.dot_general(qblk_sc[...], kn, _NT, preferred_element_type=F32)
        mbn = mb_ref[:, mb_ref.shape[1] - LANES:]
        update(sn + jnp.concatenate([mbn] * H_A, axis=0) + bias_ref[1], vn, False)
        o = acc_sc[...] / l_sc[:, 0:1]
        c = lax.broadcasted_iota(I32, (_TP, W_ATTN), 1) // D_HA
        out = jnp.zeros((_TP, W_ATTN), F32)
        for h in range(H_A):
            out = jnp.where(c == h, o[h * _TP:(h + 1) * _TP, :], out)
        o_ref[...] = out


def _sample_attn(page_table, q8, maskb, k_new, v_new, bias_s, cache_k, cache_v, n_pages):
    nseq = q8.shape[0]
    steps = n_pages // _PPS
    lpad = maskb.shape[1]
    pspec = lambda: [pl.BlockSpec((None, W_ATTN, PAGE), functools.partial(
        lambda b, s, pt, p: (pt[b, s * _PPS + p], 0, 0), p=p)) for p in range(_PPS)]
    seq3 = lambda t: pl.BlockSpec((None, t, W_ATTN), lambda b, s, pt: (b, 0, 0))
    return pl.pallas_call(
        _sample_attn_kernel,
        out_shape=jax.ShapeDtypeStruct((nseq, _TP, W_ATTN), F32),
        grid_spec=pltpu.PrefetchScalarGridSpec(
            num_scalar_prefetch=1, grid=(nseq, steps),
            in_specs=[seq3(_TP), pl.BlockSpec((_TP, lpad), lambda b, s, pt: (b, 0)),
                      seq3(_TK), seq3(_TK),
                      pl.BlockSpec(bias_s.shape, lambda b, s, pt: (0, 0, 0))] + pspec() + pspec(),
            out_specs=seq3(_TP),
            scratch_shapes=[pltpu.VMEM((H_A * _TP, W_ATTN), BF16), pltpu.VMEM((H_A * _TP, LANES), F32),
                            pltpu.VMEM((H_A * _TP, LANES), F32), pltpu.VMEM((H_A * _TP, W_ATTN), F32)]),
        compiler_params=_cp(("arbitrary", "arbitrary")),
        name="sample_attn",
    )(page_table, q8, maskb, k_new, v_new, bias_s, *([cache_k] * _PPS), *([cache_v] * _PPS))


def _hgrn_chunk(q, k, v, logf, st, c):
    row = lax.broadcasted_iota(I32, (c, DK_B), 0)
    g = logf
    sh = 1
    while sh < c:
        g = g + jnp.where(row >= sh, pltpu.roll(g, sh, 0), 0.0)
        sh *= 2

    o = lax.dot_general((q * jnp.exp(g)).astype(BF16), st.astype(BF16), _NT, preferred_element_type=F32)

    if c > SUBLANES:
        rr = lax.broadcasted_iota(I32, (c, c), 0)
        cc = lax.broadcasted_iota(I32, (c, c), 1)
        a = jnp.zeros((c, c), F32)
        w = SUBLANES
        while w < c:
            nb = c // (2 * w)
            gb = jnp.broadcast_to(g.reshape(nb, 2 * w, DK_B)[:, w - 1:w, :], (nb, 2 * w, DK_B)).reshape(c, DK_B)
            right = (row % (2 * w)) >= w
            qt = jnp.where(right, q * jnp.exp(jnp.where(right, g - gb, 0.0)), 0.0)
            kt = jnp.where(right, 0.0, k * jnp.exp(jnp.where(right, 0.0, gb - g)))
            aw = lax.dot_general(qt.astype(BF16), kt.astype(BF16), _NT, preferred_element_type=F32)
            a = a + jnp.where((rr // (2 * w)) == (cc // (2 * w)), aw, 0.0)
            w *= 2
        o = o + jnp.dot(a.astype(BF16), v.astype(BF16), preferred_element_type=F32)

    o = o + jnp.sum(q * k, axis=1, keepdims=True) * v
    for d in range(1, SUBLANES):
        ok = (row % SUBLANES) >= d
        e = jnp.exp(jnp.where(ok, g - pltpu.roll(g, d, 0), 0.0))
        coef = jnp.sum(jnp.where(ok, q * pltpu.roll(k, d, 0) * e, 0.0), axis=1, keepdims=True)
        o = o + coef * pltpu.roll(v, d, 0)

    g_end = g[c - 1:c, :]
    kd = k * jnp.exp(g_end - g)
    st = st * jnp.exp(g_end) + lax.dot_general(v.astype(BF16), kd.astype(BF16), _TN, preferred_element_type=F32)
    return o, st


def _hgrn_kernel(q_ref, f_ref, v_ref, s0_ref, o_ref, s_ref, *, nb, t_len, c):
    for j in range(nb):
        def body(ci, st):
            r = pl.ds(pl.multiple_of(ci * c, c), c)
            f = f_ref[j, r, :]
            o, st = _hgrn_chunk(q_ref[j, r, :], 1.0 - f, v_ref[j, r, :], jnp.log(f), st, c)
            o_ref[j, r, :] = o
            return st
        st = lax.fori_loop(0, t_len // c, body, s0_ref[j].T)
        s_ref[j] = st.T


def _hgrn(q, f, v, s0, nb, c):
    nseq, t_len, _ = q.shape
    tok = pl.BlockSpec((nb, t_len, DK_B), lambda b, h: (b, 0, h))
    st = pl.BlockSpec((nb, None, DK_B, DV_B), lambda b, h: (b, h, 0, 0))
    return pl.pallas_call(
        functools.partial(_hgrn_kernel, nb=nb, t_len=t_len, c=c),
        out_shape=[jax.ShapeDtypeStruct((nseq, t_len, W_HGRN), F32),
                   jax.ShapeDtypeStruct((nseq, H_B, DK_B, DV_B), F32)],
        grid=(nseq // nb, H_B),
        in_specs=[tok, tok, tok, st],
        out_specs=[tok, st],
        compiler_params=_cp(("arbitrary", "arbitrary")),
        name="hgrn",
    )(q, f, v, s0)


def _merge_kernel(oa_ref, oh_ref, sg_ref, x_ref, gt_ref, beta_ref, gh_ref, bd_ref, w_ref, o_ref):
    oa = oa_ref[...]
    a = oa * lax.rsqrt(_group_mean_sq(oa, bd_ref[...]) + RMS_EPS) * beta_ref[...]
    oh = oh_ref[...]
    parts = []
    for h in range(H_B):
        sl = slice(h * DV_B, (h + 1) * DV_B)
        parts.append(_rms(oh[:, sl], gh_ref[:, sl]))
    r = jnp.concatenate(parts, axis=1) * sg_ref[...]
    y = (jnp.dot(a.astype(BF16), w_ref[:W_ATTN, :], preferred_element_type=F32)
         + jnp.dot(r.astype(BF16), w_ref[W_ATTN:, :], preferred_element_type=F32))
    o_ref[...] = x_ref[...] + gt_ref[...] * y


def _merge(oa, oh, sg, x, mod, beta, gh, bd, w16, tm):
    n = x.shape[0]
    row = lambda w: pl.BlockSpec((tm, w), lambda i: (i, 0))
    full = lambda a: pl.BlockSpec(a.shape, lambda i: (0,) * a.ndim)
    return pl.pallas_call(
        _merge_kernel,
        out_shape=jax.ShapeDtypeStruct((n, D_MODEL), F32),
        grid=(n // tm,),
        in_specs=[row(W_ATTN), row(W_HGRN), row(W_HGRN), row(D_MODEL), mod.spec(5, tm, 1),
                  full(beta), full(gh), full(bd), full(w16)],
        out_specs=row(D_MODEL),
        compiler_params=_cp(("arbitrary",)),
        name="merge",
    )(oa, oh, sg, x, mod.arr, beta, gh, bd, w16)


def _pad_tokens(a, value=0.0, to=_TP):
    return jnp.pad(a, ((0, 0), (0, to - a.shape[1]), (0, 0)), constant_values=value)


def kernel(x_prompt, x_sample, cache_k, cache_v, cache_kidx, state_hgrn, page_table, c_prompt, c_sample,
           w_ada, b_ada, g_ffn1, ffn1_w_gate, ffn1_w_up, ffn1_w_down, g_mix, w_in, g_q, g_k,
           beta_attn, g_hgrn, w_out, g_ffn2, ffn2_w_gate, ffn2_w_up, ffn2_w_down, rel_bias, lb_logits):
    assert w_ada.shape[0] == 1, "single-layer problem"
    nb_p, seq, _ = x_prompt.shape
    nb_s, t_dec, _ = x_sample.shape
    n_phys = cache_k.shape[1]
    n_pages = page_table.shape[1]
    past = n_pages * PAGE
    assert t_dec <= _TP and n_pages % _PPS == 0 and (past + LANES) % _SEL_CW == 0

    row2 = lambda a: a.reshape(1, -1)
    w_in0 = w_in[0]
    pad_w = jnp.zeros((D_MODEL, LANES - H_I), F32)
    wpack = jnp.concatenate(
        [w_in0[:, :2048], w_in0[:, 2048:2112], w_in0[:, 2048:2112], w_in0[:, 2112:2120], pad_w,
         w_in0[:, 2120:]], axis=1).astype(BF16)
    assert wpack.shape[1] == _N_PACK
    gq = row2(jnp.tile(g_q[0], H_A))
    gk = row2(jnp.tile(g_k[0], H_A))
    gidx = jnp.arange(W_ATTN) // D_HA
    bd = jnp.where(gidx[:, None] == gidx[None, :], 1.0 / D_HA, 0.0).astype(BF16)
    ffn1 = (row2(g_ffn1[0]), ffn1_w_gate[0].astype(BF16), ffn1_w_up[0].astype(BF16), ffn1_w_down[0].astype(BF16))
    ffn2 = (row2(g_ffn2[0]), ffn2_w_gate[0].astype(BF16), ffn2_w_up[0].astype(BF16), ffn2_w_down[0].astype(BF16))
    w_out16 = w_out[0].astype(BF16)

    n_c = nb_p + nb_s
    c_pad = (-n_c) % SUBLANES
    c_all = jnp.concatenate([c_prompt, c_sample, jnp.zeros((c_pad, D_MODEL), F32)], axis=0)
    mods = _ada(c_all, w_ada[0].astype(BF16), row2(b_ada[0]))
    bias_p, bias_s = _bias_tiles(rel_bias)

    def layer(x, mod, tm, tf, mixer):
        x1 = _ffn(x, mod, 0, *ffn1, tm, tf)
        proj = _mix(x1, mod, row2(g_mix[0]), wpack, gq, gk, lb_logits, bd, tm)
        oa, oh, sg, extras = mixer(proj)
        x2 = _merge(oa, oh, sg, x1, mod, row2(beta_attn[0]), row2(g_hgrn[0]), bd, w_out16, tm)
        return _ffn(x2, mod, 6, *ffn2, tm, tf), extras

    tm_p = 512
    mod_p = _Mod(mods[:, :nb_p].reshape(N_MOD, nb_p, 1, D_MODEL), False, seq // tm_p)

    def prompt_mixer(proj):
        qa16, ka, ka16, va, va16, qi16, ki, ki16, wi, qb, f, vb, sg = proj
        oa = _attn_prompt(qa16, qi16, wi, ki16, ka16, va16, bias_p, nb_p, seq)
        r3 = lambda a: a.reshape(nb_p, seq, W_HGRN)
        oh, st = _hgrn(r3(qb), r3(f), r3(vb), jnp.zeros((nb_p, H_B, DK_B, DV_B), F32), 1, 64)
        return oa, oh.reshape(nb_p * seq, W_HGRN), sg, (ka, va, ki[:, :D_I], st)

    yp, (kp, vp, ip, sp) = layer(x_prompt.reshape(nb_p * seq, D_MODEL), mod_p, tm_p, D_FF // 2, prompt_mixer)

    n_tok = nb_s * t_dec
    mod_s = _Mod(jnp.repeat(mods[:, nb_p:nb_p + nb_s], t_dec, axis=1), True)
    ck = jnp.transpose(cache_k[0], (0, 2, 3, 1)).reshape(n_phys, W_ATTN, PAGE)
    cv = jnp.transpose(cache_v[0], (0, 2, 3, 1)).reshape(n_phys, W_ATTN, PAGE)
    cki = jnp.transpose(cache_kidx[0], (0, 2, 1))

    def sample_mixer(proj):
        qa16, ka, ka16, va, va16, qi16, ki, ki16, wi, qb, f, vb, sg = proj
        seq3 = lambda a: a.reshape(nb_s, t_dec, a.shape[-1])
        q_st = _pad_tokens(seq3(qi16)).reshape(nb_s, _TP, H_I, D_I).transpose(0, 2, 1, 3).reshape(nb_s, H_I * _TP, D_I)
        w_st = _pad_tokens(seq3(wi[:, :H_I])).transpose(0, 2, 1).reshape(nb_s, H_I * _TP, 1)
        w_b = jnp.broadcast_to(w_st, (nb_s, H_I * _TP, LANES))
        ki_new = _pad_tokens(seq3(ki16[:, :D_I]), to=_TK)
        scores = _sample_scores(page_table, q_st, w_b, ki_new, cki, n_pages)
        maskb = _sample_select(scores.reshape(nb_s * _TP, -1), past)
        oa8 = _sample_attn(page_table, _pad_tokens(seq3(qa16.astype(F32))), maskb,
                           _pad_tokens(seq3(ka16), to=_TK), _pad_tokens(seq3(va16), to=_TK), bias_s, ck, cv, n_pages)
        oa = oa8[:, :t_dec].reshape(n_tok, W_ATTN)
        oh8, st = _hgrn(_pad_tokens(seq3(qb)), _pad_tokens(seq3(f), 1.0), _pad_tokens(seq3(vb)),
                        state_hgrn[0], 8, _TP)
        return oa, oh8[:, :t_dec].reshape(n_tok, W_HGRN), sg, (ka, va, ki[:, :D_I], st)

    ys, (ks, vs, is_, ss) = layer(x_sample.reshape(n_tok, D_MODEL), mod_s, n_tok, D_FF // 2, sample_mixer)

    return (yp.reshape(nb_p, seq, D_MODEL).astype(x_prompt.dtype),
            ys.reshape(nb_s, t_dec, D_MODEL).astype(x_sample.dtype),
            kp.reshape(1, nb_p, seq, H_A, D_HA), vp.reshape(1, nb_p, seq, H_A, D_HA),
            ip.reshape(1, nb_p, seq, D_I), sp[None],
            ks.reshape(1, nb_s, t_dec, H_A, D_HA), vs.reshape(1, nb_s, t_dec, H_A, D_HA),
            is_.reshape(1, nb_s, t_dec, D_I), ss[None])
```

```python
import functools
import math

import jax
import jax.numpy as jnp
from jax import lax
from jax.experimental import pallas as pl
from jax.experimental.pallas import tpu as pltpu

F32 = jnp.float32
BF16 = jnp.bfloat16
I32 = jnp.int32

D_MODEL = 1024
W_ATTN = 512
W_HGRN = 512
D_HA = 64
H_A = 8
H_I = 8
D_I = 64
TOPK_MAX = 256
H_B = 4
DK_B = 128
DV_B = 128
D_FF = 2816
T5_BUCKETS = 32
T5_MAX_DIST = 128
RMS_EPS = 1e-6
N_MOD = 9
PAGE = 128

LANES = 128
SUBLANES = 8
NEG = -1e30
INT_MIN = -(2 ** 31)
VMEM_LIMIT = 56 * 1024 * 1024

_NT = (((1,), (1,)), ((), ()))
_TN = (((0,), (0,)), ((), ()))


def _cp(sem):
    return pltpu.CompilerParams(dimension_semantics=sem, vmem_limit_bytes=VMEM_LIMIT)


def _silu(x):
    return x / (1.0 + jnp.exp(-x))


def _sigmoid(x):
    return 1.0 / (1.0 + jnp.exp(-x))


def _rms(x, g):
    return x * lax.rsqrt(jnp.mean(x * x, axis=-1, keepdims=True) + RMS_EPS) * g


def _group_mean_sq(x, bd):
    xx = x * x
    hi = xx.astype(BF16)
    lo = (xx - hi.astype(F32)).astype(BF16)
    return (jnp.dot(hi, bd, preferred_element_type=F32)
            + jnp.dot(lo, bd, preferred_element_type=F32))


def _ada_kernel(c_ref, w_ref, b_ref, o_ref):
    a = _silu(c_ref[...]).astype(BF16)
    o_ref[...] = jnp.dot(a, w_ref[...], preferred_element_type=F32) + b_ref[...]


def _ada(c_all, w16, b):
    rows = c_all.shape[0]
    return pl.pallas_call(
        _ada_kernel,
        out_shape=jax.ShapeDtypeStruct((N_MOD, rows, D_MODEL), F32),
        grid=(N_MOD,),
        in_specs=[pl.BlockSpec((rows, D_MODEL), lambda j: (0, 0)),
                  pl.BlockSpec((D_MODEL, D_MODEL), lambda j: (0, j)),
                  pl.BlockSpec((1, D_MODEL), lambda j: (0, j))],
        out_specs=pl.BlockSpec((None, rows, D_MODEL), lambda j: (j, 0, 0)),
        compiler_params=_cp(("arbitrary",)),
        name="ada",
    )(c_all, w16, b)


class _Mod:
    def __init__(self, arr, per_token, tiles_per_seq=1):
        self.arr = arr
        self.per_token = per_token
        self.tiles_per_seq = tiles_per_seq

    def spec(self, j, tm, ngrid):
        if self.per_token:
            if ngrid == 1:
                return pl.BlockSpec((None, tm, D_MODEL), lambda i: (j, i, 0))
            return pl.BlockSpec((None, tm, D_MODEL), lambda i, k: (j, i, 0))
        tps = self.tiles_per_seq
        if ngrid == 1:
            return pl.BlockSpec((None, None, 1, D_MODEL), lambda i: (j, i // tps, 0, 0))
        return pl.BlockSpec((None, None, 1, D_MODEL), lambda i, k: (j, i // tps, 0, 0))


def _ffn_kernel(x_ref, sh_ref, sc_ref, gt_ref, g_ref, wg_ref, wu_ref, wd_ref, o_ref, h_sc, acc_sc):
    k = pl.program_id(1)

    @pl.when(k == 0)
    def _():
        h = _rms(x_ref[...], g_ref[...]) * (1.0 + sc_ref[...]) + sh_ref[...]
        h_sc[...] = h.astype(BF16)
        acc_sc[...] = jnp.zeros_like(acc_sc)

    h = h_sc[...]
    a = jnp.dot(h, wg_ref[...], preferred_element_type=F32)
    b = jnp.dot(h, wu_ref[...], preferred_element_type=F32)
    act = (_silu(a) * b).astype(BF16)
    acc_sc[...] += jnp.dot(act, wd_ref[...], preferred_element_type=F32)

    @pl.when(k == pl.num_programs(1) - 1)
    def _():
        o_ref[...] = x_ref[...] + 0.5 * gt_ref[...] * acc_sc[...]


def _ffn(x, mod, j0, g, wg, wu, wd, tm, tf):
    n = x.shape[0]
    nk = D_FF // tf
    return pl.pallas_call(
        _ffn_kernel,
        out_shape=jax.ShapeDtypeStruct((n, D_MODEL), F32),
        grid=(n // tm, nk),
        in_specs=[pl.BlockSpec((tm, D_MODEL), lambda i, k: (i, 0)),
                  mod.spec(j0, tm, 2), mod.spec(j0 + 1, tm, 2), mod.spec(j0 + 2, tm, 2),
                  pl.BlockSpec((1, D_MODEL), lambda i, k: (0, 0)),
                  pl.BlockSpec((D_MODEL, tf), lambda i, k: (0, k)),
                  pl.BlockSpec((D_MODEL, tf), lambda i, k: (0, k)),
                  pl.BlockSpec((tf, D_MODEL), lambda i, k: (k, 0))],
        out_specs=pl.BlockSpec((tm, D_MODEL), lambda i, k: (i, 0)),
        scratch_shapes=[pltpu.VMEM((tm, D_MODEL), BF16), pltpu.VMEM((tm, D_MODEL), F32)],
        compiler_params=_cp(("arbitrary", "arbitrary")),
        name="ffn",
    )(x, mod.arr, mod.arr, mod.arr, g, wg, wu, wd)


_C_QA, _C_KA, _C_VA, _C_QI = 0, 512, 1024, 1536
_C_KI, _C_WI = 2048, 2176
_C_QB, _C_FB, _C_IB, _C_GB = 2304, 2816, 3328, 3840
_N_PACK = 4352


def _mix_kernel(x_ref, sh_ref, sc_ref, g_ref, w_ref, gq_ref, gk_ref, lbl_ref, bd_ref,
                qa_o, ka_o, ka16_o, va_o, va16_o, qi_o, ki_o, ki16_o, wi_o, qb_o, f_o, vb_o, sg_o):
    h = (_rms(x_ref[...], g_ref[...]) * (1.0 + sc_ref[...]) + sh_ref[...]).astype(BF16)

    def proj(c0, width):
        return jnp.dot(h, w_ref[:, c0:c0 + width], preferred_element_type=F32)

    bd = bd_ref[...]
    qa = proj(_C_QA, W_ATTN)
    qa = qa * lax.rsqrt(_group_mean_sq(qa, bd) + RMS_EPS) * gq_ref[...]
    qa_o[...] = (qa * (D_HA ** -0.5)).astype(BF16)
    ka = proj(_C_KA, W_ATTN)
    ka = ka * lax.rsqrt(_group_mean_sq(ka, bd) + RMS_EPS) * gk_ref[...]
    ka_o[...] = ka
    ka16_o[...] = ka.astype(BF16)
    va = proj(_C_VA, W_ATTN)
    va_o[...] = va
    va16_o[...] = va.astype(BF16)
    qi_o[...] = (proj(_C_QI, H_I * D_I) * (D_I ** -0.5)).astype(BF16)
    ki = proj(_C_KI, LANES)
    ki_o[...] = ki
    ki16_o[...] = ki.astype(BF16)
    wi_o[...] = proj(_C_WI, LANES) * (H_I ** -0.5)
    qb_o[...] = _silu(proj(_C_QB, W_HGRN))
    l0 = lbl_ref[0:1, :]
    l1 = lbl_ref[1:2, :]
    mx = jnp.maximum(l0, l1)
    e0 = jnp.exp(l0 - mx)
    e1 = jnp.exp(l1 - mx)
    lb = e0 / (e0 + e1)
    f_o[...] = lb + (1.0 - lb) * _sigmoid(proj(_C_FB, W_HGRN))
    vb_o[...] = proj(_C_IB, W_HGRN)
    sg_o[...] = _silu(proj(_C_GB, W_HGRN))


def _mix(x, mod, g, wpack, gq, gk, lbl, bd, tm):
    n = x.shape[0]
    row = lambda w: pl.BlockSpec((tm, w), lambda i: (i, 0))
    full = lambda a: pl.BlockSpec(a.shape, lambda i: (0,) * a.ndim)
    outs = [(W_ATTN, BF16), (W_ATTN, F32), (W_ATTN, BF16), (W_ATTN, F32), (W_ATTN, BF16),
            (H_I * D_I, BF16), (LANES, F32), (LANES, BF16), (LANES, F32),
            (W_HGRN, F32), (W_HGRN, F32), (W_HGRN, F32), (W_HGRN, F32)]
    return pl.pallas_call(
        _mix_kernel,
        out_shape=[jax.ShapeDtypeStruct((n, w), dt) for w, dt in outs],
        grid=(n // tm,),
        in_specs=[row(D_MODEL), mod.spec(3, tm, 1), mod.spec(4, tm, 1), full(g), full(wpack),
                  full(gq), full(gk), full(lbl), full(bd)],
        out_specs=[row(w) for w, _ in outs],
        compiler_params=_cp(("arbitrary",)),
        name="mix",
    )(x, mod.arr, mod.arr, g, wpack, gq, gk, lbl, bd)


def _t5_bucket(dist):
    n = jnp.maximum(dist, 0)
    max_exact = T5_BUCKETS // 2
    nf = jnp.maximum(n, 1).astype(F32)
    large = max_exact + (jnp.log(nf / max_exact) / math.log(T5_MAX_DIST / max_exact)
                         * (T5_BUCKETS - max_exact)).astype(I32)
    large = jnp.minimum(large, T5_BUCKETS - 1)
    return jnp.where(n < max_exact, n, large)


def _bias_kernel(rb_ref, bp_ref, bs_ref):
    def lookup(bucket, h):
        val = jnp.zeros(bucket.shape, F32)
        for b in range(T5_BUCKETS):
            val = jnp.where(bucket == b, rb_ref[b, h], val)
        return val

    k = lax.broadcasted_iota(I32, (2 * LANES, LANES), 0)
    q = lax.broadcasted_iota(I32, (2 * LANES, LANES), 1)
    far_bucket = _t5_bucket(jnp.full((2 * LANES, LANES), 2 * LANES, I32))
    for v, dist in enumerate((q - k, LANES + q - k)):
        bucket = _t5_bucket(dist)
        for h in range(H_A):
            bp_ref[v, h // 2, :, (h % 2) * LANES:(h % 2 + 1) * LANES] = lookup(bucket, h) - lookup(far_bucket, h)

    t = lax.broadcasted_iota(I32, (SUBLANES, LANES), 0)
    u = lax.broadcasted_iota(I32, (SUBLANES, LANES), 1)
    sdists = (PAGE + t - u, t - u, jnp.full((SUBLANES, LANES), 2 * LANES, I32))
    for m in range(3):
        bucket = _t5_bucket(sdists[m])
        for h in range(H_A):
            bs_ref[m, h * SUBLANES:(h + 1) * SUBLANES, :] = lookup(bucket, h)


def _bias_tiles(rel_bias):
    return pl.pallas_call(
        _bias_kernel,
        out_shape=[jax.ShapeDtypeStruct((2, H_A // 2, 2 * LANES, 2 * LANES), F32),
                   jax.ShapeDtypeStruct((3, H_A * SUBLANES, LANES), F32)],
        in_specs=[pl.BlockSpec(memory_space=pltpu.SMEM)],
        out_specs=[pl.BlockSpec(memory_space=pltpu.VMEM), pl.BlockSpec(memory_space=pltpu.VMEM)],
        name="t5_bias",
    )(rel_bias)


def _sortable_key(score):
    bits = pltpu.bitcast(score, I32)
    return jnp.where(bits < 0, INT_MIN - bits, bits)


def _topk_maskbias(keys_ref, maskb_ref, nch, rows, cw, n_sel, idx_bits):
    nt = cw // LANES
    lane = lax.broadcasted_iota(I32, (rows, LANES), 1)

    def count(indicator):
        def body(c, acc):
            k0 = pl.multiple_of(c * cw, cw)
            for j in range(nt):
                kk = keys_ref[:, pl.ds(k0 + j * LANES, LANES)]
                acc = acc + indicator(kk, k0 + j * LANES)
            return acc
        acc = lax.fori_loop(0, nch, body, jnp.zeros((rows, LANES), F32))
        return jnp.broadcast_to(jnp.sum(acc, axis=1, keepdims=True), (rows, LANES))

    def bisect(it, ans):
        cand = ans | jnp.left_shift(jnp.int32(1), 31 - it)
        cs = cand ^ INT_MIN
        cnt = count(lambda kk, p0: jnp.where(kk >= cs, 1.0, 0.0))
        return jnp.where(cnt >= n_sel, cand, ans)

    thr = lax.fori_loop(0, 32, bisect, jnp.zeros((rows, LANES), I32)) ^ INT_MIN
    need = n_sel - count(lambda kk, p0: jnp.where(kk > thr, 1.0, 0.0))
    n_eq = count(lambda kk, p0: jnp.where(kk == thr, 1.0, 0.0))
    has_tie = jnp.max(jnp.where(n_eq > need, 1.0, 0.0)) > 0.0

    def tie_search(_):
        def step(it, aj):
            cand = aj | jnp.left_shift(jnp.int32(1), idx_bits - 1 - it)
            cnt = count(lambda kk, p0: jnp.where(kk == thr, jnp.where((p0 + lane) < cand, 1.0, 0.0), 0.0))
            return jnp.where(cnt < need, cand, aj)
        return lax.fori_loop(0, idx_bits, step, jnp.zeros((rows, LANES), I32))

    cut = lax.cond(has_tie, tie_search, lambda _: jnp.full((rows, LANES), 2 ** 30, I32), 0)
    cut = jnp.where(thr == INT_MIN, -1, cut)

    def fin(c, _):
        k0 = pl.multiple_of(c * cw, cw)
        for j in range(nt):
            sl = pl.ds(k0 + j * LANES, LANES)
            kk = keys_ref[:, sl]
            tie = jnp.where((k0 + j * LANES + lane) <= cut, 0.0, NEG)
            maskb_ref[:, sl] = jnp.where(kk > thr, 0.0, jnp.where(kk == thr, tie, NEG))
        return 0

    lax.fori_loop(0, nch, fin, 0)


def _tree_reduce(x, op):
    while x.shape[0] > 1:
        half = x.shape[0] // 2
        x = op(x[:half], x[half:])
    return x[0]


def _sublane_allreduce(x, op):
    for sh in (4, 2, 1):
        x = op(x, pltpu.roll(x, sh, 0))
    return x


def _topk_maskbias_t(keys_ref, maskb_ref, far_ref, nch, cw, n_sel, idx_bits, far_limit):
    ng = cw // SUBLANES
    sub = lax.broadcasted_iota(I32, (ng, SUBLANES, LANES), 0) * SUBLANES + lax.broadcasted_iota(
        I32, (ng, SUBLANES, LANES), 1)

    def count(indicator):
        def body(c, acc):
            k0 = pl.multiple_of(c * cw, cw)
            kk = keys_ref[pl.ds(k0, cw), :].reshape(ng, SUBLANES, LANES)
            return acc + _tree_reduce(indicator(kk, k0), jnp.add)
        acc = lax.fori_loop(0, nch, body, jnp.zeros((SUBLANES, LANES), F32))
        return _sublane_allreduce(acc, jnp.add)

    def bisect(it, ans):
        cand = ans | jnp.left_shift(jnp.int32(1), 31 - it)
        cs = cand ^ INT_MIN
        cnt = count(lambda kk, k0: jnp.where(kk >= cs[None], 1.0, 0.0))
        return jnp.where(cnt >= n_sel, cand, ans)

    thr = lax.fori_loop(0, 32, bisect, jnp.zeros((SUBLANES, LANES), I32)) ^ INT_MIN
    need = n_sel - count(lambda kk, k0: jnp.where(kk > thr[None], 1.0, 0.0))
    n_eq = count(lambda kk, k0: jnp.where(kk == thr[None], 1.0, 0.0))
    has_tie = jnp.max(jnp.where(n_eq > need, 1.0, 0.0)) > 0.0

    def tie_search(_):
        def step(it, aj):
            cand = aj | jnp.left_shift(jnp.int32(1), idx_bits - 1 - it)
            cnt = count(lambda kk, k0: jnp.where(kk == thr[None], jnp.where((k0 + sub) < cand[None], 1.0, 0.0), 0.0))
            return jnp.where(cnt < need, cand, aj)
        return lax.fori_loop(0, idx_bits, step, jnp.zeros((SUBLANES, LANES), I32))

    cut = lax.cond(has_tie, tie_search, lambda _: jnp.full((SUBLANES, LANES), 2 ** 30, I32), 0)
    cut = jnp.where(thr == INT_MIN, -1, cut)

    def fin(c, _):
        k0 = pl.multiple_of(c * cw, cw)
        kk = keys_ref[pl.ds(k0, cw), :].reshape(ng, SUBLANES, LANES)
        kpos = k0 + sub
        tie = jnp.where(kpos <= cut[None], 0.0, NEG)
        mb = jnp.where(kk > thr[None], 0.0, jnp.where(kk == thr[None], tie, NEG))
        maskb_ref[pl.ds(k0, cw), :] = mb.reshape(cw, LANES)
        far_ref[pl.ds(k0, cw), :] = jnp.where(kpos < far_limit, mb, NEG).reshape(cw, LANES)
        return 0

    lax.fori_loop(0, nch, fin, 0)


def _softmax_step(carry, s, v16, v_is_transposed=False):
    m, l, acc = carry
    mn = jnp.maximum(m, jnp.max(s, axis=1, keepdims=True))
    alpha = jnp.exp(m - mn)
    p = jnp.exp(s - mn)
    l = alpha * l + jnp.sum(p, axis=1, keepdims=True)
    if v_is_transposed:
        pv = lax.dot_general(p.astype(BF16), v16, _NT, preferred_element_type=F32)
    else:
        pv = jnp.dot(p.astype(BF16), v16, preferred_element_type=F32)
    return mn, l, alpha * acc + pv


_QB = 128
_KC = 512
_FW = 512


def _attn_prompt_kernel(qa_ref, qi_ref, wi_ref, ki_ref, ka_ref, vat_ref, bias_ref, o_ref,
                        keys_sc, maskb_sc, far_sc, qs_sc, q2_sc, m_sc, l_sc, acc_sc):
    i = pl.program_id(1)
    lane = lax.broadcasted_iota(I32, (_QB, LANES), 1)
    lo = lane < D_I
    nch = i // (_KC // _QB) + 1
    ng = _KC // SUBLANES

    for h in range(H_I):
        pair = qi_ref[:, (h // 2) * LANES:(h // 2 + 1) * LANES].astype(F32)
        keep = jnp.where(lo, pair, 0.0) if h % 2 == 0 else jnp.where(lo, 0.0, pair)
        qs_sc[h * _QB:(h + 1) * _QB, :] = keep.astype(BF16)

    qpos = i * _QB + lax.broadcasted_iota(I32, (_KC, LANES), 1)

    def scores(c, _):
        k0 = pl.multiple_of(c * _KC, _KC)
        s = lax.dot_general(ki_ref[pl.ds(k0, _KC), :], qs_sc[...], _NT, preferred_element_type=F32)
        sc = jnp.zeros((_KC, LANES), F32)
        for h in range(H_I):
            sc = sc + jnp.maximum(s[:, h * _QB:(h + 1) * _QB], 0.0) * wi_ref[h:h + 1, :]
        kpos = k0 + lax.broadcasted_iota(I32, (_KC, LANES), 0)
        keys_sc[pl.ds(k0, _KC), :] = jnp.where(kpos <= qpos, _sortable_key(sc), INT_MIN)
        return 0

    lax.fori_loop(0, nch, scores, 0)
    near0 = jnp.maximum(i - 1, 0) * _QB
    _topk_maskbias_t(keys_sc, maskb_sc, far_sc, nch, _KC, TOPK_MAX, 12, near0)

    npair = H_A // 2
    cols2 = 2 * _QB
    for p in range(npair):
        qp = qa_ref[:, p * LANES:(p + 1) * LANES].astype(F32)
        q2_sc[p * cols2:(p + 1) * cols2, :] = jnp.concatenate(
            [jnp.where(lo, qp, 0.0), jnp.where(lo, 0.0, qp)], axis=0).astype(BF16)
    m_sc[...] = jnp.full(m_sc.shape, NEG, F32)
    l_sc[...] = jnp.zeros_like(l_sc)
    acc_sc[...] = jnp.zeros_like(acc_sc)

    def update_all(k0, width, mask_ref, biases):
        mb = mask_ref[pl.ds(k0, width), :]
        mb2 = jnp.concatenate([mb, mb], axis=1)
        logits = []
        for p in range(npair):
            s = lax.dot_general(ka_ref[pl.ds(k0, width), p * LANES:(p + 1) * LANES],
                                q2_sc[p * cols2:(p + 1) * cols2, :], _NT, preferred_element_type=F32) + mb2
            if biases is not None:
                s = s + biases[p]
            logits.append(s.reshape(width // SUBLANES, SUBLANES, cols2))
        probs = []
        for p in range(npair):
            m_old = m_sc[p]
            m_new = jnp.maximum(m_old, _sublane_allreduce(_tree_reduce(logits[p], jnp.maximum), jnp.maximum))
            alpha = jnp.exp(m_old - m_new)
            pexp = jnp.exp(logits[p] - m_new[None])
            m_sc[p] = m_new
            l_sc[p] = alpha * l_sc[p] + _tree_reduce(pexp, jnp.add)
            probs.append((pexp.reshape(width, cols2).astype(BF16), alpha))
        for p in range(npair):
            p16, alpha = probs[p]
            pv = jnp.dot(vat_ref[p * LANES:(p + 1) * LANES, pl.ds(k0, width)], p16,
                         preferred_element_type=F32)
            acc = acc_sc[p].reshape(LANES // SUBLANES, SUBLANES, cols2) * alpha[None]
            acc_sc[p] = acc.reshape(LANES, cols2) + pv

    def far(c, _):
        update_all(pl.multiple_of(c * _FW, _FW), _FW, far_sc, None)
        return 0

    lax.fori_loop(0, (near0 + _FW - 1) // _FW, far, 0)
    first = jnp.minimum(i, 1)
    update_all(pl.multiple_of(near0, _QB), 2 * _QB, maskb_sc, [bias_ref[first, p] for p in range(npair)])
    for p in range(npair):
        l = _sublane_allreduce(l_sc[p], jnp.add)
        o = (acc_sc[p].reshape(LANES // SUBLANES, SUBLANES, cols2) / l[None]).reshape(LANES, cols2)
        ot = jnp.concatenate([o[:D_HA, :_QB], o[D_HA:, _QB:]], axis=0)
        o_ref[:, p * LANES:(p + 1) * LANES] = ot.T


def _attn_prompt(qa16, qi16, wi_t, ki16, ka16, va16_t, bias_p, nbatch, seq):
    nq = seq // _QB
    blk = lambda w: pl.BlockSpec((_QB, w), lambda b, i: (b * nq + i, 0))
    per_b = lambda w: pl.BlockSpec((seq, w), lambda b, i: (b, 0))
    npair = H_A // 2
    return pl.pallas_call(
        _attn_prompt_kernel,
        out_shape=jax.ShapeDtypeStruct((nbatch * seq, W_ATTN), F32),
        grid=(nbatch, nq),
        in_specs=[blk(W_ATTN), blk(H_I * D_I),
                  pl.BlockSpec((H_I, _QB), lambda b, i: (0, b * nq + i)),
                  per_b(LANES), per_b(W_ATTN),
                  pl.BlockSpec((None, W_ATTN, seq), lambda b, i: (b, 0, 0)),
                  pl.BlockSpec(bias_p.shape, lambda b, i: (0, 0, 0, 0))],
        out_specs=blk(W_ATTN),
        scratch_shapes=[pltpu.VMEM((seq, LANES), I32), pltpu.VMEM((seq, LANES), F32), pltpu.VMEM((seq, LANES), F32),
                        pltpu.VMEM((H_I * _QB, LANES), BF16), pltpu.VMEM((H_A * _QB, LANES), BF16),
                        pltpu.VMEM((npair, SUBLANES, 2 * _QB), F32), pltpu.VMEM((npair, SUBLANES, 2 * _QB), F32),
                        pltpu.VMEM((npair, LANES, 2 * _QB), F32)],
        compiler_params=_cp(("arbitrary", "arbitrary")),
        name="attn_prompt",
    )(qa16, qi16, wi_t, ki16, ka16, va16_t, bias_p)


_PPS = 16
_TP = SUBLANES
_TK = 2 * SUBLANES


def _sample_scores_kernel(pt_ref, q_ref, w_ref, kn_ref, *rest):
    pages, o_ref = rest[:_PPS], rest[_PPS]
    s_id = pl.program_id(1)
    q = q_ref[...]

    def head_sum(s):
        n = s.shape[1]
        out = jnp.zeros((_TP, n), F32)
        for h in range(H_I):
            wh = w_ref[h * _TP:(h + 1) * _TP, :]
            out = out + jnp.maximum(s[h * _TP:(h + 1) * _TP, :], 0.0) * jnp.concatenate([wh] * (n // LANES), axis=1)
        return out

    kc = jnp.concatenate([pg[...] for pg in pages], axis=1).astype(BF16)
    s = jnp.dot(q, kc, preferred_element_type=F32)
    o_ref[:, pl.ds(pl.multiple_of(s_id * (_PPS * PAGE), _PPS * PAGE), _PPS * PAGE)] = head_sum(s)

    @pl.when(s_id == pl.num_programs(1) - 1)
    def _():
        kn = jnp.concatenate([kn_ref[...], jnp.zeros((LANES - _TK, D_I), BF16)], axis=0)
        sn = lax.dot_general(q, kn, _NT, preferred_element_type=F32)
        o_ref[:, o_ref.shape[1] - LANES:] = head_sum(sn)


def _sample_scores(page_table, q_st, w_b, ki_new, cache_kidx, n_pages):
    nseq = q_st.shape[0]
    steps = n_pages // _PPS
    lpad = n_pages * PAGE + LANES
    page_specs = [pl.BlockSpec((None, D_I, PAGE), functools.partial(
        lambda b, s, pt, p: (pt[b, s * _PPS + p], 0, 0), p=p)) for p in range(_PPS)]
    return pl.pallas_call(
        _sample_scores_kernel,
        out_shape=jax.ShapeDtypeStruct((nseq, _TP, lpad), F32),
        grid_spec=pltpu.PrefetchScalarGridSpec(
            num_scalar_prefetch=1, grid=(nseq, steps),
            in_specs=[pl.BlockSpec((None, H_I * _TP, D_I), lambda b, s, pt: (b, 0, 0)),
                      pl.BlockSpec((None, H_I * _TP, LANES), lambda b, s, pt: (b, 0, 0)),
                      pl.BlockSpec((None, _TK, D_I), lambda b, s, pt: (b, 0, 0))] + page_specs,
            out_specs=pl.BlockSpec((None, _TP, lpad), lambda b, s, pt: (b, 0, 0))),
        compiler_params=_cp(("arbitrary", "arbitrary")),
        name="sample_scores",
    )(page_table, q_st, w_b, ki_new, *([cache_kidx] * _PPS))


_SEL_ROWS = 128
_SEL_CW = 640


def _sample_select_kernel(s_ref, o_ref, keys_sc, *, past):
    lpad = s_ref.shape[1]
    t = lax.broadcasted_iota(I32, (_SEL_ROWS, _SEL_CW), 0) % _TP
    for c in range(lpad // _SEL_CW):
        sl = slice(c * _SEL_CW, (c + 1) * _SEL_CW)
        kpos = c * _SEL_CW + lax.broadcasted_iota(I32, (_SEL_ROWS, _SEL_CW), 1)
        keys_sc[:, sl] = jnp.where(kpos <= past + t, _sortable_key(s_ref[:, sl]), INT_MIN)
    _topk_maskbias(keys_sc, o_ref, lpad // _SEL_CW, _SEL_ROWS, _SEL_CW, TOPK_MAX, 14)


def _sample_select(scores2d, past):
    rows, lpad = scores2d.shape
    return pl.pallas_call(
        functools.partial(_sample_select_kernel, past=past),
        out_shape=jax.ShapeDtypeStruct((rows, lpad), F32),
        grid=(rows // _SEL_ROWS,),
        in_specs=[pl.BlockSpec((_SEL_ROWS, lpad), lambda g: (g, 0))],
        out_specs=pl.BlockSpec((_SEL_ROWS, lpad), lambda g: (g, 0)),
        scratch_shapes=[pltpu.VMEM((_SEL_ROWS, lpad), I32)],
        compiler_params=_cp(("arbitrary",)),
        name="sample_select",
    )(scores2d)


def _sample_attn_kernel(pt_ref, q_ref, mb_ref, kn_ref, vn_ref, bias_ref, *rest):
    kp, vp = rest[:_PPS], rest[_PPS:2 * _PPS]
    o_ref, qblk_sc, m_sc, l_sc, acc_sc = rest[2 * _PPS:]
    s_id = pl.program_id(1)
    last = pl.num_programs(1) - 1
    rows = H_A * _TP
    span = _PPS * PAGE

    @pl.when(s_id == 0)
    def _():
        q = jnp.concatenate([q_ref[...]] * H_A, axis=0)
        r = lax.broadcasted_iota(I32, (rows, W_ATTN), 0) // _TP
        c = lax.broadcasted_iota(I32, (rows, W_ATTN), 1) // D_HA
        qblk_sc[...] = jnp.where(r == c, q, 0.0).astype(BF16)
        m_sc[...] = jnp.full(m_sc.shape, NEG, F32)
        l_sc[...] = jnp.zeros_like(l_sc)
        acc_sc[...] = jnp.zeros_like(acc_sc)

    def update(s, v16, v_is_transposed):
        carry = (m_sc[:, 0:1], l_sc[:, 0:1], acc_sc[...])
        m, l, acc = _softmax_step(carry, s, v16, v_is_transposed)
        m_sc[...] = jnp.broadcast_to(m, m_sc.shape)
        l_sc[...] = jnp.broadcast_to(l, l_sc.shape)
        acc_sc[...] = acc

    k16 = jnp.concatenate([pg[...] for pg in kp], axis=1).astype(BF16)
    v16 = jnp.concatenate([pg[...] for pg in vp], axis=1).astype(BF16)
    s = jnp.dot(qblk_sc[...], k16, preferred_element_type=F32)
    mb = mb_ref[:, pl.ds(pl.multiple_of(s_id * span, span), span)]
    last_tile = jnp.where(s_id == last, bias_ref[0], bias_ref[2])
    bias = jnp.concatenate([bias_ref[2]] * (_PPS - 1) + [last_tile], axis=1)
    update(s + jnp.concatenate([mb] * H_A, axis=0) + bias, v16, True)

    @pl.when(s_id == last)
    def _():
        pad = jnp.zeros((LANES - _TK, W_ATTN), BF16)
        kn = jnp.concatenate([kn_ref[...], pad], axis=0)
        vn = jnp.concatenate([vn_ref[...], pad], axis=0)
        sn = lax.dot_general(qblk_sc[...], kn, _NT, preferred_element_type=F32)
        mbn = mb_ref[:, mb_ref.shape[1] - LANES:]
        update(sn + jnp.concatenate([mbn] * H_A, axis=0) + bias_ref[1], vn, False)
        o = acc_sc[...] / l_sc[:, 0:1]
        c = lax.broadcasted_iota(I32, (_TP, W_ATTN), 1) // D_HA
        out = jnp.zeros((_TP, W_ATTN), F32)
        for h in range(H_A):
            out = jnp.where(c == h, o[h * _TP:(h + 1) * _TP, :], out)
        o_ref[...] = out


def _sample_attn(page_table, q8, maskb, k_new, v_new, bias_s, cache_k, cache_v, n_pages):
    nseq = q8.shape[0]
    steps = n_pages // _PPS
    lpad = maskb.shape[1]
    pspec = lambda: [pl.BlockSpec((None, W_ATTN, PAGE), functools.partial(
        lambda b, s, pt, p: (pt[b, s * _PPS + p], 0, 0), p=p)) for p in range(_PPS)]
    seq3 = lambda t: pl.BlockSpec((None, t, W_ATTN), lambda b, s, pt: (b, 0, 0))
    return pl.pallas_call(
        _sample_attn_kernel,
        out_shape=jax.ShapeDtypeStruct((nseq, _TP, W_ATTN), F32),
        grid_spec=pltpu.PrefetchScalarGridSpec(
            num_scalar_prefetch=1, grid=(nseq, steps),
            in_specs=[seq3(_TP), pl.BlockSpec((_TP, lpad), lambda b, s, pt: (b, 0)),
                      seq3(_TK), seq3(_TK),
                      pl.BlockSpec(bias_s.shape, lambda b, s, pt: (0, 0, 0))] + pspec() + pspec(),
            out_specs=seq3(_TP),
            scratch_shapes=[pltpu.VMEM((H_A * _TP, W_ATTN), BF16), pltpu.VMEM((H_A * _TP, LANES), F32),
                            pltpu.VMEM((H_A * _TP, LANES), F32), pltpu.VMEM((H_A * _TP, W_ATTN), F32)]),
        compiler_params=_cp(("arbitrary", "arbitrary")),
        name="sample_attn",
    )(page_table, q8, maskb, k_new, v_new, bias_s, *([cache_k] * _PPS), *([cache_v] * _PPS))


def _hgrn_chunk(q, k, v, logf, st, c):
    row = lax.broadcasted_iota(I32, (c, DK_B), 0)
    g = logf
    sh = 1
    while sh < c:
        g = g + jnp.where(row >= sh, pltpu.roll(g, sh, 0), 0.0)
        sh *= 2

    o = lax.dot_general((q * jnp.exp(g)).astype(BF16), st.astype(BF16), _NT, preferred_element_type=F32)

    if c > SUBLANES:
        rr = lax.broadcasted_iota(I32, (c, c), 0)
        cc = lax.broadcasted_iota(I32, (c, c), 1)
        a = jnp.zeros((c, c), F32)
        w = SUBLANES
        while w < c:
            nb = c // (2 * w)
            gb = jnp.broadcast_to(g.reshape(nb, 2 * w, DK_B)[:, w - 1:w, :], (nb, 2 * w, DK_B)).reshape(c, DK_B)
            right = (row % (2 * w)) >= w
            qt = jnp.where(right, q * jnp.exp(jnp.where(right, g - gb, 0.0)), 0.0)
            kt = jnp.where(right, 0.0, k * jnp.exp(jnp.where(right, 0.0, gb - g)))
            aw = lax.dot_general(qt.astype(BF16), kt.astype(BF16), _NT, preferred_element_type=F32)
            a = a + jnp.where((rr // (2 * w)) == (cc // (2 * w)), aw, 0.0)
            w *= 2
        o = o + jnp.dot(a.astype(BF16), v.astype(BF16), preferred_element_type=F32)

    o = o + jnp.sum(q * k, axis=1, keepdims=True) * v
    for d in range(1, SUBLANES):
        ok = (row % SUBLANES) >= d
        e = jnp.exp(jnp.where(ok, g - pltpu.roll(g, d, 0), 0.0))
        coef = jnp.sum(jnp.where(ok, q * pltpu.roll(k, d, 0) * e, 0.0), axis=1, keepdims=True)
        o = o + coef * pltpu.roll(v, d, 0)

    g_end = g[c - 1:c, :]
    kd = k * jnp.exp(g_end - g)
    st = st * jnp.exp(g_end) + lax.dot_general(v.astype(BF16), kd.astype(BF16), _TN, preferred_element_type=F32)
    return o, st


def _hgrn_kernel(q_ref, f_ref, v_ref, s0_ref, o_ref, s_ref, *, nb, t_len, c):
    for j in range(nb):
        def body(ci, st):
            r = pl.ds(pl.multiple_of(ci * c, c), c)
            f = f_ref[j, r, :]
            o, st = _hgrn_chunk(q_ref[j, r, :], 1.0 - f, v_ref[j, r, :], jnp.log(f), st, c)
            o_ref[j, r, :] = o
            return st
        st = lax.fori_loop(0, t_len // c, body, s0_ref[j].T)
        s_ref[j] = st.T


def _hgrn(q, f, v, s0, nb, c):
    nseq, t_len, _ = q.shape
    tok = pl.BlockSpec((nb, t_len, DK_B), lambda b, h: (b, 0, h))
    st = pl.BlockSpec((nb, None, DK_B, DV_B), lambda b, h: (b, h, 0, 0))
    return pl.pallas_call(
        functools.partial(_hgrn_kernel, nb=nb, t_len=t_len, c=c),
        out_shape=[jax.ShapeDtypeStruct((nseq, t_len, W_HGRN), F32),
                   jax.ShapeDtypeStruct((nseq, H_B, DK_B, DV_B), F32)],
        grid=(nseq // nb, H_B),
        in_specs=[tok, tok, tok, st],
        out_specs=[tok, st],
        compiler_params=_cp(("arbitrary", "arbitrary")),
        name="hgrn",
    )(q, f, v, s0)


def _merge_kernel(oa_ref, oh_ref, sg_ref, x_ref, gt_ref, beta_ref, gh_ref, bd_ref, w_ref, o_ref):
    oa = oa_ref[...]
    a = oa * lax.rsqrt(_group_mean_sq(oa, bd_ref[...]) + RMS_EPS) * beta_ref[...]
    oh = oh_ref[...]
    parts = []
    for h in range(H_B):
        sl = slice(h * DV_B, (h + 1) * DV_B)
        parts.append(_rms(oh[:, sl], gh_ref[:, sl]))
    r = jnp.concatenate(parts, axis=1) * sg_ref[...]
    y = (jnp.dot(a.astype(BF16), w_ref[:W_ATTN, :], preferred_element_type=F32)
         + jnp.dot(r.astype(BF16), w_ref[W_ATTN:, :], preferred_element_type=F32))
    o_ref[...] = x_ref[...] + gt_ref[...] * y


def _merge(oa, oh, sg, x, mod, beta, gh, bd, w16, tm):
    n = x.shape[0]
    row = lambda w: pl.BlockSpec((tm, w), lambda i: (i, 0))
    full = lambda a: pl.BlockSpec(a.shape, lambda i: (0,) * a.ndim)
    return pl.pallas_call(
        _merge_kernel,
        out_shape=jax.ShapeDtypeStruct((n, D_MODEL), F32),
        grid=(n // tm,),
        in_specs=[row(W_ATTN), row(W_HGRN), row(W_HGRN), row(D_MODEL), mod.spec(5, tm, 1),
                  full(beta), full(gh), full(bd), full(w16)],
        out_specs=row(D_MODEL),
        compiler_params=_cp(("arbitrary",)),
        name="merge",
    )(oa, oh, sg, x, mod.arr, beta, gh, bd, w16)


def _pad_tokens(a, value=0.0, to=_TP):
    return jnp.pad(a, ((0, 0), (0, to - a.shape[1]), (0, 0)), constant_values=value)


def kernel(x_prompt, x_sample, cache_k, cache_v, cache_kidx, state_hgrn, page_table, c_prompt, c_sample,
           w_ada, b_ada, g_ffn1, ffn1_w_gate, ffn1_w_up, ffn1_w_down, g_mix, w_in, g_q, g_k,
           beta_attn, g_hgrn, w_out, g_ffn2, ffn2_w_gate, ffn2_w_up, ffn2_w_down, rel_bias, lb_logits):
    assert w_ada.shape[0] == 1, "single-layer problem"
    nb_p, seq, _ = x_prompt.shape
    nb_s, t_dec, _ = x_sample.shape
    n_phys = cache_k.shape[1]
    n_pages = page_table.shape[1]
    past = n_pages * PAGE
    assert t_dec <= _TP and n_pages % _PPS == 0 and (past + LANES) % _SEL_CW == 0

    row2 = lambda a: a.reshape(1, -1)
    w_in0 = w_in[0]
    pad_w = jnp.zeros((D_MODEL, LANES - H_I), F32)
    wpack = jnp.concatenate(
        [w_in0[:, :2048], w_in0[:, 2048:2112], w_in0[:, 2048:2112], w_in0[:, 2112:2120], pad_w,
         w_in0[:, 2120:]], axis=1).astype(BF16)
    assert wpack.shape[1] == _N_PACK
    gq = row2(jnp.tile(g_q[0], H_A))
    gk = row2(jnp.tile(g_k[0], H_A))
    gidx = jnp.arange(W_ATTN) // D_HA
    bd = jnp.where(gidx[:, None] == gidx[None, :], 1.0 / D_HA, 0.0).astype(BF16)
    ffn1 = (row2(g_ffn1[0]), ffn1_w_gate[0].astype(BF16), ffn1_w_up[0].astype(BF16), ffn1_w_down[0].astype(BF16))
    ffn2 = (row2(g_ffn2[0]), ffn2_w_gate[0].astype(BF16), ffn2_w_up[0].astype(BF16), ffn2_w_down[0].astype(BF16))
    w_out16 = w_out[0].astype(BF16)

    n_c = nb_p + nb_s
    c_pad = (-n_c) % SUBLANES
    c_all = jnp.concatenate([c_prompt, c_sample, jnp.zeros((c_pad, D_MODEL), F32)], axis=0)
    mods = _ada(c_all, w_ada[0].astype(BF16), row2(b_ada[0]))
    bias_p, bias_s = _bias_tiles(rel_bias)

    def layer(x, mod, tm, tf, mixer):
        x1 = _ffn(x, mod, 0, *ffn1, tm, tf)
        proj = _mix(x1, mod, row2(g_mix[0]), wpack, gq, gk, lb_logits, bd, tm)
        oa, oh, sg, extras = mixer(proj)
        x2 = _merge(oa, oh, sg, x1, mod, row2(beta_attn[0]), row2(g_hgrn[0]), bd, w_out16, tm)
        return _ffn(x2, mod, 6, *ffn2, tm, tf), extras

    tm_p = 512
    mod_p = _Mod(mods[:, :nb_p].reshape(N_MOD, nb_p, 1, D_MODEL), False, seq // tm_p)

    def prompt_mixer(proj):
        qa16, ka, ka16, va, va16, qi16, ki, ki16, wi, qb, f, vb, sg = proj
        va16_t = va16.reshape(nb_p, seq, W_ATTN).transpose(0, 2, 1)
        oa = _attn_prompt(qa16, qi16, wi[:, :H_I].T, ki16, ka16, va16_t, bias_p, nb_p, seq)
        r3 = lambda a: a.reshape(nb_p, seq, W_HGRN)
        oh, st = _hgrn(r3(qb), r3(f), r3(vb), jnp.zeros((nb_p, H_B, DK_B, DV_B), F32), 1, 64)
        return oa, oh.reshape(nb_p * seq, W_HGRN), sg, (ka, va, ki[:, :D_I], st)

    yp, (kp, vp, ip, sp) = layer(x_prompt.reshape(nb_p * seq, D_MODEL), mod_p, tm_p, D_FF // 2, prompt_mixer)

    n_tok = nb_s * t_dec
    mod_s = _Mod(jnp.repeat(mods[:, nb_p:nb_p + nb_s], t_dec, axis=1), True)
    ck = jnp.transpose(cache_k[0], (0, 2, 3, 1)).reshape(n_phys, W_ATTN, PAGE)
    cv = jnp.transpose(cache_v[0], (0, 2, 3, 1)).reshape(n_phys, W_ATTN, PAGE)
    cki = jnp.transpose(cache_kidx[0], (0, 2, 1))

    def sample_mixer(proj):
        qa16, ka, ka16, va, va16, qi16, ki, ki16, wi, qb, f, vb, sg = proj
        seq3 = lambda a: a.reshape(nb_s, t_dec, a.shape[-1])
        q_st = _pad_tokens(seq3(qi16)).reshape(nb_s, _TP, H_I, D_I).transpose(0, 2, 1, 3).reshape(nb_s, H_I * _TP, D_I)
        w_st = _pad_tokens(seq3(wi[:, :H_I])).transpose(0, 2, 1).reshape(nb_s, H_I * _TP, 1)
        w_b = jnp.broadcast_to(w_st, (nb_s, H_I * _TP, LANES))
        ki_new = _pad_tokens(seq3(ki16[:, :D_I]), to=_TK)
        scores = _sample_scores(page_table, q_st, w_b, ki_new, cki, n_pages)
        maskb = _sample_select(scores.reshape(nb_s * _TP, -1), past)
        oa8 = _sample_attn(page_table, _pad_tokens(seq3(qa16.astype(F32))), maskb,
                           _pad_tokens(seq3(ka16), to=_TK), _pad_tokens(seq3(va16), to=_TK), bias_s, ck, cv, n_pages)
        oa = oa8[:, :t_dec].reshape(n_tok, W_ATTN)
        oh8, st = _hgrn(_pad_tokens(seq3(qb)), _pad_tokens(seq3(f), 1.0), _pad_tokens(seq3(vb)),
                        state_hgrn[0], 8, _TP)
        return oa, oh8[:, :t_dec].reshape(n_tok, W_HGRN), sg, (ka, va, ki[:, :D_I], st)

    ys, (ks, vs, is_, ss) = layer(x_sample.reshape(n_tok, D_MODEL), mod_s, n_tok, D_FF // 2, sample_mixer)

    return (yp.reshape(nb_p, seq, D_MODEL).astype(x_prompt.dtype),
            ys.reshape(nb_s, t_dec, D_MODEL).astype(x_sample.dtype),
            kp.reshape(1, nb_p, seq, H_A, D_HA), vp.reshape(1, nb_p, seq, H_A, D_HA),
            ip.reshape(1, nb_p, seq, D_I), sp[None],
            ks.reshape(1, nb_s, t_dec, H_A, D_HA), vs.reshape(1, nb_s, t_dec, H_A, D_HA),
            is_.reshape(1, nb_s, t_dec, D_I), ss[None])
```

```python
import functools
import math

import jax
import jax.numpy as jnp
from jax import lax
from jax.experimental import pallas as pl
from jax.experimental.pallas import tpu as pltpu

F32 = jnp.float32
BF16 = jnp.bfloat16
I32 = jnp.int32

D_MODEL = 1024
W_ATTN = 512
W_HGRN = 512
D_HA = 64
H_A = 8
H_I = 8
D_I = 64
TOPK_MAX = 256
H_B = 4
DK_B = 128
DV_B = 128
D_FF = 2816
T5_BUCKETS = 32
T5_MAX_DIST = 128
RMS_EPS = 1e-6
N_MOD = 9
PAGE = 128

LANES = 128
SUBLANES = 8
NEG = -1e30
INT_MIN = -(2 ** 31)
VMEM_LIMIT = 56 * 1024 * 1024

_NT = (((1,), (1,)), ((), ()))
_TN = (((0,), (0,)), ((), ()))


def _cp(sem):
    return pltpu.CompilerParams(dimension_semantics=sem, vmem_limit_bytes=VMEM_LIMIT)


def _silu(x):
    return x / (1.0 + jnp.exp(-x))


def _sigmoid(x):
    return 1.0 / (1.0 + jnp.exp(-x))


def _rms(x, g):
    return x * lax.rsqrt(jnp.mean(x * x, axis=-1, keepdims=True) + RMS_EPS) * g


def _group_mean_sq(x, bd):
    xx = x * x
    hi = xx.astype(BF16)
    lo = (xx - hi.astype(F32)).astype(BF16)
    return (jnp.dot(hi, bd, preferred_element_type=F32)
            + jnp.dot(lo, bd, preferred_element_type=F32))


def _ada_kernel(c_ref, w_ref, b_ref, o_ref):
    a = _silu(c_ref[...]).astype(BF16)
    o_ref[...] = jnp.dot(a, w_ref[...], preferred_element_type=F32) + b_ref[...]


def _ada(c_all, w16, b):
    rows = c_all.shape[0]
    return pl.pallas_call(
        _ada_kernel,
        out_shape=jax.ShapeDtypeStruct((N_MOD, rows, D_MODEL), F32),
        grid=(N_MOD,),
        in_specs=[pl.BlockSpec((rows, D_MODEL), lambda j: (0, 0)),
                  pl.BlockSpec((D_MODEL, D_MODEL), lambda j: (0, j)),
                  pl.BlockSpec((1, D_MODEL), lambda j: (0, j))],
        out_specs=pl.BlockSpec((None, rows, D_MODEL), lambda j: (j, 0, 0)),
        compiler_params=_cp(("arbitrary",)),
        name="ada",
    )(c_all, w16, b)


class _Mod:
    def __init__(self, arr, per_token, tiles_per_seq=1):
        self.arr = arr
        self.per_token = per_token
        self.tiles_per_seq = tiles_per_seq

    def spec(self, j, tm, ngrid):
        if self.per_token:
            if ngrid == 1:
                return pl.BlockSpec((None, tm, D_MODEL), lambda i: (j, i, 0))
            return pl.BlockSpec((None, tm, D_MODEL), lambda i, k: (j, i, 0))
        tps = self.tiles_per_seq
        if ngrid == 1:
            return pl.BlockSpec((None, None, 1, D_MODEL), lambda i: (j, i // tps, 0, 0))
        return pl.BlockSpec((None, None, 1, D_MODEL), lambda i, k: (j, i // tps, 0, 0))


def _ffn_kernel(x_ref, sh_ref, sc_ref, gt_ref, g_ref, wg_ref, wu_ref, wd_ref, o_ref, h_sc, acc_sc):
    k = pl.program_id(1)

    @pl.when(k == 0)
    def _():
        h = _rms(x_ref[...], g_ref[...]) * (1.0 + sc_ref[...]) + sh_ref[...]
        h_sc[...] = h.astype(BF16)
        acc_sc[...] = jnp.zeros_like(acc_sc)

    h = h_sc[...]
    a = jnp.dot(h, wg_ref[...], preferred_element_type=F32)
    b = jnp.dot(h, wu_ref[...], preferred_element_type=F32)
    act = (_silu(a) * b).astype(BF16)
    acc_sc[...] += jnp.dot(act, wd_ref[...], preferred_element_type=F32)

    @pl.when(k == pl.num_programs(1) - 1)
    def _():
        o_ref[...] = x_ref[...] + 0.5 * gt_ref[...] * acc_sc[...]


def _ffn(x, mod, j0, g, wg, wu, wd, tm, tf):
    n = x.shape[0]
    nk = D_FF // tf
    return pl.pallas_call(
        _ffn_kernel,
        out_shape=jax.ShapeDtypeStruct((n, D_MODEL), F32),
        grid=(n // tm, nk),
        in_specs=[pl.BlockSpec((tm, D_MODEL), lambda i, k: (i, 0)),
                  mod.spec(j0, tm, 2), mod.spec(j0 + 1, tm, 2), mod.spec(j0 + 2, tm, 2),
                  pl.BlockSpec((1, D_MODEL), lambda i, k: (0, 0)),
                  pl.BlockSpec((D_MODEL, tf), lambda i, k: (0, k)),
                  pl.BlockSpec((D_MODEL, tf), lambda i, k: (0, k)),
                  pl.BlockSpec((tf, D_MODEL), lambda i, k: (k, 0))],
        out_specs=pl.BlockSpec((tm, D_MODEL), lambda i, k: (i, 0)),
        scratch_shapes=[pltpu.VMEM((tm, D_MODEL), BF16), pltpu.VMEM((tm, D_MODEL), F32)],
        compiler_params=_cp(("arbitrary", "arbitrary")),
        name="ffn",
    )(x, mod.arr, mod.arr, mod.arr, g, wg, wu, wd)


_C_QA, _C_KA, _C_VA, _C_QI = 0, 512, 1024, 1536
_C_KI, _C_WI = 2048, 2176
_C_QB, _C_FB, _C_IB, _C_GB = 2304, 2816, 3328, 3840
_N_PACK = 4352


def _mix_kernel(x_ref, sh_ref, sc_ref, g_ref, w_ref, gq_ref, gk_ref, lbl_ref, bd_ref,
                qa_o, ka_o, ka16_o, va_o, va16_o, qi_o, ki_o, ki16_o, wi_o, qb_o, f_o, vb_o, sg_o):
    h = (_rms(x_ref[...], g_ref[...]) * (1.0 + sc_ref[...]) + sh_ref[...]).astype(BF16)

    def proj(c0, width):
        return jnp.dot(h, w_ref[:, c0:c0 + width], preferred_element_type=F32)

    bd = bd_ref[...]
    qa = proj(_C_QA, W_ATTN)
    qa = qa * lax.rsqrt(_group_mean_sq(qa, bd) + RMS_EPS) * gq_ref[...]
    qa_o[...] = (qa * (D_HA ** -0.5)).astype(BF16)
    ka = proj(_C_KA, W_ATTN)
    ka = ka * lax.rsqrt(_group_mean_sq(ka, bd) + RMS_EPS) * gk_ref[...]
    ka_o[...] = ka
    ka16_o[...] = ka.astype(BF16)
    va = proj(_C_VA, W_ATTN)
    va_o[...] = va
    va16_o[...] = va.astype(BF16)
    qi_o[...] = (proj(_C_QI, H_I * D_I) * (D_I ** -0.5)).astype(BF16)
    ki = proj(_C_KI, LANES)
    ki_o[...] = ki
    ki16_o[...] = ki.astype(BF16)
    wi_o[...] = proj(_C_WI, LANES) * (H_I ** -0.5)
    qb_o[...] = _silu(proj(_C_QB, W_HGRN))
    l0 = lbl_ref[0:1, :]
    l1 = lbl_ref[1:2, :]
    mx = jnp.maximum(l0, l1)
    e0 = jnp.exp(l0 - mx)
    e1 = jnp.exp(l1 - mx)
    lb = e0 / (e0 + e1)
    f_o[...] = lb + (1.0 - lb) * _sigmoid(proj(_C_FB, W_HGRN))
    vb_o[...] = proj(_C_IB, W_HGRN)
    sg_o[...] = _silu(proj(_C_GB, W_HGRN))


def _mix(x, mod, g, wpack, gq, gk, lbl, bd, tm):
    n = x.shape[0]
    row = lambda w: pl.BlockSpec((tm, w), lambda i: (i, 0))
    full = lambda a: pl.BlockSpec(a.shape, lambda i: (0,) * a.ndim)
    outs = [(W_ATTN, BF16), (W_ATTN, F32), (W_ATTN, BF16), (W_ATTN, F32), (W_ATTN, BF16),
            (H_I * D_I, BF16), (LANES, F32), (LANES, BF16), (LANES, F32),
            (W_HGRN, F32), (W_HGRN, F32), (W_HGRN, F32), (W_HGRN, F32)]
    return pl.pallas_call(
        _mix_kernel,
        out_shape=[jax.ShapeDtypeStruct((n, w), dt) for w, dt in outs],
        grid=(n // tm,),
        in_specs=[row(D_MODEL), mod.spec(3, tm, 1), mod.spec(4, tm, 1), full(g), full(wpack),
                  full(gq), full(gk), full(lbl), full(bd)],
        out_specs=[row(w) for w, _ in outs],
        compiler_params=_cp(("arbitrary",)),
        name="mix",
    )(x, mod.arr, mod.arr, g, wpack, gq, gk, lbl, bd)


def _t5_bucket(dist):
    n = jnp.maximum(dist, 0)
    max_exact = T5_BUCKETS // 2
    nf = jnp.maximum(n, 1).astype(F32)
    large = max_exact + (jnp.log(nf / max_exact) / math.log(T5_MAX_DIST / max_exact)
                         * (T5_BUCKETS - max_exact)).astype(I32)
    large = jnp.minimum(large, T5_BUCKETS - 1)
    return jnp.where(n < max_exact, n, large)


def _bias_kernel(rb_ref, bp_ref, bs_ref):
    def lookup(bucket, h):
        val = jnp.zeros(bucket.shape, F32)
        for b in range(T5_BUCKETS):
            val = jnp.where(bucket == b, rb_ref[b, h], val)
        return val

    k = lax.broadcasted_iota(I32, (2 * LANES, LANES), 0)
    q = lax.broadcasted_iota(I32, (2 * LANES, LANES), 1)
    far_bucket = _t5_bucket(jnp.full((2 * LANES, LANES), 2 * LANES, I32))
    for v, dist in enumerate((q - k, LANES + q - k)):
        bucket = _t5_bucket(dist)
        for h in range(H_A):
            bp_ref[v, h // 2, :, (h % 2) * LANES:(h % 2 + 1) * LANES] = lookup(bucket, h) - lookup(far_bucket, h)

    t = lax.broadcasted_iota(I32, (SUBLANES, LANES), 0)
    u = lax.broadcasted_iota(I32, (SUBLANES, LANES), 1)
    sdists = (PAGE + t - u, t - u, jnp.full((SUBLANES, LANES), 2 * LANES, I32))
    for m in range(3):
        bucket = _t5_bucket(sdists[m])
        for h in range(H_A):
            bs_ref[m, h * SUBLANES:(h + 1) * SUBLANES, :] = lookup(bucket, h)


def _bias_tiles(rel_bias):
    return pl.pallas_call(
        _bias_kernel,
        out_shape=[jax.ShapeDtypeStruct((2, H_A // 2, 2 * LANES, 2 * LANES), F32),
                   jax.ShapeDtypeStruct((3, H_A * SUBLANES, LANES), F32)],
        in_specs=[pl.BlockSpec(memory_space=pltpu.SMEM)],
        out_specs=[pl.BlockSpec(memory_space=pltpu.VMEM), pl.BlockSpec(memory_space=pltpu.VMEM)],
        name="t5_bias",
    )(rel_bias)


def _sortable_key(score):
    bits = pltpu.bitcast(score, I32)
    return jnp.where(bits < 0, INT_MIN - bits, bits)


def _topk_maskbias(keys_ref, maskb_ref, nch, rows, cw, n_sel, idx_bits):
    nt = cw // LANES
    lane = lax.broadcasted_iota(I32, (rows, LANES), 1)

    def count(indicator):
        def body(c, acc):
            k0 = pl.multiple_of(c * cw, cw)
            for j in range(nt):
                kk = keys_ref[:, pl.ds(k0 + j * LANES, LANES)]
                acc = acc + indicator(kk, k0 + j * LANES)
            return acc
        acc = lax.fori_loop(0, nch, body, jnp.zeros((rows, LANES), F32))
        return jnp.broadcast_to(jnp.sum(acc, axis=1, keepdims=True), (rows, LANES))

    def bisect(it, ans):
        cand = ans | jnp.left_shift(jnp.int32(1), 31 - it)
        cs = cand ^ INT_MIN
        cnt = count(lambda kk, p0: jnp.where(kk >= cs, 1.0, 0.0))
        return jnp.where(cnt >= n_sel, cand, ans)

    thr = lax.fori_loop(0, 32, bisect, jnp.zeros((rows, LANES), I32)) ^ INT_MIN
    need = n_sel - count(lambda kk, p0: jnp.where(kk > thr, 1.0, 0.0))
    n_eq = count(lambda kk, p0: jnp.where(kk == thr, 1.0, 0.0))
    has_tie = jnp.max(jnp.where(n_eq > need, 1.0, 0.0)) > 0.0

    def tie_search(_):
        def step(it, aj):
            cand = aj | jnp.left_shift(jnp.int32(1), idx_bits - 1 - it)
            cnt = count(lambda kk, p0: jnp.where(kk == thr, jnp.where((p0 + lane) < cand, 1.0, 0.0), 0.0))
            return jnp.where(cnt < need, cand, aj)
        return lax.fori_loop(0, idx_bits, step, jnp.zeros((rows, LANES), I32))

    cut = lax.cond(has_tie, tie_search, lambda _: jnp.full((rows, LANES), 2 ** 30, I32), 0)
    cut = jnp.where(thr == INT_MIN, -1, cut)

    def fin(c, _):
        k0 = pl.multiple_of(c * cw, cw)
        for j in range(nt):
            sl = pl.ds(k0 + j * LANES, LANES)
            kk = keys_ref[:, sl]
            tie = jnp.where((k0 + j * LANES + lane) <= cut, 0.0, NEG)
            maskb_ref[:, sl] = jnp.where(kk > thr, 0.0, jnp.where(kk == thr, tie, NEG))
        return 0

    lax.fori_loop(0, nch, fin, 0)


def _tree_reduce(x, op):
    while x.shape[0] > 1:
        half = x.shape[0] // 2
        x = op(x[:half], x[half:])
    return x[0]


def _sublane_allreduce(x, op):
    for sh in (4, 2, 1):
        x = op(x, pltpu.roll(x, sh, 0))
    return x


def _topk_maskbias_t(keys_ref, maskb_ref, far_ref, nch, cw, n_sel, idx_bits, far_limit):
    ng = cw // SUBLANES
    sub = lax.broadcasted_iota(I32, (ng, SUBLANES, LANES), 0) * SUBLANES + lax.broadcasted_iota(
        I32, (ng, SUBLANES, LANES), 1)

    def count(indicator):
        def body(c, acc):
            k0 = pl.multiple_of(c * cw, cw)
            kk = keys_ref[pl.ds(k0, cw), :].reshape(ng, SUBLANES, LANES)
            return acc + _tree_reduce(indicator(kk, k0), jnp.add)
        acc = lax.fori_loop(0, nch, body, jnp.zeros((SUBLANES, LANES), F32))
        return _sublane_allreduce(acc, jnp.add)

    def bisect(it, ans):
        cand = ans | jnp.left_shift(jnp.int32(1), 31 - it)
        cs = cand ^ INT_MIN
        cnt = count(lambda kk, k0: jnp.where(kk >= cs[None], 1.0, 0.0))
        return jnp.where(cnt >= n_sel, cand, ans)

    thr = lax.fori_loop(0, 32, bisect, jnp.zeros((SUBLANES, LANES), I32)) ^ INT_MIN
    need = n_sel - count(lambda kk, k0: jnp.where(kk > thr[None], 1.0, 0.0))
    n_eq = count(lambda kk, k0: jnp.where(kk == thr[None], 1.0, 0.0))
    has_tie = jnp.max(jnp.where(n_eq > need, 1.0, 0.0)) > 0.0

    def tie_search(_):
        def step(it, aj):
            cand = aj | jnp.left_shift(jnp.int32(1), idx_bits - 1 - it)
            cnt = count(lambda kk, k0: jnp.where(kk == thr[None], jnp.where((k0 + sub) < cand[None], 1.0, 0.0), 0.0))
            return jnp.where(cnt < need, cand, aj)
        return lax.fori_loop(0, idx_bits, step, jnp.zeros((SUBLANES, LANES), I32))

    cut = lax.cond(has_tie, tie_search, lambda _: jnp.full((SUBLANES, LANES), 2 ** 30, I32), 0)
    cut = jnp.where(thr == INT_MIN, -1, cut)

    def fin(c, _):
        k0 = pl.multiple_of(c * cw, cw)
        kk = keys_ref[pl.ds(k0, cw), :].reshape(ng, SUBLANES, LANES)
        kpos = k0 + sub
        tie = jnp.where(kpos <= cut[None], 0.0, NEG)
        mb = jnp.where(kk > thr[None], 0.0, jnp.where(kk == thr[None], tie, NEG))
        maskb_ref[pl.ds(k0, cw), :] = mb.reshape(cw, LANES)
        far_ref[pl.ds(k0, cw), :] = jnp.where(kpos < far_limit, mb, NEG).reshape(cw, LANES)
        return 0

    lax.fori_loop(0, nch, fin, 0)


def _softmax_step(carry, s, v16, v_is_transposed=False):
    m, l, acc = carry
    mn = jnp.maximum(m, jnp.max(s, axis=1, keepdims=True))
    alpha = jnp.exp(m - mn)
    p = jnp.exp(s - mn)
    l = alpha * l + jnp.sum(p, axis=1, keepdims=True)
    if v_is_transposed:
        pv = lax.dot_general(p.astype(BF16), v16, _NT, preferred_element_type=F32)
    else:
        pv = jnp.dot(p.astype(BF16), v16, preferred_element_type=F32)
    return mn, l, alpha * acc + pv


_QB = 128
_KC = 512
_FW = 512


def _prompt_attn_step(step, nsteps, i, qa_ref, qi_ref, wi_ref, ki_ref, ka_ref, vat_ref, bias_ref, o_ref,
                      keys_sc, maskb_sc, far_sc, qs_sc, q2_sc, m_sc, l_sc, acc_sc):
    lane = lax.broadcasted_iota(I32, (_QB, LANES), 1)
    lo = lane < D_I
    nch = i // (_KC // _QB) + 1
    near0 = jnp.maximum(i - 1, 0) * _QB
    npair = H_A // 2
    cols2 = 2 * _QB

    @pl.when(step == 0)
    def _():
        for h in range(H_I):
            pair = qi_ref[:, (h // 2) * LANES:(h // 2 + 1) * LANES].astype(F32)
            keep = jnp.where(lo, pair, 0.0) if h % 2 == 0 else jnp.where(lo, 0.0, pair)
            qs_sc[h * _QB:(h + 1) * _QB, :] = keep.astype(BF16)

        qpos = i * _QB + lax.broadcasted_iota(I32, (_KC, LANES), 1)

        def scores(c, _):
            k0 = pl.multiple_of(c * _KC, _KC)
            s = lax.dot_general(ki_ref[pl.ds(k0, _KC), :], qs_sc[...], _NT, preferred_element_type=F32)
            sc = jnp.zeros((_KC, LANES), F32)
            for h in range(H_I):
                sc = sc + jnp.maximum(s[:, h * _QB:(h + 1) * _QB], 0.0) * wi_ref[h:h + 1, :]
            kpos = k0 + lax.broadcasted_iota(I32, (_KC, LANES), 0)
            keys_sc[pl.ds(k0, _KC), :] = jnp.where(kpos <= qpos, _sortable_key(sc), INT_MIN)
            return 0

        lax.fori_loop(0, nch, scores, 0)
        _topk_maskbias_t(keys_sc, maskb_sc, far_sc, nch, _KC, TOPK_MAX, 12, near0)

        for p in range(npair):
            qp = qa_ref[:, p * LANES:(p + 1) * LANES].astype(F32)
            q2_sc[p * cols2:(p + 1) * cols2, :] = jnp.concatenate(
                [jnp.where(lo, qp, 0.0), jnp.where(lo, 0.0, qp)], axis=0).astype(BF16)
        m_sc[...] = jnp.full(m_sc.shape, NEG, F32)
        l_sc[...] = jnp.zeros_like(l_sc)
        acc_sc[...] = jnp.zeros_like(acc_sc)

    def update_all(k0, width, mask_ref, biases):
        mb = mask_ref[pl.ds(k0, width), :]
        mb2 = jnp.concatenate([mb, mb], axis=1)
        logits = []
        for p in range(npair):
            s = lax.dot_general(ka_ref[pl.ds(k0, width), p * LANES:(p + 1) * LANES],
                                q2_sc[p * cols2:(p + 1) * cols2, :], _NT, preferred_element_type=F32) + mb2
            if biases is not None:
                s = s + biases[p]
            logits.append(s.reshape(width // SUBLANES, SUBLANES, cols2))
        probs = []
        for p in range(npair):
            m_old = m_sc[p]
            m_new = jnp.maximum(m_old, _sublane_allreduce(_tree_reduce(logits[p], jnp.maximum), jnp.maximum))
            alpha = jnp.exp(m_old - m_new)
            pexp = jnp.exp(logits[p] - m_new[None])
            m_sc[p] = m_new
            l_sc[p] = alpha * l_sc[p] + _tree_reduce(pexp, jnp.add)
            probs.append((pexp.reshape(width, cols2).astype(BF16), alpha))
        for p in range(npair):
            p16, alpha = probs[p]
            pv = jnp.dot(vat_ref[p * LANES:(p + 1) * LANES, pl.ds(k0, width)], p16,
                         preferred_element_type=F32)
            acc = acc_sc[p].reshape(LANES // SUBLANES, SUBLANES, cols2) * alpha[None]
            acc_sc[p] = acc.reshape(LANES, cols2) + pv

    def far(j, _):
        update_all(pl.multiple_of((step + j * nsteps) * _FW, _FW), _FW, far_sc, None)
        return 0

    nfar = (near0 + _FW - 1) // _FW
    lax.fori_loop(0, (nfar - step + nsteps - 1) // nsteps, far, 0)

    @pl.when(step == nsteps - 1)
    def _():
        first = jnp.minimum(i, 1)
        update_all(pl.multiple_of(near0, _QB), 2 * _QB, maskb_sc, [bias_ref[first, p] for p in range(npair)])
        for p in range(npair):
            l = _sublane_allreduce(l_sc[p], jnp.add)
            o = (acc_sc[p].reshape(LANES // SUBLANES, SUBLANES, cols2) / l[None]).reshape(LANES, cols2)
            ot = jnp.concatenate([o[:D_HA, :_QB], o[D_HA:, _QB:]], axis=0)
            o_ref[:, p * LANES:(p + 1) * LANES] = ot.T


_PPS = 16
_TP = SUBLANES
_TK = 2 * SUBLANES


def _sample_scores_kernel(pt_ref, q_ref, w_ref, kn_ref, *rest):
    pages, o_ref = rest[:_PPS], rest[_PPS]
    s_id = pl.program_id(1)
    q = q_ref[...]

    def head_sum(s):
        n = s.shape[1]
        out = jnp.zeros((_TP, n), F32)
        for h in range(H_I):
            wh = w_ref[h * _TP:(h + 1) * _TP, :]
            out = out + jnp.maximum(s[h * _TP:(h + 1) * _TP, :], 0.0) * jnp.concatenate([wh] * (n // LANES), axis=1)
        return out

    kc = jnp.concatenate([pg[...] for pg in pages], axis=1).astype(BF16)
    s = jnp.dot(q, kc, preferred_element_type=F32)
    o_ref[:, pl.ds(pl.multiple_of(s_id * (_PPS * PAGE), _PPS * PAGE), _PPS * PAGE)] = head_sum(s)

    @pl.when(s_id == pl.num_programs(1) - 1)
    def _():
        kn = jnp.concatenate([kn_ref[...], jnp.zeros((LANES - _TK, D_I), BF16)], axis=0)
        sn = lax.dot_general(q, kn, _NT, preferred_element_type=F32)
        o_ref[:, o_ref.shape[1] - LANES:] = head_sum(sn)


def _sample_scores(page_table, q_st, w_b, ki_new, cache_kidx, n_pages):
    nseq = q_st.shape[0]
    steps = n_pages // _PPS
    lpad = n_pages * PAGE + LANES
    page_specs = [pl.BlockSpec((None, D_I, PAGE), functools.partial(
        lambda b, s, pt, p: (pt[b, s * _PPS + p], 0, 0), p=p)) for p in range(_PPS)]
    return pl.pallas_call(
        _sample_scores_kernel,
        out_shape=jax.ShapeDtypeStruct((nseq, _TP, lpad), F32),
        grid_spec=pltpu.PrefetchScalarGridSpec(
            num_scalar_prefetch=1, grid=(nseq, steps),
            in_specs=[pl.BlockSpec((None, H_I * _TP, D_I), lambda b, s, pt: (b, 0, 0)),
                      pl.BlockSpec((None, H_I * _TP, LANES), lambda b, s, pt: (b, 0, 0)),
                      pl.BlockSpec((None, _TK, D_I), lambda b, s, pt: (b, 0, 0))] + page_specs,
            out_specs=pl.BlockSpec((None, _TP, lpad), lambda b, s, pt: (b, 0, 0))),
        compiler_params=_cp(("arbitrary", "arbitrary")),
        name="sample_scores",
    )(page_table, q_st, w_b, ki_new, *([cache_kidx] * _PPS))


_SEL_ROWS = 128
_SEL_CW = 640


def _sample_select_kernel(s_ref, o_ref, keys_sc, *, past):
    lpad = s_ref.shape[1]
    t = lax.broadcasted_iota(I32, (_SEL_ROWS, _SEL_CW), 0) % _TP
    for c in range(lpad // _SEL_CW):
        sl = slice(c * _SEL_CW, (c + 1) * _SEL_CW)
        kpos = c * _SEL_CW + lax.broadcasted_iota(I32, (_SEL_ROWS, _SEL_CW), 1)
        keys_sc[:, sl] = jnp.where(kpos <= past + t, _sortable_key(s_ref[:, sl]), INT_MIN)
    _topk_maskbias(keys_sc, o_ref, lpad // _SEL_CW, _SEL_ROWS, _SEL_CW, TOPK_MAX, 14)


def _sample_select(scores2d, past):
    rows, lpad = scores2d.shape
    return pl.pallas_call(
        functools.partial(_sample_select_kernel, past=past),
        out_shape=jax.ShapeDtypeStruct((rows, lpad), F32),
        grid=(rows // _SEL_ROWS,),
        in_specs=[pl.BlockSpec((_SEL_ROWS, lpad), lambda g: (g, 0))],
        out_specs=pl.BlockSpec((_SEL_ROWS, lpad), lambda g: (g, 0)),
        scratch_shapes=[pltpu.VMEM((_SEL_ROWS, lpad), I32)],
        compiler_params=_cp(("arbitrary",)),
        name="sample_select",
    )(scores2d)


def _sample_attn_step(s_id, last, q_ref, mb_ref, kn_ref, vn_ref, bias_ref, kp, vp, o_ref,
                      qblk_sc, m_sc, l_sc, acc_sc):
    rows = H_A * _TP
    span = _PPS * PAGE

    @pl.when(s_id == 0)
    def _():
        q = jnp.concatenate([q_ref[...]] * H_A, axis=0)
        r = lax.broadcasted_iota(I32, (rows, W_ATTN), 0) // _TP
        c = lax.broadcasted_iota(I32, (rows, W_ATTN), 1) // D_HA
        qblk_sc[...] = jnp.where(r == c, q, 0.0).astype(BF16)
        m_sc[...] = jnp.full(m_sc.shape, NEG, F32)
        l_sc[...] = jnp.zeros_like(l_sc)
        acc_sc[...] = jnp.zeros_like(acc_sc)

    def update(s, v16, v_is_transposed):
        carry = (m_sc[:, 0:1], l_sc[:, 0:1], acc_sc[...])
        m, l, acc = _softmax_step(carry, s, v16, v_is_transposed)
        m_sc[...] = jnp.broadcast_to(m, m_sc.shape)
        l_sc[...] = jnp.broadcast_to(l, l_sc.shape)
        acc_sc[...] = acc

    k16 = jnp.concatenate([pg[...].astype(BF16) for pg in kp], axis=1)
    v16 = jnp.concatenate([pg[...].astype(BF16) for pg in vp], axis=1)
    s = jnp.dot(qblk_sc[...], k16, preferred_element_type=F32)
    mb = mb_ref[:, pl.ds(pl.multiple_of(s_id * span, span), span)]
    last_tile = jnp.where(s_id == last, bias_ref[0], bias_ref[2])
    bias = jnp.concatenate([bias_ref[2]] * (_PPS - 1) + [last_tile], axis=1)
    update(s + jnp.concatenate([mb] * H_A, axis=0) + bias, v16, True)

    @pl.when(s_id == last)
    def _():
        pad = jnp.zeros((LANES - _TK, W_ATTN), BF16)
        kn = jnp.concatenate([kn_ref[...], pad], axis=0)
        vn = jnp.concatenate([vn_ref[...], pad], axis=0)
        sn = lax.dot_general(qblk_sc[...], kn, _NT, preferred_element_type=F32)
        mbn = mb_ref[:, mb_ref.shape[1] - LANES:]
        update(sn + jnp.concatenate([mbn] * H_A, axis=0) + bias_ref[1], vn, False)
        o = acc_sc[...] / l_sc[:, 0:1]
        c = lax.broadcasted_iota(I32, (_TP, W_ATTN), 1) // D_HA
        out = jnp.zeros((_TP, W_ATTN), F32)
        for h in range(H_A):
            out = jnp.where(c == h, o[h * _TP:(h + 1) * _TP, :], out)
        o_ref[...] = out


def _attn_kernel(pt_ref, qa_ref, qi_ref, wi_ref, ki_ref, ka_ref, vat_ref, bias_p_ref,
                 sq_ref, smb_ref, skn_ref, svn_ref, bias_s_ref, *rest):
    kp, vp = rest[:_PPS], rest[_PPS:2 * _PPS]
    o_ref, so_ref = rest[2 * _PPS:2 * _PPS + 2]
    prompt_scratch = rest[2 * _PPS + 2:2 * _PPS + 10]
    sample_scratch = rest[2 * _PPS + 10:]
    step = pl.program_id(2)
    nsteps = pl.num_programs(2)
    _prompt_attn_step(step, nsteps, pl.program_id(1), qa_ref, qi_ref, wi_ref, ki_ref, ka_ref, vat_ref, bias_p_ref,
                      o_ref, *prompt_scratch)
    _sample_attn_step(step, nsteps - 1, sq_ref, smb_ref, skn_ref, svn_ref, bias_s_ref, kp, vp, so_ref,
                      *sample_scratch)


def _attention(qa16, qi16, wi_t, ki16, ka16, va16_t, bias_p, nbatch, seq,
               page_table, q8, maskb, k_new, v_new, bias_s, cache_k, cache_v, n_pages):
    nq = seq // _QB
    nseq = q8.shape[0]
    steps = n_pages // _PPS
    lpad = maskb.shape[1]
    npair = H_A // 2
    assert nseq == nbatch * nq, "one decode sequence per prompt query block"
    blk = lambda w: pl.BlockSpec((_QB, w), lambda b, i, s, pt: (b * nq + i, 0))
    once = pl.Buffered(1)
    per_b = lambda w: pl.BlockSpec((seq, w), lambda b, i, s, pt: (b, 0), pipeline_mode=once)
    pspec = lambda: [pl.BlockSpec((None, W_ATTN, PAGE), functools.partial(
        lambda b, i, s, pt, p: (pt[b * nq + i, s * _PPS + p], 0, 0), p=p)) for p in range(_PPS)]
    seq3 = lambda t: pl.BlockSpec((None, t, W_ATTN), lambda b, i, s, pt: (b * nq + i, 0, 0))
    return pl.pallas_call(
        _attn_kernel,
        out_shape=[jax.ShapeDtypeStruct((nbatch * seq, W_ATTN), F32),
                   jax.ShapeDtypeStruct((nseq, _TP, W_ATTN), F32)],
        grid_spec=pltpu.PrefetchScalarGridSpec(
            num_scalar_prefetch=1, grid=(nbatch, nq, steps),
            in_specs=[blk(W_ATTN), blk(H_I * D_I),
                      pl.BlockSpec((H_I, _QB), lambda b, i, s, pt: (0, b * nq + i)),
                      per_b(LANES), per_b(W_ATTN),
                      pl.BlockSpec((None, W_ATTN, seq), lambda b, i, s, pt: (b, 0, 0), pipeline_mode=once),
                      pl.BlockSpec(bias_p.shape, lambda b, i, s, pt: (0, 0, 0, 0), pipeline_mode=once),
                      seq3(_TP), pl.BlockSpec((_TP, lpad), lambda b, i, s, pt: (b * nq + i, 0)),
                      seq3(_TK), seq3(_TK),
                      pl.BlockSpec(bias_s.shape, lambda b, i, s, pt: (0, 0, 0), pipeline_mode=once)]
                     + pspec() + pspec(),
            out_specs=[blk(W_ATTN), seq3(_TP)],
            scratch_shapes=[pltpu.VMEM((seq, LANES), I32), pltpu.VMEM((seq, LANES), F32),
                            pltpu.VMEM((seq, LANES), F32),
                            pltpu.VMEM((H_I * _QB, LANES), BF16), pltpu.VMEM((H_A * _QB, LANES), BF16),
                            pltpu.VMEM((npair, SUBLANES, 2 * _QB), F32), pltpu.VMEM((npair, SUBLANES, 2 * _QB), F32),
                            pltpu.VMEM((npair, LANES, 2 * _QB), F32),
                            pltpu.VMEM((H_A * _TP, W_ATTN), BF16), pltpu.VMEM((H_A * _TP, LANES), F32),
                            pltpu.VMEM((H_A * _TP, LANES), F32), pltpu.VMEM((H_A * _TP, W_ATTN), F32)]),
        compiler_params=_cp(("arbitrary", "arbitrary", "arbitrary")),
        name="attention",
    )(page_table, qa16, qi16, wi_t, ki16, ka16, va16_t, bias_p, q8, maskb, k_new, v_new, bias_s,
      *([cache_k] * _PPS), *([cache_v] * _PPS))


def _hgrn_chunk(q, k, v, logf, st, c):
    row = lax.broadcasted_iota(I32, (c, DK_B), 0)
    g = logf
    sh = 1
    while sh < c:
        g = g + jnp.where(row >= sh, pltpu.roll(g, sh, 0), 0.0)
        sh *= 2

    o = lax.dot_general((q * jnp.exp(g)).astype(BF16), st.astype(BF16), _NT, preferred_element_type=F32)

    if c > SUBLANES:
        rr = lax.broadcasted_iota(I32, (c, c), 0)
        cc = lax.broadcasted_iota(I32, (c, c), 1)
        a = jnp.zeros((c, c), F32)
        w = SUBLANES
        while w < c:
            nb = c // (2 * w)
            gb = jnp.broadcast_to(g.reshape(nb, 2 * w, DK_B)[:, w - 1:w, :], (nb, 2 * w, DK_B)).reshape(c, DK_B)
            right = (row % (2 * w)) >= w
            qt = jnp.where(right, q * jnp.exp(jnp.where(right, g - gb, 0.0)), 0.0)
            kt = jnp.where(right, 0.0, k * jnp.exp(jnp.where(right, 0.0, gb - g)))
            aw = lax.dot_general(qt.astype(BF16), kt.astype(BF16), _NT, preferred_element_type=F32)
            a = a + jnp.where((rr // (2 * w)) == (cc // (2 * w)), aw, 0.0)
            w *= 2
        o = o + jnp.dot(a.astype(BF16), v.astype(BF16), preferred_element_type=F32)

    o = o + jnp.sum(q * k, axis=1, keepdims=True) * v
    for d in range(1, SUBLANES):
        ok = (row % SUBLANES) >= d
        e = jnp.exp(jnp.where(ok, g - pltpu.roll(g, d, 0), 0.0))
        coef = jnp.sum(jnp.where(ok, q * pltpu.roll(k, d, 0) * e, 0.0), axis=1, keepdims=True)
        o = o + coef * pltpu.roll(v, d, 0)

    g_end = g[c - 1:c, :]
    kd = k * jnp.exp(g_end - g)
    st = st * jnp.exp(g_end) + lax.dot_general(v.astype(BF16), kd.astype(BF16), _TN, preferred_element_type=F32)
    return o, st


def _hgrn_kernel(q_ref, f_ref, v_ref, s0_ref, o_ref, s_ref, *, nb, t_len, c):
    for j in range(nb):
        def body(ci, st):
            r = pl.ds(pl.multiple_of(ci * c, c), c)
            f = f_ref[j, r, :]
            o, st = _hgrn_chunk(q_ref[j, r, :], 1.0 - f, v_ref[j, r, :], jnp.log(f), st, c)
            o_ref[j, r, :] = o
            return st
        st = lax.fori_loop(0, t_len // c, body, s0_ref[j].T)
        s_ref[j] = st.T


def _hgrn(q, f, v, s0, nb, c):
    nseq, t_len, _ = q.shape
    tok = pl.BlockSpec((nb, t_len, DK_B), lambda b, h: (b, 0, h))
    st = pl.BlockSpec((nb, None, DK_B, DV_B), lambda b, h: (b, h, 0, 0))
    return pl.pallas_call(
        functools.partial(_hgrn_kernel, nb=nb, t_len=t_len, c=c),
        out_shape=[jax.ShapeDtypeStruct((nseq, t_len, W_HGRN), F32),
                   jax.ShapeDtypeStruct((nseq, H_B, DK_B, DV_B), F32)],
        grid=(nseq // nb, H_B),
        in_specs=[tok, tok, tok, st],
        out_specs=[tok, st],
        compiler_params=_cp(("arbitrary", "arbitrary")),
        name="hgrn",
    )(q, f, v, s0)


def _merge_kernel(oa_ref, oh_ref, sg_ref, x_ref, gt_ref, beta_ref, gh_ref, bd_ref, w_ref, o_ref):
    oa = oa_ref[...]
    a = oa * lax.rsqrt(_group_mean_sq(oa, bd_ref[...]) + RMS_EPS) * beta_ref[...]
    oh = oh_ref[...]
    parts = []
    for h in range(H_B):
        sl = slice(h * DV_B, (h + 1) * DV_B)
        parts.append(_rms(oh[:, sl], gh_ref[:, sl]))
    r = jnp.concatenate(parts, axis=1) * sg_ref[...]
    y = (jnp.dot(a.astype(BF16), w_ref[:W_ATTN, :], preferred_element_type=F32)
         + jnp.dot(r.astype(BF16), w_ref[W_ATTN:, :], preferred_element_type=F32))
    o_ref[...] = x_ref[...] + gt_ref[...] * y


def _merge(oa, oh, sg, x, mod, beta, gh, bd, w16, tm):
    n = x.shape[0]
    row = lambda w: pl.BlockSpec((tm, w), lambda i: (i, 0))
    full = lambda a: pl.BlockSpec(a.shape, lambda i: (0,) * a.ndim)
    return pl.pallas_call(
        _merge_kernel,
        out_shape=jax.ShapeDtypeStruct((n, D_MODEL), F32),
        grid=(n // tm,),
        in_specs=[row(W_ATTN), row(W_HGRN), row(W_HGRN), row(D_MODEL), mod.spec(5, tm, 1),
                  full(beta), full(gh), full(bd), full(w16)],
        out_specs=row(D_MODEL),
        compiler_params=_cp(("arbitrary",)),
        name="merge",
    )(oa, oh, sg, x, mod.arr, beta, gh, bd, w16)


def _pad_tokens(a, value=0.0, to=_TP):
    return jnp.pad(a, ((0, 0), (0, to - a.shape[1]), (0, 0)), constant_values=value)


def kernel(x_prompt, x_sample, cache_k, cache_v, cache_kidx, state_hgrn, page_table, c_prompt, c_sample,
           w_ada, b_ada, g_ffn1, ffn1_w_gate, ffn1_w_up, ffn1_w_down, g_mix, w_in, g_q, g_k,
           beta_attn, g_hgrn, w_out, g_ffn2, ffn2_w_gate, ffn2_w_up, ffn2_w_down, rel_bias, lb_logits):
    assert w_ada.shape[0] == 1, "single-layer problem"
    nb_p, seq, _ = x_prompt.shape
    nb_s, t_dec, _ = x_sample.shape
    n_phys = cache_k.shape[1]
    n_pages = page_table.shape[1]
    past = n_pages * PAGE
    assert t_dec <= _TP and n_pages % _PPS == 0 and (past + LANES) % _SEL_CW == 0

    row2 = lambda a: a.reshape(1, -1)
    w_in0 = w_in[0]
    pad_w = jnp.zeros((D_MODEL, LANES - H_I), F32)
    wpack = jnp.concatenate(
        [w_in0[:, :2048], w_in0[:, 2048:2112], w_in0[:, 2048:2112], w_in0[:, 2112:2120], pad_w,
         w_in0[:, 2120:]], axis=1).astype(BF16)
    assert wpack.shape[1] == _N_PACK
    gq = row2(jnp.tile(g_q[0], H_A))
    gk = row2(jnp.tile(g_k[0], H_A))
    gidx = jnp.arange(W_ATTN) // D_HA
    bd = jnp.where(gidx[:, None] == gidx[None, :], 1.0 / D_HA, 0.0).astype(BF16)
    ffn1 = (row2(g_ffn1[0]), ffn1_w_gate[0].astype(BF16), ffn1_w_up[0].astype(BF16), ffn1_w_down[0].astype(BF16))
    ffn2 = (row2(g_ffn2[0]), ffn2_w_gate[0].astype(BF16), ffn2_w_up[0].astype(BF16), ffn2_w_down[0].astype(BF16))
    w_out16 = w_out[0].astype(BF16)

    n_c = nb_p + nb_s
    c_pad = (-n_c) % SUBLANES
    c_all = jnp.concatenate([c_prompt, c_sample, jnp.zeros((c_pad, D_MODEL), F32)], axis=0)
    mods = _ada(c_all, w_ada[0].astype(BF16), row2(b_ada[0]))
    bias_p, bias_s = _bias_tiles(rel_bias)

    def front(x, mod, tm, tf):
        x1 = _ffn(x, mod, 0, *ffn1, tm, tf)
        return x1, _mix(x1, mod, row2(g_mix[0]), wpack, gq, gk, lb_logits, bd, tm)

    def back(x1, mod, tm, tf, oa, oh, sg):
        x2 = _merge(oa, oh, sg, x1, mod, row2(beta_attn[0]), row2(g_hgrn[0]), bd, w_out16, tm)
        return _ffn(x2, mod, 6, *ffn2, tm, tf)

    tm_p, tf = 512, D_FF // 2
    n_tok = nb_s * t_dec
    mod_p = _Mod(mods[:, :nb_p].reshape(N_MOD, nb_p, 1, D_MODEL), False, seq // tm_p)
    mod_s = _Mod(jnp.repeat(mods[:, nb_p:nb_p + nb_s], t_dec, axis=1), True)
    x1p, proj_p = front(x_prompt.reshape(nb_p * seq, D_MODEL), mod_p, tm_p, tf)
    x1s, proj_s = front(x_sample.reshape(n_tok, D_MODEL), mod_s, n_tok, tf)
    qa16, kp, ka16, vp, va16, qi16, ki_p, ki16, wi, qb, f, vb, sg_p = proj_p
    qa16s, ks, ka16s, vs, va16s, qi16s, ki_s, ki16s, wis, qbs, fs, vbs, sg_s = proj_s

    ck = jnp.transpose(cache_k[0], (0, 2, 3, 1)).reshape(n_phys, W_ATTN, PAGE)
    cv = jnp.transpose(cache_v[0], (0, 2, 3, 1)).reshape(n_phys, W_ATTN, PAGE)
    cki = jnp.transpose(cache_kidx[0], (0, 2, 1))
    seq3 = lambda a: a.reshape(nb_s, t_dec, a.shape[-1])
    q_st = _pad_tokens(seq3(qi16s)).reshape(nb_s, _TP, H_I, D_I).transpose(0, 2, 1, 3).reshape(nb_s, H_I * _TP, D_I)
    w_st = _pad_tokens(seq3(wis[:, :H_I])).transpose(0, 2, 1).reshape(nb_s, H_I * _TP, 1)
    w_b = jnp.broadcast_to(w_st, (nb_s, H_I * _TP, LANES))
    scores = _sample_scores(page_table, q_st, w_b, _pad_tokens(seq3(ki16s[:, :D_I]), to=_TK), cki, n_pages)
    maskb = _sample_select(scores.reshape(nb_s * _TP, -1), past)

    va16_t = va16.reshape(nb_p, seq, W_ATTN).transpose(0, 2, 1)
    oa_p, oa8 = _attention(qa16, qi16, wi[:, :H_I].T, ki16, ka16, va16_t, bias_p, nb_p, seq,
                           page_table, _pad_tokens(seq3(qa16s.astype(F32))), maskb,
                           _pad_tokens(seq3(ka16s), to=_TK), _pad_tokens(seq3(va16s), to=_TK), bias_s, ck, cv, n_pages)
    r3 = lambda a: a.reshape(nb_p, seq, W_HGRN)
    oh_p, sp = _hgrn(r3(qb), r3(f), r3(vb), jnp.zeros((nb_p, H_B, DK_B, DV_B), F32), 1, 64)
    oh8, ss = _hgrn(_pad_tokens(seq3(qbs)), _pad_tokens(seq3(fs), 1.0), _pad_tokens(seq3(vbs)), state_hgrn[0], 8, _TP)

    yp = back(x1p, mod_p, tm_p, tf, oa_p, oh_p.reshape(nb_p * seq, W_HGRN), sg_p)
    ys = back(x1s, mod_s, n_tok, tf, oa8[:, :t_dec].reshape(n_tok, W_ATTN),
              oh8[:, :t_dec].reshape(n_tok, W_HGRN), sg_s)

    return (yp.reshape(nb_p, seq, D_MODEL).astype(x_prompt.dtype),
            ys.reshape(nb_s, t_dec, D_MODEL).astype(x_sample.dtype),
            kp.reshape(1, nb_p, seq, H_A, D_HA), vp.reshape(1, nb_p, seq, H_A, D_HA),
            ki_p[:, :D_I].reshape(1, nb_p, seq, D_I), sp[None],
            ks.reshape(1, nb_s, t_dec, H_A, D_HA), vs.reshape(1, nb_s, t_dec, H_A, D_HA),
            ki_s[:, :D_I].reshape(1, nb_s, t_dec, D_I), ss[None])
```

```python
import functools
import math

import jax
import jax.numpy as jnp
from jax import lax
from jax.experimental import pallas as pl
from jax.experimental.pallas import tpu as pltpu

F32 = jnp.float32
BF16 = jnp.bfloat16
I32 = jnp.int32

D_MODEL = 1024
W_ATTN = 512
W_HGRN = 512
D_HA = 64
H_A = 8
H_I = 8
D_I = 64
TOPK_MAX = 256
H_B = 4
DK_B = 128
DV_B = 128
D_FF = 2816
T5_BUCKETS = 32
T5_MAX_DIST = 128
RMS_EPS = 1e-6
N_MOD = 9
PAGE = 128

LANES = 128
SUBLANES = 8
NEG = -1e30
INT_MIN = -(2 ** 31)
VMEM_LIMIT = 56 * 1024 * 1024

_NT = (((1,), (1,)), ((), ()))
_TN = (((0,), (0,)), ((), ()))


def _cp(sem):
    return pltpu.CompilerParams(dimension_semantics=sem, vmem_limit_bytes=VMEM_LIMIT)


def _silu(x):
    return x / (1.0 + jnp.exp(-x))


def _sigmoid(x):
    return 1.0 / (1.0 + jnp.exp(-x))


def _rms(x, g):
    return x * lax.rsqrt(jnp.mean(x * x, axis=-1, keepdims=True) + RMS_EPS) * g


def _group_mean_sq(x, bd):
    xx = x * x
    hi = xx.astype(BF16)
    lo = (xx - hi.astype(F32)).astype(BF16)
    return (jnp.dot(hi, bd, preferred_element_type=F32)
            + jnp.dot(lo, bd, preferred_element_type=F32))


def _ada_kernel(c_ref, w_ref, b_ref, o_ref):
    a = _silu(c_ref[...]).astype(BF16)
    o_ref[...] = jnp.dot(a, w_ref[...], preferred_element_type=F32) + b_ref[...]


def _ada(c_all, w16, b):
    rows = c_all.shape[0]
    return pl.pallas_call(
        _ada_kernel,
        out_shape=jax.ShapeDtypeStruct((N_MOD, rows, D_MODEL), F32),
        grid=(N_MOD,),
        in_specs=[pl.BlockSpec((rows, D_MODEL), lambda j: (0, 0)),
                  pl.BlockSpec((D_MODEL, D_MODEL), lambda j: (0, j)),
                  pl.BlockSpec((1, D_MODEL), lambda j: (0, j))],
        out_specs=pl.BlockSpec((None, rows, D_MODEL), lambda j: (j, 0, 0)),
        compiler_params=_cp(("arbitrary",)),
        name="ada",
    )(c_all, w16, b)


class _Mod:
    def __init__(self, arr, per_token, tiles_per_seq=1):
        self.arr = arr
        self.per_token = per_token
        self.tiles_per_seq = tiles_per_seq

    def spec(self, j, tm, ngrid):
        if self.per_token:
            if ngrid == 1:
                return pl.BlockSpec((None, tm, D_MODEL), lambda i: (j, i, 0))
            return pl.BlockSpec((None, tm, D_MODEL), lambda i, k: (j, i, 0))
        tps = self.tiles_per_seq
        if ngrid == 1:
            return pl.BlockSpec((None, None, 1, D_MODEL), lambda i: (j, i // tps, 0, 0))
        return pl.BlockSpec((None, None, 1, D_MODEL), lambda i, k: (j, i // tps, 0, 0))


def _ffn_kernel(x_ref, sh_ref, sc_ref, gt_ref, g_ref, wg_ref, wu_ref, wd_ref, o_ref, h_sc, acc_sc):
    k = pl.program_id(1)

    @pl.when(k == 0)
    def _():
        h = _rms(x_ref[...], g_ref[...]) * (1.0 + sc_ref[...]) + sh_ref[...]
        h_sc[...] = h.astype(BF16)
        acc_sc[...] = jnp.zeros_like(acc_sc)

    h = h_sc[...]
    a = jnp.dot(h, wg_ref[...], preferred_element_type=F32)
    b = jnp.dot(h, wu_ref[...], preferred_element_type=F32)
    act = (_silu(a) * b).astype(BF16)
    acc_sc[...] += jnp.dot(act, wd_ref[...], preferred_element_type=F32)

    @pl.when(k == pl.num_programs(1) - 1)
    def _():
        o_ref[...] = x_ref[...] + 0.5 * gt_ref[...] * acc_sc[...]


def _ffn(x, mod, j0, g, wg, wu, wd, tm, tf):
    n = x.shape[0]
    nk = D_FF // tf
    return pl.pallas_call(
        _ffn_kernel,
        out_shape=jax.ShapeDtypeStruct((n, D_MODEL), F32),
        grid=(n // tm, nk),
        in_specs=[pl.BlockSpec((tm, D_MODEL), lambda i, k: (i, 0)),
                  mod.spec(j0, tm, 2), mod.spec(j0 + 1, tm, 2), mod.spec(j0 + 2, tm, 2),
                  pl.BlockSpec((1, D_MODEL), lambda i, k: (0, 0)),
                  pl.BlockSpec((D_MODEL, tf), lambda i, k: (0, k)),
                  pl.BlockSpec((D_MODEL, tf), lambda i, k: (0, k)),
                  pl.BlockSpec((tf, D_MODEL), lambda i, k: (k, 0))],
        out_specs=pl.BlockSpec((tm, D_MODEL), lambda i, k: (i, 0)),
        scratch_shapes=[pltpu.VMEM((tm, D_MODEL), BF16), pltpu.VMEM((tm, D_MODEL), F32)],
        compiler_params=_cp(("arbitrary", "arbitrary")),
        name="ffn",
    )(x, mod.arr, mod.arr, mod.arr, g, wg, wu, wd)


_C_QA, _C_KA, _C_VA, _C_QI = 0, 512, 1024, 1536
_C_KI, _C_WI = 2048, 2176
_C_QB, _C_FB, _C_IB, _C_GB = 2304, 2816, 3328, 3840
_N_PACK = 4352


def _mix_kernel(x_ref, sh_ref, sc_ref, g_ref, w_ref, gq_ref, gk_ref, lbl_ref, bd_ref,
                qa_o, ka_o, ka16_o, va_o, va16_o, qi_o, ki_o, ki16_o, wi_o, qb_o, f_o, vb_o, sg_o):
    h = (_rms(x_ref[...], g_ref[...]) * (1.0 + sc_ref[...]) + sh_ref[...]).astype(BF16)

    def proj(c0, width):
        return jnp.dot(h, w_ref[:, c0:c0 + width], preferred_element_type=F32)

    bd = bd_ref[...]
    qa = proj(_C_QA, W_ATTN)
    qa = qa * lax.rsqrt(_group_mean_sq(qa, bd) + RMS_EPS) * gq_ref[...]
    qa_o[...] = (qa * (D_HA ** -0.5)).astype(BF16)
    ka = proj(_C_KA, W_ATTN)
    ka = ka * lax.rsqrt(_group_mean_sq(ka, bd) + RMS_EPS) * gk_ref[...]
    ka_o[...] = ka.T
    ka16_o[...] = ka.astype(BF16)
    va_t = proj(_C_VA, W_ATTN).T
    va_o[...] = va_t
    va16_o[...] = va_t.astype(BF16)
    qi_o[...] = (proj(_C_QI, H_I * D_I) * (D_I ** -0.5)).astype(BF16)
    ki = proj(_C_KI, LANES)
    ki_o[...] = ki
    ki16_o[...] = ki.astype(BF16)
    wi_o[...] = proj(_C_WI, LANES) * (H_I ** -0.5)
    qb_o[...] = _silu(proj(_C_QB, W_HGRN))
    l0 = lbl_ref[0:1, :]
    l1 = lbl_ref[1:2, :]
    mx = jnp.maximum(l0, l1)
    e0 = jnp.exp(l0 - mx)
    e1 = jnp.exp(l1 - mx)
    lb = e0 / (e0 + e1)
    f_o[...] = lb + (1.0 - lb) * _sigmoid(proj(_C_FB, W_HGRN))
    vb_o[...] = proj(_C_IB, W_HGRN)
    sg_o[...] = _silu(proj(_C_GB, W_HGRN))


def _mix(x, mod, g, wpack, gq, gk, lbl, bd, tm, rows_per_group):
    n = x.shape[0]
    tpg = rows_per_group // tm
    row = lambda w: pl.BlockSpec((tm, w), lambda i: (i, 0))
    full = lambda a: pl.BlockSpec(a.shape, lambda i: (0,) * a.ndim)
    featmajor = pl.BlockSpec((None, W_ATTN, tm), lambda i: (i // tpg, 0, i % tpg))
    outs = [(W_ATTN, BF16), None, (W_ATTN, BF16), None, None,
            (H_I * D_I, BF16), (LANES, F32), (LANES, BF16), (LANES, F32),
            (W_HGRN, F32), (W_HGRN, F32), (W_HGRN, F32), (W_HGRN, F32)]
    fm_dtypes = {1: F32, 3: F32, 4: BF16}
    return pl.pallas_call(
        _mix_kernel,
        out_shape=[jax.ShapeDtypeStruct((n // rows_per_group, W_ATTN, rows_per_group), fm_dtypes[k]) if o is None
                   else jax.ShapeDtypeStruct((n, o[0]), o[1]) for k, o in enumerate(outs)],
        grid=(n // tm,),
        in_specs=[row(D_MODEL), mod.spec(3, tm, 1), mod.spec(4, tm, 1), full(g), full(wpack),
                  full(gq), full(gk), full(lbl), full(bd)],
        out_specs=[featmajor if o is None else row(o[0]) for o in outs],
        compiler_params=_cp(("arbitrary",)),
        name="mix",
    )(x, mod.arr, mod.arr, g, wpack, gq, gk, lbl, bd)


def _t5_bucket(dist):
    n = jnp.maximum(dist, 0)
    max_exact = T5_BUCKETS // 2
    nf = jnp.maximum(n, 1).astype(F32)
    large = max_exact + (jnp.log(nf / max_exact) / math.log(T5_MAX_DIST / max_exact)
                         * (T5_BUCKETS - max_exact)).astype(I32)
    large = jnp.minimum(large, T5_BUCKETS - 1)
    return jnp.where(n < max_exact, n, large)


def _bias_kernel(rb_ref, bp_ref, bs_ref):
    def lookup(bucket, h):
        val = jnp.zeros(bucket.shape, F32)
        for b in range(T5_BUCKETS):
            val = jnp.where(bucket == b, rb_ref[b, h], val)
        return val

    k = lax.broadcasted_iota(I32, (2 * LANES, LANES), 0)
    q = lax.broadcasted_iota(I32, (2 * LANES, LANES), 1)
    far_bucket = _t5_bucket(jnp.full((2 * LANES, LANES), 2 * LANES, I32))
    for v, dist in enumerate((q - k, LANES + q - k)):
        bucket = _t5_bucket(dist)
        for h in range(H_A):
            bp_ref[v, h // 2, :, (h % 2) * LANES:(h % 2 + 1) * LANES] = lookup(bucket, h) - lookup(far_bucket, h)

    t = lax.broadcasted_iota(I32, (SUBLANES, LANES), 0)
    u = lax.broadcasted_iota(I32, (SUBLANES, LANES), 1)
    sdists = (PAGE + t - u, t - u, jnp.full((SUBLANES, LANES), 2 * LANES, I32))
    for m in range(3):
        bucket = _t5_bucket(sdists[m])
        for h in range(H_A):
            bs_ref[m, h * SUBLANES:(h + 1) * SUBLANES, :] = lookup(bucket, h)


def _bias_tiles(rel_bias):
    return pl.pallas_call(
        _bias_kernel,
        out_shape=[jax.ShapeDtypeStruct((2, H_A // 2, 2 * LANES, 2 * LANES), F32),
                   jax.ShapeDtypeStruct((3, H_A * SUBLANES, LANES), F32)],
        in_specs=[pl.BlockSpec(memory_space=pltpu.SMEM)],
        out_specs=[pl.BlockSpec(memory_space=pltpu.VMEM), pl.BlockSpec(memory_space=pltpu.VMEM)],
        name="t5_bias",
    )(rel_bias)


def _sortable_key(score):
    bits = pltpu.bitcast(score, I32)
    return jnp.where(bits < 0, INT_MIN - bits, bits)


def _tree_reduce(x, op):
    while x.shape[0] > 1:
        half = x.shape[0] // 2
        y = op(x[:half], x[half:2 * half])
        x = y if x.shape[0] % 2 == 0 else jnp.concatenate([y, x[2 * half:]], axis=0)
    return x[0]


def _sublane_allreduce(x, op):
    for sh in (4, 2, 1):
        x = op(x, pltpu.roll(x, sh, 0))
    return x


def _topk_maskbias_t(keys_ref, maskb_ref, far_ref, nch, cw, n_sel, idx_bits, far_limit):
    ng = cw // SUBLANES
    sub = lax.broadcasted_iota(I32, (ng, SUBLANES, LANES), 0) * SUBLANES + lax.broadcasted_iota(
        I32, (ng, SUBLANES, LANES), 1)

    def count(indicator):
        def body(c, acc):
            k0 = pl.multiple_of(c * cw, cw)
            kk = keys_ref[pl.ds(k0, cw), :].reshape(ng, SUBLANES, LANES)
            return acc + _tree_reduce(indicator(kk, k0), jnp.add)
        acc = lax.fori_loop(0, nch, body, jnp.zeros((SUBLANES, LANES), F32))
        return _sublane_allreduce(acc, jnp.add)

    def bisect(it, ans):
        cand = ans | jnp.left_shift(jnp.int32(1), 31 - it)
        cs = cand ^ INT_MIN
        cnt = count(lambda kk, k0: jnp.where(kk >= cs[None], 1.0, 0.0))
        return jnp.where(cnt >= n_sel, cand, ans)

    thr = lax.fori_loop(0, 32, bisect, jnp.zeros((SUBLANES, LANES), I32)) ^ INT_MIN
    need = n_sel - count(lambda kk, k0: jnp.where(kk > thr[None], 1.0, 0.0))
    n_eq = count(lambda kk, k0: jnp.where(kk == thr[None], 1.0, 0.0))
    has_tie = jnp.max(jnp.where(n_eq > need, 1.0, 0.0)) > 0.0

    def tie_search(_):
        def step(it, aj):
            cand = aj | jnp.left_shift(jnp.int32(1), idx_bits - 1 - it)
            cnt = count(lambda kk, k0: jnp.where(kk == thr[None], jnp.where((k0 + sub) < cand[None], 1.0, 0.0), 0.0))
            return jnp.where(cnt < need, cand, aj)
        return lax.fori_loop(0, idx_bits, step, jnp.zeros((SUBLANES, LANES), I32))

    cut = lax.cond(has_tie, tie_search, lambda _: jnp.full((SUBLANES, LANES), 2 ** 30, I32), 0)
    cut = jnp.where(thr == INT_MIN, -1, cut)

    def fin(c, _):
        k0 = pl.multiple_of(c * cw, cw)
        kk = keys_ref[pl.ds(k0, cw), :].reshape(ng, SUBLANES, LANES)
        kpos = k0 + sub
        tie = jnp.where(kpos <= cut[None], 0.0, NEG)
        mb = jnp.where(kk > thr[None], 0.0, jnp.where(kk == thr[None], tie, NEG))
        maskb_ref[pl.ds(k0, cw), :] = mb.reshape(cw, LANES)
        if far_ref is not None:
            far_ref[pl.ds(k0, cw), :] = jnp.where(kpos < far_limit, mb, NEG).reshape(cw, LANES)
        return 0

    lax.fori_loop(0, nch, fin, 0)


def _softmax_step(carry, s, v16, v_is_transposed=False):
    m, l, acc = carry
    mn = jnp.maximum(m, jnp.max(s, axis=1, keepdims=True))
    alpha = jnp.exp(m - mn)
    p = jnp.exp(s - mn)
    l = alpha * l + jnp.sum(p, axis=1, keepdims=True)
    if v_is_transposed:
        pv = lax.dot_general(p.astype(BF16), v16, _NT, preferred_element_type=F32)
    else:
        pv = jnp.dot(p.astype(BF16), v16, preferred_element_type=F32)
    return mn, l, alpha * acc + pv


_QB = 128
_KC = 512
_FW = 512


def _prompt_attn_step(step, nsteps, i, qa_ref, qi_ref, wi_ref, ki_ref, ka_ref, vat_ref, bias_ref, o_ref,
                      keys_sc, maskb_sc, far_sc, qs_sc, q2_sc, m_sc, l_sc, acc_sc):
    lane = lax.broadcasted_iota(I32, (_QB, LANES), 1)
    lo = lane < D_I
    nch = i // (_KC // _QB) + 1
    near0 = jnp.maximum(i - 1, 0) * _QB
    npair = H_A // 2
    cols2 = 2 * _QB

    @pl.when(step == 0)
    def _():
        for h in range(H_I):
            pair = qi_ref[:, (h // 2) * LANES:(h // 2 + 1) * LANES].astype(F32)
            keep = jnp.where(lo, pair, 0.0) if h % 2 == 0 else jnp.where(lo, 0.0, pair)
            qs_sc[h * _QB:(h + 1) * _QB, :] = keep.astype(BF16)

        qpos = i * _QB + lax.broadcasted_iota(I32, (_KC, LANES), 1)

        def scores(c, _):
            k0 = pl.multiple_of(c * _KC, _KC)
            s = lax.dot_general(ki_ref[pl.ds(k0, _KC), :], qs_sc[...], _NT, preferred_element_type=F32)
            sc = jnp.zeros((_KC, LANES), F32)
            for h in range(H_I):
                sc = sc + jnp.maximum(s[:, h * _QB:(h + 1) * _QB], 0.0) * wi_ref[h:h + 1, :]
            kpos = k0 + lax.broadcasted_iota(I32, (_KC, LANES), 0)
            keys_sc[pl.ds(k0, _KC), :] = jnp.where(kpos <= qpos, _sortable_key(sc), INT_MIN)
            return 0

        lax.fori_loop(0, nch, scores, 0)
        _topk_maskbias_t(keys_sc, maskb_sc, far_sc, nch, _KC, TOPK_MAX, 12, near0)

        for p in range(npair):
            qp = qa_ref[:, p * LANES:(p + 1) * LANES].astype(F32)
            q2_sc[p * cols2:(p + 1) * cols2, :] = jnp.concatenate(
                [jnp.where(lo, qp, 0.0), jnp.where(lo, 0.0, qp)], axis=0).astype(BF16)
        m_sc[...] = jnp.full(m_sc.shape, NEG, F32)
        l_sc[...] = jnp.zeros_like(l_sc)
        acc_sc[...] = jnp.zeros_like(acc_sc)

    def update_all(k0, width, mask_ref, biases):
        mb = mask_ref[pl.ds(k0, width), :]
        mb2 = jnp.concatenate([mb, mb], axis=1)
        logits = []
        for p in range(npair):
            s = lax.dot_general(ka_ref[pl.ds(k0, width), p * LANES:(p + 1) * LANES],
                                q2_sc[p * cols2:(p + 1) * cols2, :], _NT, preferred_element_type=F32) + mb2
            if biases is not None:
                s = s + biases[p]
            logits.append(s.reshape(width // SUBLANES, SUBLANES, cols2))
        probs = []
        for p in range(npair):
            m_old = m_sc[p]
            m_new = jnp.maximum(m_old, _sublane_allreduce(_tree_reduce(logits[p], jnp.maximum), jnp.maximum))
            alpha = jnp.exp(m_old - m_new)
            pexp = jnp.exp(logits[p] - m_new[None])
            m_sc[p] = m_new
            l_sc[p] = alpha * l_sc[p] + _tree_reduce(pexp, jnp.add)
            probs.append((pexp.reshape(width, cols2).astype(BF16), alpha))
        for p in range(npair):
            p16, alpha = probs[p]
            pv = jnp.dot(vat_ref[p * LANES:(p + 1) * LANES, pl.ds(k0, width)], p16,
                         preferred_element_type=F32)
            acc = acc_sc[p].reshape(LANES // SUBLANES, SUBLANES, cols2) * alpha[None]
            acc_sc[p] = acc.reshape(LANES, cols2) + pv

    def far(j, _):
        update_all(pl.multiple_of((step + j * nsteps) * _FW, _FW), _FW, far_sc, None)
        return 0

    nfar = (near0 + _FW - 1) // _FW
    lax.fori_loop(0, (nfar - step + nsteps - 1) // nsteps, far, 0)

    @pl.when(step == nsteps - 1)
    def _():
        first = jnp.minimum(i, 1)
        update_all(pl.multiple_of(near0, _QB), 2 * _QB, maskb_sc, [bias_ref[first, p] for p in range(npair)])
        for p in range(npair):
            l = _sublane_allreduce(l_sc[p], jnp.add)
            o = (acc_sc[p].reshape(LANES // SUBLANES, SUBLANES, cols2) / l[None]).reshape(LANES, cols2)
            ot = jnp.concatenate([o[:D_HA, :_QB], o[D_HA:, _QB:]], axis=0)
            o_ref[:, p * LANES:(p + 1) * LANES] = ot.T


_PPS = 16
_TP = SUBLANES
_TK = 2 * SUBLANES


def _sample_scores_kernel(pt_ref, q_ref, w_ref, kn_ref, *rest):
    pages, o_ref = rest[:_PPS], rest[_PPS]
    s_id = pl.program_id(1)
    q = q_ref[...]

    def head_sum(s):
        n = s.shape[1]
        out = jnp.zeros((_TP, n), F32)
        for h in range(H_I):
            wh = w_ref[h * _TP:(h + 1) * _TP, :]
            out = out + jnp.maximum(s[h * _TP:(h + 1) * _TP, :], 0.0) * jnp.concatenate([wh] * (n // LANES), axis=1)
        return out

    kc = jnp.concatenate([pg[...] for pg in pages], axis=1).astype(BF16)
    s = jnp.dot(q, kc, preferred_element_type=F32)
    o_ref[:, pl.ds(pl.multiple_of(s_id * (_PPS * PAGE), _PPS * PAGE), _PPS * PAGE)] = head_sum(s)

    @pl.when(s_id == pl.num_programs(1) - 1)
    def _():
        kn = jnp.concatenate([kn_ref[...], jnp.zeros((LANES - _TK, D_I), BF16)], axis=0)
        sn = lax.dot_general(q, kn, _NT, preferred_element_type=F32)
        o_ref[:, o_ref.shape[1] - LANES:] = head_sum(sn)


def _sample_scores(page_table, q_st, w_b, ki_new, cache_kidx, n_pages):
    nseq = q_st.shape[0]
    steps = n_pages // _PPS
    lpad = n_pages * PAGE + LANES
    page_specs = [pl.BlockSpec((None, D_I, PAGE), functools.partial(
        lambda b, s, pt, p: (pt[b, s * _PPS + p], 0, 0), p=p)) for p in range(_PPS)]
    return pl.pallas_call(
        _sample_scores_kernel,
        out_shape=jax.ShapeDtypeStruct((nseq, _TP, lpad), F32),
        grid_spec=pltpu.PrefetchScalarGridSpec(
            num_scalar_prefetch=1, grid=(nseq, steps),
            in_specs=[pl.BlockSpec((None, H_I * _TP, D_I), lambda b, s, pt: (b, 0, 0)),
                      pl.BlockSpec((None, H_I * _TP, LANES), lambda b, s, pt: (b, 0, 0)),
                      pl.BlockSpec((None, _TK, D_I), lambda b, s, pt: (b, 0, 0))] + page_specs,
            out_specs=pl.BlockSpec((None, _TP, lpad), lambda b, s, pt: (b, 0, 0))),
        compiler_params=_cp(("arbitrary", "arbitrary")),
        name="sample_scores",
    )(page_table, q_st, w_b, ki_new, *([cache_kidx] * _PPS))


_SEL_CW = 640


def _sample_select_kernel(s_ref, o_ref, keys_sc, *, past):
    lpad = s_ref.shape[0]
    t = lax.broadcasted_iota(I32, (_SEL_CW, LANES), 1) % _TP
    for c in range(lpad // _SEL_CW):
        sl = slice(c * _SEL_CW, (c + 1) * _SEL_CW)
        kpos = c * _SEL_CW + lax.broadcasted_iota(I32, (_SEL_CW, LANES), 0)
        keys_sc[sl, :] = jnp.where(kpos <= past + t, _sortable_key(s_ref[sl, :]), INT_MIN)
    _topk_maskbias_t(keys_sc, o_ref, None, lpad // _SEL_CW, _SEL_CW, TOPK_MAX, 14, None)


def _sample_select(scores_t, past):
    lpad, cols = scores_t.shape
    return pl.pallas_call(
        functools.partial(_sample_select_kernel, past=past),
        out_shape=jax.ShapeDtypeStruct((lpad, cols), F32),
        grid=(cols // LANES,),
        in_specs=[pl.BlockSpec((lpad, LANES), lambda g: (0, g))],
        out_specs=pl.BlockSpec((lpad, LANES), lambda g: (0, g)),
        scratch_shapes=[pltpu.VMEM((lpad, LANES), I32)],
        compiler_params=_cp(("arbitrary",)),
        name="sample_select",
    )(scores_t)


def _sample_attn_step(s_id, last, q_ref, mb_ref, kn_ref, vn_ref, bias_ref, kp, vp, o_ref,
                      qblk_sc, m_sc, l_sc, acc_sc):
    rows = H_A * _TP
    span = _PPS * PAGE

    @pl.when(s_id == 0)
    def _():
        q = jnp.concatenate([q_ref[...]] * H_A, axis=0)
        r = lax.broadcasted_iota(I32, (rows, W_ATTN), 0) // _TP
        c = lax.broadcasted_iota(I32, (rows, W_ATTN), 1) // D_HA
        qblk_sc[...] = jnp.where(r == c, q, 0.0).astype(BF16)
        m_sc[...] = jnp.full(m_sc.shape, NEG, F32)
        l_sc[...] = jnp.zeros_like(l_sc)
        acc_sc[...] = jnp.zeros_like(acc_sc)

    def update(s, v16, v_is_transposed):
        carry = (m_sc[:, 0:1], l_sc[:, 0:1], acc_sc[...])
        m, l, acc = _softmax_step(carry, s, v16, v_is_transposed)
        m_sc[...] = jnp.broadcast_to(m, m_sc.shape)
        l_sc[...] = jnp.broadcast_to(l, l_sc.shape)
        acc_sc[...] = acc

    k16 = jnp.concatenate([pg[...].astype(BF16) for pg in kp], axis=1)
    v16 = jnp.concatenate([pg[...].astype(BF16) for pg in vp], axis=1)
    s = jnp.dot(qblk_sc[...], k16, preferred_element_type=F32)
    mb = mb_ref[:, pl.ds(pl.multiple_of(s_id * span, span), span)]
    last_tile = jnp.where(s_id == last, bias_ref[0], bias_ref[2])
    bias = jnp.concatenate([bias_ref[2]] * (_PPS - 1) + [last_tile], axis=1)
    update(s + jnp.concatenate([mb] * H_A, axis=0) + bias, v16, True)

    @pl.when(s_id == last)
    def _():
        pad = jnp.zeros((LANES - _TK, W_ATTN), BF16)
        kn = jnp.concatenate([kn_ref[...], pad], axis=0)
        vn = jnp.concatenate([vn_ref[...], pad], axis=0)
        sn = lax.dot_general(qblk_sc[...], kn, _NT, preferred_element_type=F32)
        mbn = mb_ref[:, mb_ref.shape[1] - LANES:]
        update(sn + jnp.concatenate([mbn] * H_A, axis=0) + bias_ref[1], vn, False)
        o = acc_sc[...] / l_sc[:, 0:1]
        c = lax.broadcasted_iota(I32, (_TP, W_ATTN), 1) // D_HA
        out = jnp.zeros((_TP, W_ATTN), F32)
        for h in range(H_A):
            out = jnp.where(c == h, o[h * _TP:(h + 1) * _TP, :], out)
        o_ref[...] = out


def _attn_kernel(pt_ref, qa_ref, qi_ref, wi_ref, ki_ref, ka_ref, vat_ref, bias_p_ref,
                 sq_ref, smb_ref, skn_ref, svn_ref, bias_s_ref, *rest):
    kp, vp = rest[:_PPS], rest[_PPS:2 * _PPS]
    o_ref, so_ref = rest[2 * _PPS:2 * _PPS + 2]
    prompt_scratch = rest[2 * _PPS + 2:2 * _PPS + 10]
    sample_scratch = rest[2 * _PPS + 10:]
    step = pl.program_id(2)
    nsteps = pl.num_programs(2)
    _prompt_attn_step(step, nsteps, pl.program_id(1), qa_ref, qi_ref, wi_ref, ki_ref, ka_ref, vat_ref, bias_p_ref,
                      o_ref, *prompt_scratch)
    _sample_attn_step(step, nsteps - 1, sq_ref, smb_ref, skn_ref, svn_ref, bias_s_ref, kp, vp, so_ref,
                      *sample_scratch)


def _attention(qa16, qi16, wi_t, ki16, ka16, va16_t, bias_p, nbatch, seq,
               page_table, q8, maskb, k_new, v_new, bias_s, cache_k, cache_v, n_pages):
    nq = seq // _QB
    nseq = q8.shape[0]
    steps = n_pages // _PPS
    lpad = maskb.shape[1]
    npair = H_A // 2
    assert nseq == nbatch * nq, "one decode sequence per prompt query block"
    blk = lambda w: pl.BlockSpec((_QB, w), lambda b, i, s, pt: (b * nq + i, 0))
    once = pl.Buffered(1)
    per_b = lambda w: pl.BlockSpec((seq, w), lambda b, i, s, pt: (b, 0), pipeline_mode=once)
    pspec = lambda: [pl.BlockSpec((None, W_ATTN, PAGE), functools.partial(
        lambda b, i, s, pt, p: (pt[b * nq + i, s * _PPS + p], 0, 0), p=p)) for p in range(_PPS)]
    seq3 = lambda t: pl.BlockSpec((None, t, W_ATTN), lambda b, i, s, pt: (b * nq + i, 0, 0))
    return pl.pallas_call(
        _attn_kernel,
        out_shape=[jax.ShapeDtypeStruct((nbatch * seq, W_ATTN), F32),
                   jax.ShapeDtypeStruct((nseq, _TP, W_ATTN), F32)],
        grid_spec=pltpu.PrefetchScalarGridSpec(
            num_scalar_prefetch=1, grid=(nbatch, nq, steps),
            in_specs=[blk(W_ATTN), blk(H_I * D_I),
                      pl.BlockSpec((H_I, _QB), lambda b, i, s, pt: (0, b * nq + i)),
                      per_b(LANES), per_b(W_ATTN),
                      pl.BlockSpec((None, W_ATTN, seq), lambda b, i, s, pt: (b, 0, 0), pipeline_mode=once),
                      pl.BlockSpec(bias_p.shape, lambda b, i, s, pt: (0, 0, 0, 0), pipeline_mode=once),
                      seq3(_TP), pl.BlockSpec((_TP, lpad), lambda b, i, s, pt: (b * nq + i, 0)),
                      seq3(_TK), seq3(_TK),
                      pl.BlockSpec(bias_s.shape, lambda b, i, s, pt: (0, 0, 0), pipeline_mode=once)]
                     + pspec() + pspec(),
            out_specs=[blk(W_ATTN), seq3(_TP)],
            scratch_shapes=[pltpu.VMEM((seq, LANES), I32), pltpu.VMEM((seq, LANES), F32),
                            pltpu.VMEM((seq, LANES), F32),
                            pltpu.VMEM((H_I * _QB, LANES), BF16), pltpu.VMEM((H_A * _QB, LANES), BF16),
                            pltpu.VMEM((npair, SUBLANES, 2 * _QB), F32), pltpu.VMEM((npair, SUBLANES, 2 * _QB), F32),
                            pltpu.VMEM((npair, LANES, 2 * _QB), F32),
                            pltpu.VMEM((H_A * _TP, W_ATTN), BF16), pltpu.VMEM((H_A * _TP, LANES), F32),
                            pltpu.VMEM((H_A * _TP, LANES), F32), pltpu.VMEM((H_A * _TP, W_ATTN), F32)]),
        compiler_params=_cp(("arbitrary", "arbitrary", "arbitrary")),
        name="attention",
    )(page_table, qa16, qi16, wi_t, ki16, ka16, va16_t, bias_p, q8, maskb, k_new, v_new, bias_s,
      *([cache_k] * _PPS), *([cache_v] * _PPS))


def _hgrn_chunk(q, k, v, logf, st, c):
    row = lax.broadcasted_iota(I32, (c, DK_B), 0)
    g = logf
    sh = 1
    while sh < c:
        g = g + jnp.where(row >= sh, pltpu.roll(g, sh, 0), 0.0)
        sh *= 2

    o = lax.dot_general((q * jnp.exp(g)).astype(BF16), st.astype(BF16), _NT, preferred_element_type=F32)

    if c > SUBLANES:
        rr = lax.broadcasted_iota(I32, (c, c), 0)
        cc = lax.broadcasted_iota(I32, (c, c), 1)
        a = jnp.zeros((c, c), F32)
        w = SUBLANES
        while w < c:
            nb = c // (2 * w)
            gb = jnp.broadcast_to(g.reshape(nb, 2 * w, DK_B)[:, w - 1:w, :], (nb, 2 * w, DK_B)).reshape(c, DK_B)
            right = (row % (2 * w)) >= w
            qt = jnp.where(right, q * jnp.exp(jnp.where(right, g - gb, 0.0)), 0.0)
            kt = jnp.where(right, 0.0, k * jnp.exp(jnp.where(right, 0.0, gb - g)))
            aw = lax.dot_general(qt.astype(BF16), kt.astype(BF16), _NT, preferred_element_type=F32)
            a = a + jnp.where((rr // (2 * w)) == (cc // (2 * w)), aw, 0.0)
            w *= 2
        o = o + jnp.dot(a.astype(BF16), v.astype(BF16), preferred_element_type=F32)

    o = o + jnp.sum(q * k, axis=1, keepdims=True) * v
    for d in range(1, SUBLANES):
        ok = (row % SUBLANES) >= d
        e = jnp.exp(jnp.where(ok, g - pltpu.roll(g, d, 0), 0.0))
        coef = jnp.sum(jnp.where(ok, q * pltpu.roll(k, d, 0) * e, 0.0), axis=1, keepdims=True)
        o = o + coef * pltpu.roll(v, d, 0)

    g_end = g[c - 1:c, :]
    kd = k * jnp.exp(g_end - g)
    st = st * jnp.exp(g_end) + lax.dot_general(v.astype(BF16), kd.astype(BF16), _TN, preferred_element_type=F32)
    return o, st


_HG = 2


def _hgrn_kernel(q_ref, f_ref, v_ref, s0_ref, o_ref, s_ref, *, nb, t_len, c):
    for j in range(nb):
        def body(ci, states):
            r = pl.ds(pl.multiple_of(ci * c, c), c)
            new_states = []
            for g in range(_HG):
                cs = slice(g * DK_B, (g + 1) * DK_B)
                f = f_ref[j, r, cs]
                o, st = _hgrn_chunk(q_ref[j, r, cs], 1.0 - f, v_ref[j, r, cs], jnp.log(f), states[g], c)
                o_ref[j, r, cs] = o
                new_states.append(st)
            return tuple(new_states)
        states = lax.fori_loop(0, t_len // c, body, tuple(s0_ref[j, g].T for g in range(_HG)))
        for g in range(_HG):
            s_ref[j, g] = states[g].T


def _hgrn(q, f, v, s0, nb, c):
    nseq, t_len, _ = q.shape
    tok = pl.BlockSpec((nb, t_len, _HG * DK_B), lambda b, h: (b, 0, h))
    st = pl.BlockSpec((nb, _HG, DK_B, DV_B), lambda b, h: (b, h, 0, 0))
    return pl.pallas_call(
        functools.partial(_hgrn_kernel, nb=nb, t_len=t_len, c=c),
        out_shape=[jax.ShapeDtypeStruct((nseq, t_len, W_HGRN), F32),
                   jax.ShapeDtypeStruct((nseq, H_B, DK_B, DV_B), F32)],
        grid=(nseq // nb, H_B // _HG),
        in_specs=[tok, tok, tok, st],
        out_specs=[tok, st],
        compiler_params=_cp(("arbitrary", "arbitrary")),
        name="hgrn",
    )(q, f, v, s0)


def _merge_kernel(oa_ref, oh_ref, sg_ref, x_ref, gt_ref, beta_ref, gh_ref, bd_ref, w_ref, o_ref):
    oa = oa_ref[...]
    a = oa * lax.rsqrt(_group_mean_sq(oa, bd_ref[...]) + RMS_EPS) * beta_ref[...]
    oh = oh_ref[...]
    parts = []
    for h in range(H_B):
        sl = slice(h * DV_B, (h + 1) * DV_B)
        parts.append(_rms(oh[:, sl], gh_ref[:, sl]))
    r = jnp.concatenate(parts, axis=1) * sg_ref[...]
    y = (jnp.dot(a.astype(BF16), w_ref[:W_ATTN, :], preferred_element_type=F32)
         + jnp.dot(r.astype(BF16), w_ref[W_ATTN:, :], preferred_element_type=F32))
    o_ref[...] = x_ref[...] + gt_ref[...] * y


def _merge(oa, oh, sg, x, mod, beta, gh, bd, w16, tm):
    n = x.shape[0]
    row = lambda w: pl.BlockSpec((tm, w), lambda i: (i, 0))
    full = lambda a: pl.BlockSpec(a.shape, lambda i: (0,) * a.ndim)
    return pl.pallas_call(
        _merge_kernel,
        out_shape=jax.ShapeDtypeStruct((n, D_MODEL), F32),
        grid=(n // tm,),
        in_specs=[row(W_ATTN), row(W_HGRN), row(W_HGRN), row(D_MODEL), mod.spec(5, tm, 1),
                  full(beta), full(gh), full(bd), full(w16)],
        out_specs=row(D_MODEL),
        compiler_params=_cp(("arbitrary",)),
        name="merge",
    )(oa, oh, sg, x, mod.arr, beta, gh, bd, w16)


def _pad_tokens(a, value=0.0, to=_TP):
    return jnp.pad(a, ((0, 0), (0, to - a.shape[1]), (0, 0)), constant_values=value)


def kernel(x_prompt, x_sample, cache_k, cache_v, cache_kidx, state_hgrn, page_table, c_prompt, c_sample,
           w_ada, b_ada, g_ffn1, ffn1_w_gate, ffn1_w_up, ffn1_w_down, g_mix, w_in, g_q, g_k,
           beta_attn, g_hgrn, w_out, g_ffn2, ffn2_w_gate, ffn2_w_up, ffn2_w_down, rel_bias, lb_logits):
    assert w_ada.shape[0] == 1, "single-layer problem"
    nb_p, seq, _ = x_prompt.shape
    nb_s, t_dec, _ = x_sample.shape
    n_phys = cache_k.shape[1]
    n_pages = page_table.shape[1]
    past = n_pages * PAGE
    assert t_dec <= _TP and n_pages % _PPS == 0 and (past + LANES) % _SEL_CW == 0

    row2 = lambda a: a.reshape(1, -1)
    w_in0 = w_in[0]
    pad_w = jnp.zeros((D_MODEL, LANES - H_I), F32)
    wpack = jnp.concatenate(
        [w_in0[:, :2048], w_in0[:, 2048:2112], w_in0[:, 2048:2112], w_in0[:, 2112:2120], pad_w,
         w_in0[:, 2120:]], axis=1).astype(BF16)
    assert wpack.shape[1] == _N_PACK
    gq = row2(jnp.tile(g_q[0], H_A))
    gk = row2(jnp.tile(g_k[0], H_A))
    gidx = jnp.arange(W_ATTN) // D_HA
    bd = jnp.where(gidx[:, None] == gidx[None, :], 1.0 / D_HA, 0.0).astype(BF16)
    ffn1 = (row2(g_ffn1[0]), ffn1_w_gate[0].astype(BF16), ffn1_w_up[0].astype(BF16), ffn1_w_down[0].astype(BF16))
    ffn2 = (row2(g_ffn2[0]), ffn2_w_gate[0].astype(BF16), ffn2_w_up[0].astype(BF16), ffn2_w_down[0].astype(BF16))
    w_out16 = w_out[0].astype(BF16)

    n_c = nb_p + nb_s
    c_pad = (-n_c) % SUBLANES
    c_all = jnp.concatenate([c_prompt, c_sample, jnp.zeros((c_pad, D_MODEL), F32)], axis=0)
    mods = _ada(c_all, w_ada[0].astype(BF16), row2(b_ada[0]))
    bias_p, bias_s = _bias_tiles(rel_bias)

    def front(x, mod, tm, tf, rows_per_group):
        x1 = _ffn(x, mod, 0, *ffn1, tm, tf)
        return x1, _mix(x1, mod, row2(g_mix[0]), wpack, gq, gk, lb_logits, bd, tm, rows_per_group)

    def back(x1, mod, tm, tf, oa, oh, sg):
        x2 = _merge(oa, oh, sg, x1, mod, row2(beta_attn[0]), row2(g_hgrn[0]), bd, w_out16, tm)
        return _ffn(x2, mod, 6, *ffn2, tm, tf)

    tm_p, tf = 512, D_FF // 2
    n_tok = nb_s * t_dec
    mod_p = _Mod(mods[:, :nb_p].reshape(N_MOD, nb_p, 1, D_MODEL), False, seq // tm_p)
    mod_s = _Mod(jnp.repeat(mods[:, nb_p:nb_p + nb_s], t_dec, axis=1), True)
    x1p, proj_p = front(x_prompt.reshape(nb_p * seq, D_MODEL), mod_p, tm_p, tf, seq)
    x1s, proj_s = front(x_sample.reshape(n_tok, D_MODEL), mod_s, n_tok, tf, n_tok)
    qa16, kp_t, ka16, vp_t, va16_t, qi16, ki_p, ki16, wi, qb, f, vb, sg_p = proj_p
    qa16s, ks_t, ka16s, vs_t, va16s_t, qi16s, ki_s, ki16s, wis, qbs, fs, vbs, sg_s = proj_s
    va16s = va16s_t[0].T

    ck = jnp.transpose(cache_k[0], (0, 2, 3, 1)).reshape(n_phys, W_ATTN, PAGE)
    cv = jnp.transpose(cache_v[0], (0, 2, 3, 1)).reshape(n_phys, W_ATTN, PAGE)
    cki = jnp.transpose(cache_kidx[0], (0, 2, 1))
    seq3 = lambda a: a.reshape(nb_s, t_dec, a.shape[-1])
    q_st = _pad_tokens(seq3(qi16s)).reshape(nb_s, _TP, H_I, D_I).transpose(0, 2, 1, 3).reshape(nb_s, H_I * _TP, D_I)
    w_st = _pad_tokens(seq3(wis[:, :H_I])).transpose(0, 2, 1).reshape(nb_s, H_I * _TP, 1)
    w_b = jnp.broadcast_to(w_st, (nb_s, H_I * _TP, LANES))
    scores = _sample_scores(page_table, q_st, w_b, _pad_tokens(seq3(ki16s[:, :D_I]), to=_TK), cki, n_pages)
    maskb = _sample_select(scores.reshape(nb_s * _TP, -1).T, past).T

    oa_p, oa8 = _attention(qa16, qi16, wi[:, :H_I].T, ki16, ka16, va16_t, bias_p, nb_p, seq,
                           page_table, _pad_tokens(seq3(qa16s.astype(F32))), maskb,
                           _pad_tokens(seq3(ka16s), to=_TK), _pad_tokens(seq3(va16s), to=_TK), bias_s, ck, cv, n_pages)
    r3 = lambda a: a.reshape(nb_p, seq, W_HGRN)
    oh_p, sp = _hgrn(r3(qb), r3(f), r3(vb), jnp.zeros((nb_p, H_B, DK_B, DV_B), F32), 1, 64)
    oh8, ss = _hgrn(_pad_tokens(seq3(qbs)), _pad_tokens(seq3(fs), 1.0), _pad_tokens(seq3(vbs)), state_hgrn[0], 8, _TP)

    yp = back(x1p, mod_p, tm_p, tf, oa_p, oh_p.reshape(nb_p * seq, W_HGRN), sg_p)
    ys = back(x1s, mod_s, n_tok, tf, oa8[:, :t_dec].reshape(n_tok, W_ATTN),
              oh8[:, :t_dec].reshape(n_tok, W_HGRN), sg_s)

    def rows_out(a_t, nseq, t_len):
        a = a_t.reshape(-1, H_A, D_HA, a_t.shape[-1]).transpose(0, 3, 1, 2)
        return a.reshape(1, nseq, t_len, H_A, D_HA)

    return (yp.reshape(nb_p, seq, D_MODEL).astype(x_prompt.dtype),
            ys.reshape(nb_s, t_dec, D_MODEL).astype(x_sample.dtype),
            rows_out(kp_t, nb_p, seq), rows_out(vp_t, nb_p, seq),
            ki_p[:, :D_I].reshape(1, nb_p, seq, D_I), sp[None],
            rows_out(ks_t, nb_s, t_dec), rows_out(vs_t, nb_s, t_dec),
            ki_s[:, :D_I].reshape(1, nb_s, t_dec, D_I), ss[None])
```

```python
import functools
import math

import jax
import jax.numpy as jnp
from jax import lax
from jax.experimental import pallas as pl
from jax.experimental.pallas import tpu as pltpu

F32 = jnp.float32
BF16 = jnp.bfloat16
I32 = jnp.int32

D_MODEL = 1024
W_ATTN = 512
W_HGRN = 512
D_HA = 64
H_A = 8
H_I = 8
D_I = 64
TOPK_MAX = 256
H_B = 4
DK_B = 128
DV_B = 128
D_FF = 2816
T5_BUCKETS = 32
T5_MAX_DIST = 128
RMS_EPS = 1e-6
N_MOD = 9
PAGE = 128

LANES = 128
SUBLANES = 8
NEG = -1e30
INT_MIN = -(2 ** 31)
VMEM_LIMIT = 56 * 1024 * 1024

_NT = (((1,), (1,)), ((), ()))
_TN = (((0,), (0,)), ((), ()))


def _cp(sem):
    return pltpu.CompilerParams(dimension_semantics=sem, vmem_limit_bytes=VMEM_LIMIT)


def _silu(x):
    return x / (1.0 + jnp.exp(-x))


def _sigmoid(x):
    return 1.0 / (1.0 + jnp.exp(-x))


def _rms(x, g):
    return x * lax.rsqrt(jnp.mean(x * x, axis=-1, keepdims=True) + RMS_EPS) * g


def _group_mean_sq(x, bd):
    xx = x * x
    hi = xx.astype(BF16)
    lo = (xx - hi.astype(F32)).astype(BF16)
    return (jnp.dot(hi, bd, preferred_element_type=F32)
            + jnp.dot(lo, bd, preferred_element_type=F32))


def _ada_kernel(c_ref, w_ref, b_ref, o_ref):
    a = _silu(c_ref[...]).astype(BF16)
    o_ref[...] = jnp.dot(a, w_ref[...], preferred_element_type=F32) + b_ref[...]


def _ada(c_all, w16, b):
    rows = c_all.shape[0]
    return pl.pallas_call(
        _ada_kernel,
        out_shape=jax.ShapeDtypeStruct((N_MOD, rows, D_MODEL), F32),
        grid=(N_MOD,),
        in_specs=[pl.BlockSpec((rows, D_MODEL), lambda j: (0, 0)),
                  pl.BlockSpec((D_MODEL, D_MODEL), lambda j: (0, j)),
                  pl.BlockSpec((1, D_MODEL), lambda j: (0, j))],
        out_specs=pl.BlockSpec((None, rows, D_MODEL), lambda j: (j, 0, 0)),
        compiler_params=_cp(("arbitrary",)),
        name="ada",
    )(c_all, w16, b)


class _Mod:
    def __init__(self, arr, per_token, tiles_per_seq=1):
        self.arr = arr
        self.per_token = per_token
        self.tiles_per_seq = tiles_per_seq

    def spec(self, j, tm, ngrid):
        if self.per_token:
            if ngrid == 1:
                return pl.BlockSpec((None, tm, D_MODEL), lambda i: (j, i, 0))
            return pl.BlockSpec((None, tm, D_MODEL), lambda i, k: (j, i, 0))
        tps = self.tiles_per_seq
        if ngrid == 1:
            return pl.BlockSpec((None, None, 1, D_MODEL), lambda i: (j, i // tps, 0, 0))
        return pl.BlockSpec((None, None, 1, D_MODEL), lambda i, k: (j, i // tps, 0, 0))


def _ffn_kernel(x_ref, sh_ref, sc_ref, gt_ref, g_ref, wg_ref, wu_ref, wd_ref, o_ref, h_sc, acc_sc):
    k = pl.program_id(1)

    @pl.when(k == 0)
    def _():
        h = _rms(x_ref[...], g_ref[...]) * (1.0 + sc_ref[...]) + sh_ref[...]
        h_sc[...] = h.astype(BF16)
        acc_sc[...] = jnp.zeros_like(acc_sc)

    h = h_sc[...]
    a = jnp.dot(h, wg_ref[...], preferred_element_type=F32)
    b = jnp.dot(h, wu_ref[...], preferred_element_type=F32)
    act = (_silu(a) * b).astype(BF16)
    acc_sc[...] += jnp.dot(act, wd_ref[...], preferred_element_type=F32)

    @pl.when(k == pl.num_programs(1) - 1)
    def _():
        o_ref[...] = x_ref[...] + 0.5 * gt_ref[...] * acc_sc[...]


def _ffn(x, mod, j0, g, wg, wu, wd, tm, tf):
    n = x.shape[0]
    nk = D_FF // tf
    return pl.pallas_call(
        _ffn_kernel,
        out_shape=jax.ShapeDtypeStruct((n, D_MODEL), F32),
        grid=(n // tm, nk),
        in_specs=[pl.BlockSpec((tm, D_MODEL), lambda i, k: (i, 0)),
                  mod.spec(j0, tm, 2), mod.spec(j0 + 1, tm, 2), mod.spec(j0 + 2, tm, 2),
                  pl.BlockSpec((1, D_MODEL), lambda i, k: (0, 0)),
                  pl.BlockSpec((D_MODEL, tf), lambda i, k: (0, k)),
                  pl.BlockSpec((D_MODEL, tf), lambda i, k: (0, k)),
                  pl.BlockSpec((tf, D_MODEL), lambda i, k: (k, 0))],
        out_specs=pl.BlockSpec((tm, D_MODEL), lambda i, k: (i, 0)),
        scratch_shapes=[pltpu.VMEM((tm, D_MODEL), BF16), pltpu.VMEM((tm, D_MODEL), F32)],
        compiler_params=_cp(("arbitrary", "arbitrary")),
        name="ffn",
    )(x, mod.arr, mod.arr, mod.arr, g, wg, wu, wd)


_C_QA, _C_KA, _C_VA, _C_QI = 0, 512, 1024, 1536
_C_KI, _C_WI = 2048, 2176
_C_QB, _C_FB, _C_IB, _C_GB = 2304, 2816, 3328, 3840
_N_PACK = 4352


def _mix_kernel(x_ref, sh_ref, sc_ref, g_ref, w_ref, gq_ref, gk_ref, lbl_ref, bd_ref,
                qa_o, ka_o, ka16_o, va_o, va16_o, qi_o, ki_o, ki16_o, wi_o, qb_o, f_o, vb_o, sg_o):
    h = (_rms(x_ref[...], g_ref[...]) * (1.0 + sc_ref[...]) + sh_ref[...]).astype(BF16)

    def proj(c0, width):
        return jnp.dot(h, w_ref[:, c0:c0 + width], preferred_element_type=F32)

    bd = bd_ref[...]
    qa = proj(_C_QA, W_ATTN)
    qa = qa * lax.rsqrt(_group_mean_sq(qa, bd) + RMS_EPS) * gq_ref[...]
    qa_o[...] = (qa * (D_HA ** -0.5)).astype(BF16)
    ka = proj(_C_KA, W_ATTN)
    ka = ka * lax.rsqrt(_group_mean_sq(ka, bd) + RMS_EPS) * gk_ref[...]
    ka_o[...] = ka.T
    ka16_o[...] = ka.astype(BF16)
    va_t = proj(_C_VA, W_ATTN).T
    va_o[...] = va_t
    va16_o[...] = va_t.astype(BF16)
    qi_o[...] = (proj(_C_QI, H_I * D_I) * (D_I ** -0.5)).astype(BF16)
    ki = proj(_C_KI, LANES)
    ki_o[...] = ki
    ki16_o[...] = ki.astype(BF16)
    wi_o[...] = proj(_C_WI, LANES) * (H_I ** -0.5)
    qb_o[...] = _silu(proj(_C_QB, W_HGRN))
    l0 = lbl_ref[0:1, :]
    l1 = lbl_ref[1:2, :]
    mx = jnp.maximum(l0, l1)
    e0 = jnp.exp(l0 - mx)
    e1 = jnp.exp(l1 - mx)
    lb = e0 / (e0 + e1)
    f_o[...] = lb + (1.0 - lb) * _sigmoid(proj(_C_FB, W_HGRN))
    vb_o[...] = proj(_C_IB, W_HGRN)
    sg_o[...] = _silu(proj(_C_GB, W_HGRN))


def _mix(x, mod, g, wpack, gq, gk, lbl, bd, tm, rows_per_group):
    n = x.shape[0]
    tpg = rows_per_group // tm
    row = lambda w: pl.BlockSpec((tm, w), lambda i: (i, 0))
    full = lambda a: pl.BlockSpec(a.shape, lambda i: (0,) * a.ndim)
    featmajor = pl.BlockSpec((None, W_ATTN, tm), lambda i: (i // tpg, 0, i % tpg))
    outs = [(W_ATTN, BF16), None, (W_ATTN, BF16), None, None,
            (H_I * D_I, BF16), (LANES, F32), (LANES, BF16), (LANES, F32),
            (W_HGRN, F32), (W_HGRN, F32), (W_HGRN, F32), (W_HGRN, F32)]
    fm_dtypes = {1: F32, 3: F32, 4: BF16}
    return pl.pallas_call(
        _mix_kernel,
        out_shape=[jax.ShapeDtypeStruct((n // rows_per_group, W_ATTN, rows_per_group), fm_dtypes[k]) if o is None
                   else jax.ShapeDtypeStruct((n, o[0]), o[1]) for k, o in enumerate(outs)],
        grid=(n // tm,),
        in_specs=[row(D_MODEL), mod.spec(3, tm, 1), mod.spec(4, tm, 1), full(g), full(wpack),
                  full(gq), full(gk), full(lbl), full(bd)],
        out_specs=[featmajor if o is None else row(o[0]) for o in outs],
        compiler_params=_cp(("arbitrary",)),
        name="mix",
    )(x, mod.arr, mod.arr, g, wpack, gq, gk, lbl, bd)


def _t5_bucket(dist):
    n = jnp.maximum(dist, 0)
    max_exact = T5_BUCKETS // 2
    nf = jnp.maximum(n, 1).astype(F32)
    large = max_exact + (jnp.log(nf / max_exact) / math.log(T5_MAX_DIST / max_exact)
                         * (T5_BUCKETS - max_exact)).astype(I32)
    large = jnp.minimum(large, T5_BUCKETS - 1)
    return jnp.where(n < max_exact, n, large)


def _bias_kernel(rb_ref, bp_ref, bs_ref):
    def lookup(bucket, h):
        val = jnp.zeros(bucket.shape, F32)
        for b in range(T5_BUCKETS):
            val = jnp.where(bucket == b, rb_ref[b, h], val)
        return val

    k = lax.broadcasted_iota(I32, (2 * LANES, LANES), 0)
    q = lax.broadcasted_iota(I32, (2 * LANES, LANES), 1)
    far_bucket = _t5_bucket(jnp.full((2 * LANES, LANES), 2 * LANES, I32))
    for v, dist in enumerate((q - k, LANES + q - k)):
        bucket = _t5_bucket(dist)
        for h in range(H_A):
            bp_ref[v, h // 2, :, (h % 2) * LANES:(h % 2 + 1) * LANES] = lookup(bucket, h) - lookup(far_bucket, h)

    t = lax.broadcasted_iota(I32, (SUBLANES, LANES), 0)
    u = lax.broadcasted_iota(I32, (SUBLANES, LANES), 1)
    sdists = (PAGE + t - u, t - u, jnp.full((SUBLANES, LANES), 2 * LANES, I32))
    for m in range(3):
        bucket = _t5_bucket(sdists[m])
        for h in range(H_A):
            bs_ref[m, h * SUBLANES:(h + 1) * SUBLANES, :] = lookup(bucket, h)


def _bias_tiles(rel_bias):
    return pl.pallas_call(
        _bias_kernel,
        out_shape=[jax.ShapeDtypeStruct((2, H_A // 2, 2 * LANES, 2 * LANES), F32),
                   jax.ShapeDtypeStruct((3, H_A * SUBLANES, LANES), F32)],
        in_specs=[pl.BlockSpec(memory_space=pltpu.SMEM)],
        out_specs=[pl.BlockSpec(memory_space=pltpu.VMEM), pl.BlockSpec(memory_space=pltpu.VMEM)],
        name="t5_bias",
    )(rel_bias)


def _sortable_key(score):
    bits = pltpu.bitcast(score, I32)
    return jnp.where(bits < 0, INT_MIN - bits, bits)


def _tree_reduce(x, op):
    while x.shape[0] > 1:
        half = x.shape[0] // 2
        y = op(x[:half], x[half:2 * half])
        x = y if x.shape[0] % 2 == 0 else jnp.concatenate([y, x[2 * half:]], axis=0)
    return x[0]


def _sublane_allreduce(x, op):
    for sh in (4, 2, 1):
        x = op(x, pltpu.roll(x, sh, 0))
    return x


def _topk_maskbias_t(keys_ref, maskb_ref, far_ref, nch, cw, n_sel, idx_bits, far_limit):
    ng = cw // SUBLANES
    sub = lax.broadcasted_iota(I32, (ng, SUBLANES, LANES), 0) * SUBLANES + lax.broadcasted_iota(
        I32, (ng, SUBLANES, LANES), 1)

    def count(indicator):
        def body(c, acc):
            k0 = pl.multiple_of(c * cw, cw)
            kk = keys_ref[pl.ds(k0, cw), :].reshape(ng, SUBLANES, LANES)
            return acc + _tree_reduce(indicator(kk, k0), jnp.add)
        acc = lax.fori_loop(0, nch, body, jnp.zeros((SUBLANES, LANES), F32))
        return _sublane_allreduce(acc, jnp.add)

    def bisect(it, ans):
        cand = ans | jnp.left_shift(jnp.int32(1), 31 - it)
        cs = cand ^ INT_MIN
        cnt = count(lambda kk, k0: jnp.where(kk >= cs[None], 1.0, 0.0))
        return jnp.where(cnt >= n_sel, cand, ans)

    thr = lax.fori_loop(0, 32, bisect, jnp.zeros((SUBLANES, LANES), I32)) ^ INT_MIN
    need = n_sel - count(lambda kk, k0: jnp.where(kk > thr[None], 1.0, 0.0))
    n_eq = count(lambda kk, k0: jnp.where(kk == thr[None], 1.0, 0.0))
    has_tie = jnp.max(jnp.where(thr == INT_MIN, 0.0, jnp.where(n_eq > need, 1.0, 0.0))) > 0.0

    def tie_search(_):
        def step(it, aj):
            cand = aj | jnp.left_shift(jnp.int32(1), idx_bits - 1 - it)
            cnt = count(lambda kk, k0: jnp.where(kk == thr[None], jnp.where((k0 + sub) < cand[None], 1.0, 0.0), 0.0))
            return jnp.where(cnt < need, cand, aj)
        return lax.fori_loop(0, idx_bits, step, jnp.zeros((SUBLANES, LANES), I32))

    cut = lax.cond(has_tie, tie_search, lambda _: jnp.full((SUBLANES, LANES), 2 ** 30, I32), 0)
    cut = jnp.where(thr == INT_MIN, -1, cut)

    def fin(c, _):
        k0 = pl.multiple_of(c * cw, cw)
        kk = keys_ref[pl.ds(k0, cw), :].reshape(ng, SUBLANES, LANES)
        kpos = k0 + sub
        tie = jnp.where(kpos <= cut[None], 0.0, NEG)
        mb = jnp.where(kk > thr[None], 0.0, jnp.where(kk == thr[None], tie, NEG))
        maskb_ref[pl.ds(k0, cw), :] = mb.reshape(cw, LANES)
        if far_ref is not None:
            far_ref[pl.ds(k0, cw), :] = jnp.where(kpos < far_limit, mb, NEG).reshape(cw, LANES)
        return 0

    lax.fori_loop(0, nch, fin, 0)


def _softmax_step(carry, s, v16, v_is_transposed=False):
    m, l, acc = carry
    mn = jnp.maximum(m, jnp.max(s, axis=1, keepdims=True))
    alpha = jnp.exp(m - mn)
    p = jnp.exp(s - mn)
    l = alpha * l + jnp.sum(p, axis=1, keepdims=True)
    if v_is_transposed:
        pv = lax.dot_general(p.astype(BF16), v16, _NT, preferred_element_type=F32)
    else:
        pv = jnp.dot(p.astype(BF16), v16, preferred_element_type=F32)
    return mn, l, alpha * acc + pv


_QB = 128
_KC = 512
_FW = 512


def _prompt_attn_step(step, nsteps, i, qa_ref, qi_ref, wi_ref, ki_ref, ka_ref, vat_ref, bias_ref, o_ref,
                      keys_sc, maskb_sc, far_sc, qs_sc, q2_sc, m_sc, l_sc, acc_sc):
    lane = lax.broadcasted_iota(I32, (_QB, LANES), 1)
    lo = lane < D_I
    nch = i // (_KC // _QB) + 1
    near0 = jnp.maximum(i - 1, 0) * _QB
    npair = H_A // 2
    cols2 = 2 * _QB

    @pl.when(step == 0)
    def _():
        for h in range(H_I):
            pair = qi_ref[:, (h // 2) * LANES:(h // 2 + 1) * LANES].astype(F32)
            keep = jnp.where(lo, pair, 0.0) if h % 2 == 0 else jnp.where(lo, 0.0, pair)
            qs_sc[h * _QB:(h + 1) * _QB, :] = keep.astype(BF16)

        qpos = i * _QB + lax.broadcasted_iota(I32, (_KC, LANES), 1)

        def scores(c, _):
            k0 = pl.multiple_of(c * _KC, _KC)
            s = lax.dot_general(ki_ref[pl.ds(k0, _KC), :], qs_sc[...], _NT, preferred_element_type=F32)
            sc = jnp.zeros((_KC, LANES), F32)
            for h in range(H_I):
                sc = sc + jnp.maximum(s[:, h * _QB:(h + 1) * _QB], 0.0) * wi_ref[h:h + 1, :]
            kpos = k0 + lax.broadcasted_iota(I32, (_KC, LANES), 0)
            keys_sc[pl.ds(k0, _KC), :] = jnp.where(kpos <= qpos, _sortable_key(sc), INT_MIN)
            return 0

        lax.fori_loop(0, nch, scores, 0)
        _topk_maskbias_t(keys_sc, maskb_sc, far_sc, nch, _KC, TOPK_MAX, 12, near0)

        for p in range(npair):
            qp = qa_ref[:, p * LANES:(p + 1) * LANES].astype(F32)
            q2_sc[p * cols2:(p + 1) * cols2, :] = jnp.concatenate(
                [jnp.where(lo, qp, 0.0), jnp.where(lo, 0.0, qp)], axis=0).astype(BF16)
        m_sc[...] = jnp.full(m_sc.shape, NEG, F32)
        l_sc[...] = jnp.zeros_like(l_sc)
        acc_sc[...] = jnp.zeros_like(acc_sc)

    def update_all(k0, width, mask_ref, biases):
        mb = mask_ref[pl.ds(k0, width), :]
        mb2 = jnp.concatenate([mb, mb], axis=1)
        logits = []
        for p in range(npair):
            s = lax.dot_general(ka_ref[pl.ds(k0, width), p * LANES:(p + 1) * LANES],
                                q2_sc[p * cols2:(p + 1) * cols2, :], _NT, preferred_element_type=F32) + mb2
            if biases is not None:
                s = s + biases[p]
            logits.append(s.reshape(width // SUBLANES, SUBLANES, cols2))
        probs = []
        for p in range(npair):
            m_old = m_sc[p]
            m_new = jnp.maximum(m_old, _sublane_allreduce(_tree_reduce(logits[p], jnp.maximum), jnp.maximum))
            alpha = jnp.exp(m_old - m_new)
            pexp = jnp.exp(logits[p] - m_new[None])
            m_sc[p] = m_new
            l_sc[p] = alpha * l_sc[p] + _tree_reduce(pexp, jnp.add)
            probs.append((pexp.reshape(width, cols2).astype(BF16), alpha))
        for p in range(npair):
            p16, alpha = probs[p]
            pv = jnp.dot(vat_ref[p * LANES:(p + 1) * LANES, pl.ds(k0, width)], p16,
                         preferred_element_type=F32)
            acc = acc_sc[p].reshape(LANES // SUBLANES, SUBLANES, cols2) * alpha[None]
            acc_sc[p] = acc.reshape(LANES, cols2) + pv

    def far(j, _):
        update_all(pl.multiple_of((step + j * nsteps) * _FW, _FW), _FW, far_sc, None)
        return 0

    nfar = (near0 + _FW - 1) // _FW
    lax.fori_loop(0, (nfar - step + nsteps - 1) // nsteps, far, 0)

    @pl.when(step == nsteps - 1)
    def _():
        first = jnp.minimum(i, 1)
        update_all(pl.multiple_of(near0, _QB), 2 * _QB, maskb_sc, [bias_ref[first, p] for p in range(npair)])
        for p in range(npair):
            l = _sublane_allreduce(l_sc[p], jnp.add)
            o = (acc_sc[p].reshape(LANES // SUBLANES, SUBLANES, cols2) / l[None]).reshape(LANES, cols2)
            ot = jnp.concatenate([o[:D_HA, :_QB], o[D_HA:, _QB:]], axis=0)
            o_ref[:, p * LANES:(p + 1) * LANES] = ot.T


_PPS = 16
_TP = SUBLANES
_TK = 2 * SUBLANES


class _PageStream:
    def __init__(self, pt_ref, hbm_refs, bufs, sems, pages_per_group):
        self.pt_ref, self.hbm_refs, self.bufs, self.sems, self.ppg = pt_ref, hbm_refs, bufs, sems, pages_per_group

    def _copies(self, seq, group, slot):
        for hbm, buf, sem in zip(self.hbm_refs, self.bufs, self.sems):
            for p in range(self.ppg):
                yield pltpu.make_async_copy(hbm.at[self.pt_ref[seq, group * self.ppg + p]], buf.at[slot, p],
                                            sem.at[slot])

    def start(self, seq, group, slot):
        for cp in self._copies(seq, group, slot):
            cp.start()

    def wait(self, seq, group, slot):
        for cp in self._copies(seq, group, slot):
            cp.wait()

    def advance(self, step, nsteps, groups_per_seq):
        slot = lax.rem(step, 2)

        @pl.when(step == 0)
        def _():
            self.start(0, 0, 0)

        @pl.when(step + 1 < nsteps)
        def _():
            nxt = step + 1
            self.start(nxt // groups_per_seq, lax.rem(nxt, groups_per_seq), 1 - slot)

        self.wait(step // groups_per_seq, lax.rem(step, groups_per_seq), slot)
        return slot


def _sample_scores_kernel(pt_ref, q_ref, w_ref, kn_ref, cki_ref, o_ref, buf, sem):
    n_pages = buf.shape[1]
    slot = _PageStream(pt_ref, [cki_ref], [buf], [sem], n_pages).advance(pl.program_id(0), pl.num_programs(0), 1)
    q = q_ref[...]

    def head_sum(s):
        n = s.shape[1]
        out = jnp.zeros((_TP, n), F32)
        for h in range(H_I):
            wh = w_ref[h * _TP:(h + 1) * _TP, :]
            out = out + jnp.maximum(s[h * _TP:(h + 1) * _TP, :], 0.0) * jnp.concatenate([wh] * (n // LANES), axis=1)
        return out

    for c in range(n_pages // _PPS):
        kc = jnp.concatenate([buf[slot, c * _PPS + p].astype(BF16) for p in range(_PPS)], axis=1)
        s = jnp.dot(q, kc, preferred_element_type=F32)
        o_ref[:, c * _PPS * PAGE:(c + 1) * _PPS * PAGE] = head_sum(s)

    kn = jnp.concatenate([kn_ref[...], jnp.zeros((LANES - _TK, D_I), BF16)], axis=0)
    sn = lax.dot_general(q, kn, _NT, preferred_element_type=F32)
    o_ref[:, o_ref.shape[1] - LANES:] = head_sum(sn)


def _sample_scores(page_table, q_st, w_b, ki_new, cache_kidx, n_pages):
    nseq = q_st.shape[0]
    lpad = n_pages * PAGE + LANES
    return pl.pallas_call(
        _sample_scores_kernel,
        out_shape=jax.ShapeDtypeStruct((nseq, _TP, lpad), F32),
        grid_spec=pltpu.PrefetchScalarGridSpec(
            num_scalar_prefetch=1, grid=(nseq,),
            in_specs=[pl.BlockSpec((None, H_I * _TP, D_I), lambda b, pt: (b, 0, 0)),
                      pl.BlockSpec((None, H_I * _TP, LANES), lambda b, pt: (b, 0, 0)),
                      pl.BlockSpec((None, _TK, D_I), lambda b, pt: (b, 0, 0)),
                      pl.BlockSpec(memory_space=pl.ANY)],
            out_specs=pl.BlockSpec((None, _TP, lpad), lambda b, pt: (b, 0, 0)),
            scratch_shapes=[pltpu.VMEM((2, n_pages, D_I, PAGE), F32), pltpu.SemaphoreType.DMA((2,))]),
        compiler_params=_cp(("arbitrary",)),
        name="sample_scores",
    )(page_table, q_st, w_b, ki_new, cache_kidx)


_SEL_CW = 640


def _sample_select_kernel(s_ref, o_ref, keys_sc, *, past, t_dec):
    lpad = s_ref.shape[0]
    t = lax.broadcasted_iota(I32, (_SEL_CW, LANES), 1) % _TP
    limit = jnp.where(t < t_dec, past + t, -1)
    for c in range(lpad // _SEL_CW):
        sl = slice(c * _SEL_CW, (c + 1) * _SEL_CW)
        kpos = c * _SEL_CW + lax.broadcasted_iota(I32, (_SEL_CW, LANES), 0)
        keys_sc[sl, :] = jnp.where(kpos <= limit, _sortable_key(s_ref[sl, :]), INT_MIN)
    _topk_maskbias_t(keys_sc, o_ref, None, lpad // _SEL_CW, _SEL_CW, TOPK_MAX, 14, None)


def _sample_select(scores_t, past, t_dec):
    lpad, cols = scores_t.shape
    return pl.pallas_call(
        functools.partial(_sample_select_kernel, past=past, t_dec=t_dec),
        out_shape=jax.ShapeDtypeStruct((lpad, cols), F32),
        grid=(cols // LANES,),
        in_specs=[pl.BlockSpec((lpad, LANES), lambda g: (0, g))],
        out_specs=pl.BlockSpec((lpad, LANES), lambda g: (0, g)),
        scratch_shapes=[pltpu.VMEM((lpad, LANES), I32)],
        compiler_params=_cp(("arbitrary",)),
        name="sample_select",
    )(scores_t)


def _sample_attn_step(s_id, last, q_ref, mb_ref, kn_ref, vn_ref, bias_ref, kp, vp, o_ref,
                      qblk_sc, m_sc, l_sc, acc_sc):
    rows = H_A * _TP
    span = _PPS * PAGE

    @pl.when(s_id == 0)
    def _():
        q = jnp.concatenate([q_ref[...]] * H_A, axis=0)
        r = lax.broadcasted_iota(I32, (rows, W_ATTN), 0) // _TP
        c = lax.broadcasted_iota(I32, (rows, W_ATTN), 1) // D_HA
        qblk_sc[...] = jnp.where(r == c, q, 0.0).astype(BF16)
        m_sc[...] = jnp.full(m_sc.shape, NEG, F32)
        l_sc[...] = jnp.zeros_like(l_sc)
        acc_sc[...] = jnp.zeros_like(acc_sc)

    def update(s, v16, v_is_transposed):
        carry = (m_sc[:, 0:1], l_sc[:, 0:1], acc_sc[...])
        m, l, acc = _softmax_step(carry, s, v16, v_is_transposed)
        m_sc[...] = jnp.broadcast_to(m, m_sc.shape)
        l_sc[...] = jnp.broadcast_to(l, l_sc.shape)
        acc_sc[...] = acc

    k16 = jnp.concatenate([pg[...].astype(BF16) for pg in kp], axis=1)
    v16 = jnp.concatenate([pg[...].astype(BF16) for pg in vp], axis=1)
    s = jnp.dot(qblk_sc[...], k16, preferred_element_type=F32)
    mb = mb_ref[:, pl.ds(pl.multiple_of(s_id * span, span), span)]
    last_tile = jnp.where(s_id == last, bias_ref[0], bias_ref[2])
    bias = jnp.concatenate([bias_ref[2]] * (_PPS - 1) + [last_tile], axis=1)
    update(s + jnp.concatenate([mb] * H_A, axis=0) + bias, v16, True)

    @pl.when(s_id == last)
    def _():
        pad = jnp.zeros((LANES - _TK, W_ATTN), BF16)
        kn = jnp.concatenate([kn_ref[...], pad], axis=0)
        vn = jnp.concatenate([vn_ref[...], pad], axis=0)
        sn = lax.dot_general(qblk_sc[...], kn, _NT, preferred_element_type=F32)
        mbn = mb_ref[:, mb_ref.shape[1] - LANES:]
        update(sn + jnp.concatenate([mbn] * H_A, axis=0) + bias_ref[1], vn, False)
        o = acc_sc[...] / l_sc[:, 0:1]
        c = lax.broadcasted_iota(I32, (_TP, W_ATTN), 1) // D_HA
        out = jnp.zeros((_TP, W_ATTN), F32)
        for h in range(H_A):
            out = jnp.where(c == h, o[h * _TP:(h + 1) * _TP, :], out)
        o_ref[...] = out


def _attn_kernel(pt_ref, qa_ref, qi_ref, wi_ref, ki_ref, ka_ref, vat_ref, bias_p_ref,
                 sq_ref, smb_ref, skn_ref, svn_ref, bias_s_ref, ck_ref, cv_ref, o_ref, so_ref, *scratch,
                 nq, nsteps, nseq):
    prompt_scratch, sample_scratch = scratch[:8], scratch[8:12]
    kbuf, vbuf, ksem, vsem = scratch[12:]
    step = pl.program_id(2)
    seq_id = pl.program_id(0) * nq + pl.program_id(1)
    stream = _PageStream(pt_ref, [ck_ref, cv_ref], [kbuf, vbuf], [ksem, vsem], _PPS)
    slot = stream.advance(seq_id * nsteps + step, nseq * nsteps, nsteps)
    _prompt_attn_step(step, nsteps, pl.program_id(1), qa_ref, qi_ref, wi_ref, ki_ref, ka_ref, vat_ref, bias_p_ref,
                      o_ref, *prompt_scratch)
    _sample_attn_step(step, nsteps - 1, sq_ref, smb_ref, skn_ref, svn_ref, bias_s_ref,
                      [kbuf.at[slot, p] for p in range(_PPS)], [vbuf.at[slot, p] for p in range(_PPS)], so_ref,
                      *sample_scratch)


def _attention(qa16, qi16, wi_t, ki16, ka16, va16_t, bias_p, nbatch, seq,
               page_table, q8, maskb, k_new, v_new, bias_s, cache_k, cache_v, n_pages):
    nq = seq // _QB
    nseq = q8.shape[0]
    steps = n_pages // _PPS
    lpad = maskb.shape[1]
    npair = H_A // 2
    assert nseq == nbatch * nq, "one decode sequence per prompt query block"
    blk = lambda w: pl.BlockSpec((_QB, w), lambda b, i, s, pt: (b * nq + i, 0))
    once = pl.Buffered(1)
    per_b = lambda w: pl.BlockSpec((seq, w), lambda b, i, s, pt: (b, 0), pipeline_mode=once)
    seq3 = lambda t: pl.BlockSpec((None, t, W_ATTN), lambda b, i, s, pt: (b * nq + i, 0, 0))
    hbm = pl.BlockSpec(memory_space=pl.ANY)
    return pl.pallas_call(
        functools.partial(_attn_kernel, nq=nq, nsteps=steps, nseq=nseq),
        out_shape=[jax.ShapeDtypeStruct((nbatch * seq, W_ATTN), F32),
                   jax.ShapeDtypeStruct((nseq, _TP, W_ATTN), F32)],
        grid_spec=pltpu.PrefetchScalarGridSpec(
            num_scalar_prefetch=1, grid=(nbatch, nq, steps),
            in_specs=[blk(W_ATTN), blk(H_I * D_I),
                      pl.BlockSpec((H_I, _QB), lambda b, i, s, pt: (0, b * nq + i)),
                      per_b(LANES), per_b(W_ATTN),
                      pl.BlockSpec((None, W_ATTN, seq), lambda b, i, s, pt: (b, 0, 0), pipeline_mode=once),
                      pl.BlockSpec(bias_p.shape, lambda b, i, s, pt: (0, 0, 0, 0), pipeline_mode=once),
                      seq3(_TP), pl.BlockSpec((_TP, lpad), lambda b, i, s, pt: (b * nq + i, 0)),
                      seq3(_TK), seq3(_TK),
                      pl.BlockSpec(bias_s.shape, lambda b, i, s, pt: (0, 0, 0), pipeline_mode=once),
                      hbm, hbm],
            out_specs=[blk(W_ATTN), seq3(_TP)],
            scratch_shapes=[pltpu.VMEM((seq, LANES), I32), pltpu.VMEM((seq, LANES), F32),
                            pltpu.VMEM((seq, LANES), F32),
                            pltpu.VMEM((H_I * _QB, LANES), BF16), pltpu.VMEM((H_A * _QB, LANES), BF16),
                            pltpu.VMEM((npair, SUBLANES, 2 * _QB), F32), pltpu.VMEM((npair, SUBLANES, 2 * _QB), F32),
                            pltpu.VMEM((npair, LANES, 2 * _QB), F32),
                            pltpu.VMEM((H_A * _TP, W_ATTN), BF16), pltpu.VMEM((H_A * _TP, LANES), F32),
                            pltpu.VMEM((H_A * _TP, LANES), F32), pltpu.VMEM((H_A * _TP, W_ATTN), F32),
                            pltpu.VMEM((2, _PPS, W_ATTN, PAGE), F32), pltpu.VMEM((2, _PPS, W_ATTN, PAGE), F32),
                            pltpu.SemaphoreType.DMA((2,)), pltpu.SemaphoreType.DMA((2,))]),
        compiler_params=_cp(("arbitrary", "arbitrary", "arbitrary")),
        name="attention",
    )(page_table, qa16, qi16, wi_t, ki16, ka16, va16_t, bias_p, q8, maskb, k_new, v_new, bias_s, cache_k, cache_v)


def _hgrn_chunk(q, k, v, logf, st, c):
    row = lax.broadcasted_iota(I32, (c, DK_B), 0)
    g = logf
    sh = 1
    while sh < c:
        g = g + jnp.where(row >= sh, pltpu.roll(g, sh, 0), 0.0)
        sh *= 2

    o = lax.dot_general((q * jnp.exp(g)).astype(BF16), st.astype(BF16), _NT, preferred_element_type=F32)

    if c > SUBLANES:
        rr = lax.broadcasted_iota(I32, (c, c), 0)
        cc = lax.broadcasted_iota(I32, (c, c), 1)
        a = jnp.zeros((c, c), F32)
        w = SUBLANES
        while w < c:
            nb = c // (2 * w)
            gb = jnp.broadcast_to(g.reshape(nb, 2 * w, DK_B)[:, w - 1:w, :], (nb, 2 * w, DK_B)).reshape(c, DK_B)
            right = (row % (2 * w)) >= w
            qt = jnp.where(right, q * jnp.exp(jnp.where(right, g - gb, 0.0)), 0.0)
            kt = jnp.where(right, 0.0, k * jnp.exp(jnp.where(right, 0.0, gb - g)))
            aw = lax.dot_general(qt.astype(BF16), kt.astype(BF16), _NT, preferred_element_type=F32)
            a = a + jnp.where((rr // (2 * w)) == (cc // (2 * w)), aw, 0.0)
            w *= 2
        o = o + jnp.dot(a.astype(BF16), v.astype(BF16), preferred_element_type=F32)

    o = o + jnp.sum(q * k, axis=1, keepdims=True) * v
    for d in range(1, SUBLANES):
        ok = (row % SUBLANES) >= d
        e = jnp.exp(jnp.where(ok, g - pltpu.roll(g, d, 0), 0.0))
        coef = jnp.sum(jnp.where(ok, q * pltpu.roll(k, d, 0) * e, 0.0), axis=1, keepdims=True)
        o = o + coef * pltpu.roll(v, d, 0)

    g_end = g[c - 1:c, :]
    kd = k * jnp.exp(g_end - g)
    st = st * jnp.exp(g_end) + lax.dot_general(v.astype(BF16), kd.astype(BF16), _TN, preferred_element_type=F32)
    return o, st


_HG = 2


def _hgrn_kernel(q_ref, f_ref, v_ref, s0_ref, o_ref, s_ref, *, nb, t_len, c):
    for j in range(nb):
        def body(ci, states):
            r = pl.ds(pl.multiple_of(ci * c, c), c)
            new_states = []
            for g in range(_HG):
                cs = slice(g * DK_B, (g + 1) * DK_B)
                f = f_ref[j, r, cs]
                o, st = _hgrn_chunk(q_ref[j, r, cs], 1.0 - f, v_ref[j, r, cs], jnp.log(f), states[g], c)
                o_ref[j, r, cs] = o
                new_states.append(st)
            return tuple(new_states)
        states = lax.fori_loop(0, t_len // c, body, tuple(s0_ref[j, g].T for g in range(_HG)))
        for g in range(_HG):
            s_ref[j, g] = states[g].T


def _hgrn(q, f, v, s0, nb, c):
    nseq, t_len, _ = q.shape
    tok = pl.BlockSpec((nb, t_len, _HG * DK_B), lambda b, h: (b, 0, h))
    st = pl.BlockSpec((nb, _HG, DK_B, DV_B), lambda b, h: (b, h, 0, 0))
    return pl.pallas_call(
        functools.partial(_hgrn_kernel, nb=nb, t_len=t_len, c=c),
        out_shape=[jax.ShapeDtypeStruct((nseq, t_len, W_HGRN), F32),
                   jax.ShapeDtypeStruct((nseq, H_B, DK_B, DV_B), F32)],
        grid=(nseq // nb, H_B // _HG),
        in_specs=[tok, tok, tok, st],
        out_specs=[tok, st],
        compiler_params=_cp(("arbitrary", "arbitrary")),
        name="hgrn",
    )(q, f, v, s0)


def _merge_kernel(oa_ref, oh_ref, sg_ref, x_ref, gt_ref, beta_ref, gh_ref, bd_ref, w_ref, o_ref):
    oa = oa_ref[...]
    a = oa * lax.rsqrt(_group_mean_sq(oa, bd_ref[...]) + RMS_EPS) * beta_ref[...]
    oh = oh_ref[...]
    parts = []
    for h in range(H_B):
        sl = slice(h * DV_B, (h + 1) * DV_B)
        parts.append(_rms(oh[:, sl], gh_ref[:, sl]))
    r = jnp.concatenate(parts, axis=1) * sg_ref[...]
    y = (jnp.dot(a.astype(BF16), w_ref[:W_ATTN, :], preferred_element_type=F32)
         + jnp.dot(r.astype(BF16), w_ref[W_ATTN:, :], preferred_element_type=F32))
    o_ref[...] = x_ref[...] + gt_ref[...] * y


def _merge(oa, oh, sg, x, mod, beta, gh, bd, w16, tm):
    n = x.shape[0]
    row = lambda w: pl.BlockSpec((tm, w), lambda i: (i, 0))
    full = lambda a: pl.BlockSpec(a.shape, lambda i: (0,) * a.ndim)
    return pl.pallas_call(
        _merge_kernel,
        out_shape=jax.ShapeDtypeStruct((n, D_MODEL), F32),
        grid=(n // tm,),
        in_specs=[row(W_ATTN), row(W_HGRN), row(W_HGRN), row(D_MODEL), mod.spec(5, tm, 1),
                  full(beta), full(gh), full(bd), full(w16)],
        out_specs=row(D_MODEL),
        compiler_params=_cp(("arbitrary",)),
        name="merge",
    )(oa, oh, sg, x, mod.arr, beta, gh, bd, w16)


def _pad_tokens(a, value=0.0, to=_TP):
    return jnp.pad(a, ((0, 0), (0, to - a.shape[1]), (0, 0)), constant_values=value)


def kernel(x_prompt, x_sample, cache_k, cache_v, cache_kidx, state_hgrn, page_table, c_prompt, c_sample,
           w_ada, b_ada, g_ffn1, ffn1_w_gate, ffn1_w_up, ffn1_w_down, g_mix, w_in, g_q, g_k,
           beta_attn, g_hgrn, w_out, g_ffn2, ffn2_w_gate, ffn2_w_up, ffn2_w_down, rel_bias, lb_logits):
    assert w_ada.shape[0] == 1, "single-layer problem"
    nb_p, seq, _ = x_prompt.shape
    nb_s, t_dec, _ = x_sample.shape
    n_phys = cache_k.shape[1]
    n_pages = page_table.shape[1]
    past = n_pages * PAGE
    assert t_dec <= _TP and n_pages % _PPS == 0 and (past + LANES) % _SEL_CW == 0

    row2 = lambda a: a.reshape(1, -1)
    w_in0 = w_in[0]
    pad_w = jnp.zeros((D_MODEL, LANES - H_I), F32)
    wpack = jnp.concatenate(
        [w_in0[:, :2048], w_in0[:, 2048:2112], w_in0[:, 2048:2112], w_in0[:, 2112:2120], pad_w,
         w_in0[:, 2120:]], axis=1).astype(BF16)
    assert wpack.shape[1] == _N_PACK
    gq = row2(jnp.tile(g_q[0], H_A))
    gk = row2(jnp.tile(g_k[0], H_A))
    gidx = jnp.arange(W_ATTN) // D_HA
    bd = jnp.where(gidx[:, None] == gidx[None, :], 1.0 / D_HA, 0.0).astype(BF16)
    ffn1 = (row2(g_ffn1[0]), ffn1_w_gate[0].astype(BF16), ffn1_w_up[0].astype(BF16), ffn1_w_down[0].astype(BF16))
    ffn2 = (row2(g_ffn2[0]), ffn2_w_gate[0].astype(BF16), ffn2_w_up[0].astype(BF16), ffn2_w_down[0].astype(BF16))
    w_out16 = w_out[0].astype(BF16)

    n_c = nb_p + nb_s
    c_pad = (-n_c) % SUBLANES
    c_all = jnp.concatenate([c_prompt, c_sample, jnp.zeros((c_pad, D_MODEL), F32)], axis=0)
    mods = _ada(c_all, w_ada[0].astype(BF16), row2(b_ada[0]))
    bias_p, bias_s = _bias_tiles(rel_bias)

    def front(x, mod, tm, tf, rows_per_group):
        x1 = _ffn(x, mod, 0, *ffn1, tm, tf)
        return x1, _mix(x1, mod, row2(g_mix[0]), wpack, gq, gk, lb_logits, bd, tm, rows_per_group)

    def back(x1, mod, tm, tf, oa, oh, sg):
        x2 = _merge(oa, oh, sg, x1, mod, row2(beta_attn[0]), row2(g_hgrn[0]), bd, w_out16, tm)
        return _ffn(x2, mod, 6, *ffn2, tm, tf)

    tm_p, tf = 512, D_FF // 2
    n_tok = nb_s * t_dec
    mod_p = _Mod(mods[:, :nb_p].reshape(N_MOD, nb_p, 1, D_MODEL), False, seq // tm_p)
    mod_s = _Mod(jnp.repeat(mods[:, nb_p:nb_p + nb_s], t_dec, axis=1), True)
    x1p, proj_p = front(x_prompt.reshape(nb_p * seq, D_MODEL), mod_p, tm_p, tf, seq)
    x1s, proj_s = front(x_sample.reshape(n_tok, D_MODEL), mod_s, n_tok, tf, n_tok)
    qa16, kp_t, ka16, vp_t, va16_t, qi16, ki_p, ki16, wi, qb, f, vb, sg_p = proj_p
    qa16s, ks_t, ka16s, vs_t, va16s_t, qi16s, ki_s, ki16s, wis, qbs, fs, vbs, sg_s = proj_s
    va16s = va16s_t[0].T

    ck = jnp.transpose(cache_k[0], (0, 2, 3, 1)).reshape(n_phys, W_ATTN, PAGE)
    cv = jnp.transpose(cache_v[0], (0, 2, 3, 1)).reshape(n_phys, W_ATTN, PAGE)
    cki = jnp.transpose(cache_kidx[0], (0, 2, 1))
    seq3 = lambda a: a.reshape(nb_s, t_dec, a.shape[-1])
    q_st = _pad_tokens(seq3(qi16s)).reshape(nb_s, _TP, H_I, D_I).transpose(0, 2, 1, 3).reshape(nb_s, H_I * _TP, D_I)
    w_st = _pad_tokens(seq3(wis[:, :H_I])).transpose(0, 2, 1).reshape(nb_s, H_I * _TP, 1)
    w_b = jnp.broadcast_to(w_st, (nb_s, H_I * _TP, LANES))
    scores = _sample_scores(page_table, q_st, w_b, _pad_tokens(seq3(ki16s[:, :D_I]), to=_TK), cki, n_pages)
    maskb = _sample_select(scores.reshape(nb_s * _TP, -1).T, past, t_dec).T

    oa_p, oa8 = _attention(qa16, qi16, wi[:, :H_I].T, ki16, ka16, va16_t, bias_p, nb_p, seq,
                           page_table, _pad_tokens(seq3(qa16s.astype(F32))), maskb,
                           _pad_tokens(seq3(ka16s), to=_TK), _pad_tokens(seq3(va16s), to=_TK), bias_s, ck, cv, n_pages)
    r3 = lambda a: a.reshape(nb_p, seq, W_HGRN)
    oh_p, sp = _hgrn(r3(qb), r3(f), r3(vb), jnp.zeros((nb_p, H_B, DK_B, DV_B), F32), 1, 64)
    oh8, ss = _hgrn(_pad_tokens(seq3(qbs)), _pad_tokens(seq3(fs), 1.0), _pad_tokens(seq3(vbs)), state_hgrn[0], 8, _TP)

    yp = back(x1p, mod_p, tm_p, tf, oa_p, oh_p.reshape(nb_p * seq, W_HGRN), sg_p)
    ys = back(x1s, mod_s, n_tok, tf, oa8[:, :t_dec].reshape(n_tok, W_ATTN),
              oh8[:, :t_dec].reshape(n_tok, W_HGRN), sg_s)

    def rows_out(a_t, nseq, t_len):
        a = a_t.reshape(-1, H_A, D_HA, a_t.shape[-1]).transpose(0, 3, 1, 2)
        return a.reshape(1, nseq, t_len, H_A, D_HA)

    return (yp.reshape(nb_p, seq, D_MODEL).astype(x_prompt.dtype),
            ys.reshape(nb_s, t_dec, D_MODEL).astype(x_sample.dtype),
            rows_out(kp_t, nb_p, seq), rows_out(vp_t, nb_p, seq),
            ki_p[:, :D_I].reshape(1, nb_p, seq, D_I), sp[None],
            rows_out(ks_t, nb_s, t_dec), rows_out(vs_t, nb_s, t_dec),
            ki_s[:, :D_I].reshape(1, nb_s, t_dec, D_I), ss[None])
```

```python
import functools
import math

import jax
import jax.numpy as jnp
from jax import lax
from jax.experimental import pallas as pl
from jax.experimental.pallas import tpu as pltpu

F32 = jnp.float32
BF16 = jnp.bfloat16
I32 = jnp.int32

D_MODEL = 1024
W_ATTN = 512
W_HGRN = 512
D_HA = 64
H_A = 8
H_I = 8
D_I = 64
TOPK_MAX = 256
H_B = 4
DK_B = 128
DV_B = 128
D_FF = 2816
T5_BUCKETS = 32
T5_MAX_DIST = 128
RMS_EPS = 1e-6
N_MOD = 9
PAGE = 128

LANES = 128
SUBLANES = 8
NEG = -1e30
INT_MIN = -(2 ** 31)
VMEM_LIMIT = 56 * 1024 * 1024

_NT = (((1,), (1,)), ((), ()))
_TN = (((0,), (0,)), ((), ()))


def _cp(sem):
    return pltpu.CompilerParams(dimension_semantics=sem, vmem_limit_bytes=VMEM_LIMIT)


def _silu(x):
    return x / (1.0 + jnp.exp(-x))


def _sigmoid(x):
    return 1.0 / (1.0 + jnp.exp(-x))


def _rms(x, g):
    return x * lax.rsqrt(jnp.mean(x * x, axis=-1, keepdims=True) + RMS_EPS) * g


def _group_mean_sq(x, bd):
    xx = x * x
    hi = xx.astype(BF16)
    lo = (xx - hi.astype(F32)).astype(BF16)
    return (jnp.dot(hi, bd, preferred_element_type=F32)
            + jnp.dot(lo, bd, preferred_element_type=F32))


def _ada_kernel(c_ref, w_ref, b_ref, o_ref):
    a = _silu(c_ref[...]).astype(BF16)
    o_ref[...] = jnp.dot(a, w_ref[...], preferred_element_type=F32) + b_ref[...]


def _ada(c_all, w16, b):
    rows = c_all.shape[0]
    return pl.pallas_call(
        _ada_kernel,
        out_shape=jax.ShapeDtypeStruct((N_MOD, rows, D_MODEL), F32),
        grid=(N_MOD,),
        in_specs=[pl.BlockSpec((rows, D_MODEL), lambda j: (0, 0)),
                  pl.BlockSpec((D_MODEL, D_MODEL), lambda j: (0, j)),
                  pl.BlockSpec((1, D_MODEL), lambda j: (0, j))],
        out_specs=pl.BlockSpec((None, rows, D_MODEL), lambda j: (j, 0, 0)),
        compiler_params=_cp(("arbitrary",)),
        name="ada",
    )(c_all, w16, b)


class _Mod:
    def __init__(self, arr, per_token, tiles_per_seq=1):
        self.arr = arr
        self.per_token = per_token
        self.tiles_per_seq = tiles_per_seq

    def spec(self, j, tm, ngrid):
        if self.per_token:
            if ngrid == 1:
                return pl.BlockSpec((None, tm, D_MODEL), lambda i: (j, i, 0))
            return pl.BlockSpec((None, tm, D_MODEL), lambda i, k: (j, i, 0))
        tps = self.tiles_per_seq
        if ngrid == 1:
            return pl.BlockSpec((None, None, 1, D_MODEL), lambda i: (j, i // tps, 0, 0))
        return pl.BlockSpec((None, None, 1, D_MODEL), lambda i, k: (j, i // tps, 0, 0))


def _ffn_kernel(x_ref, sh_ref, sc_ref, gt_ref, g_ref, wg_ref, wu_ref, wd_ref, o_ref, h_sc, acc_sc):
    k = pl.program_id(1)

    @pl.when(k == 0)
    def _():
        h = _rms(x_ref[...], g_ref[...]) * (1.0 + sc_ref[...]) + sh_ref[...]
        h_sc[...] = h.astype(BF16)
        acc_sc[...] = jnp.zeros_like(acc_sc)

    h = h_sc[...]
    a = jnp.dot(h, wg_ref[...], preferred_element_type=F32)
    b = jnp.dot(h, wu_ref[...], preferred_element_type=F32)
    act = (_silu(a) * b).astype(BF16)
    acc_sc[...] += jnp.dot(act, wd_ref[...], preferred_element_type=F32)

    @pl.when(k == pl.num_programs(1) - 1)
    def _():
        o_ref[...] = x_ref[...] + 0.5 * gt_ref[...] * acc_sc[...]


def _ffn(x, mod, j0, g, wg, wu, wd, tm, tf):
    n = x.shape[0]
    nk = D_FF // tf
    return pl.pallas_call(
        _ffn_kernel,
        out_shape=jax.ShapeDtypeStruct((n, D_MODEL), F32),
        grid=(n // tm, nk),
        in_specs=[pl.BlockSpec((tm, D_MODEL), lambda i, k: (i, 0)),
                  mod.spec(j0, tm, 2), mod.spec(j0 + 1, tm, 2), mod.spec(j0 + 2, tm, 2),
                  pl.BlockSpec((1, D_MODEL), lambda i, k: (0, 0)),
                  pl.BlockSpec((D_MODEL, tf), lambda i, k: (0, k)),
                  pl.BlockSpec((D_MODEL, tf), lambda i, k: (0, k)),
                  pl.BlockSpec((tf, D_MODEL), lambda i, k: (k, 0))],
        out_specs=pl.BlockSpec((tm, D_MODEL), lambda i, k: (i, 0)),
        scratch_shapes=[pltpu.VMEM((tm, D_MODEL), BF16), pltpu.VMEM((tm, D_MODEL), F32)],
        compiler_params=_cp(("arbitrary", "arbitrary")),
        name="ffn",
    )(x, mod.arr, mod.arr, mod.arr, g, wg, wu, wd)


_C_QA, _C_KA, _C_VA, _C_QI = 0, 512, 1024, 1536
_C_KI, _C_WI = 2048, 2176
_C_QB, _C_FB, _C_IB, _C_GB = 2304, 2816, 3328, 3840
_N_PACK = 4352


def _mix_kernel(x_ref, sh_ref, sc_ref, g_ref, w_ref, gq_ref, gk_ref, lbl_ref, bd_ref,
                qa_o, ka_o, ka16_o, va_o, va16_o, qi_o, ki_o, ki16_o, wi_o, qb_o, f_o, vb_o, sg_o):
    h = (_rms(x_ref[...], g_ref[...]) * (1.0 + sc_ref[...]) + sh_ref[...]).astype(BF16)

    def proj(c0, width):
        return jnp.dot(h, w_ref[:, c0:c0 + width], preferred_element_type=F32)

    bd = bd_ref[...]
    qa = proj(_C_QA, W_ATTN)
    qa = qa * lax.rsqrt(_group_mean_sq(qa, bd) + RMS_EPS) * gq_ref[...]
    qa_o[...] = (qa * (D_HA ** -0.5)).astype(BF16)
    ka = proj(_C_KA, W_ATTN)
    ka = ka * lax.rsqrt(_group_mean_sq(ka, bd) + RMS_EPS) * gk_ref[...]
    ka_o[...] = ka.T
    ka16_o[...] = ka.astype(BF16)
    va_t = proj(_C_VA, W_ATTN).T
    va_o[...] = va_t
    va16_o[...] = va_t.astype(BF16)
    qi_o[...] = (proj(_C_QI, H_I * D_I) * (D_I ** -0.5)).astype(BF16)
    ki = proj(_C_KI, LANES)
    ki_o[...] = ki
    ki16_o[...] = ki.astype(BF16)
    wi_o[...] = proj(_C_WI, LANES) * (H_I ** -0.5)
    qb_o[...] = _silu(proj(_C_QB, W_HGRN))
    l0 = lbl_ref[0:1, :]
    l1 = lbl_ref[1:2, :]
    mx = jnp.maximum(l0, l1)
    e0 = jnp.exp(l0 - mx)
    e1 = jnp.exp(l1 - mx)
    lb = e0 / (e0 + e1)
    f_o[...] = lb + (1.0 - lb) * _sigmoid(proj(_C_FB, W_HGRN))
    vb_o[...] = proj(_C_IB, W_HGRN)
    sg_o[...] = _silu(proj(_C_GB, W_HGRN))


def _mix(x, mod, g, wpack, gq, gk, lbl, bd, tm, rows_per_group):
    n = x.shape[0]
    tpg = rows_per_group // tm
    row = lambda w: pl.BlockSpec((tm, w), lambda i: (i, 0))
    full = lambda a: pl.BlockSpec(a.shape, lambda i: (0,) * a.ndim)
    featmajor = pl.BlockSpec((None, W_ATTN, tm), lambda i: (i // tpg, 0, i % tpg))
    outs = [(W_ATTN, BF16), None, (W_ATTN, BF16), None, None,
            (H_I * D_I, BF16), (LANES, F32), (LANES, BF16), (LANES, F32),
            (W_HGRN, F32), (W_HGRN, F32), (W_HGRN, F32), (W_HGRN, F32)]
    fm_dtypes = {1: F32, 3: F32, 4: BF16}
    return pl.pallas_call(
        _mix_kernel,
        out_shape=[jax.ShapeDtypeStruct((n // rows_per_group, W_ATTN, rows_per_group), fm_dtypes[k]) if o is None
                   else jax.ShapeDtypeStruct((n, o[0]), o[1]) for k, o in enumerate(outs)],
        grid=(n // tm,),
        in_specs=[row(D_MODEL), mod.spec(3, tm, 1), mod.spec(4, tm, 1), full(g), full(wpack),
                  full(gq), full(gk), full(lbl), full(bd)],
        out_specs=[featmajor if o is None else row(o[0]) for o in outs],
        compiler_params=_cp(("arbitrary",)),
        name="mix",
    )(x, mod.arr, mod.arr, g, wpack, gq, gk, lbl, bd)


def _t5_bucket(dist):
    n = jnp.maximum(dist, 0)
    max_exact = T5_BUCKETS // 2
    nf = jnp.maximum(n, 1).astype(F32)
    large = max_exact + (jnp.log(nf / max_exact) / math.log(T5_MAX_DIST / max_exact)
                         * (T5_BUCKETS - max_exact)).astype(I32)
    large = jnp.minimum(large, T5_BUCKETS - 1)
    return jnp.where(n < max_exact, n, large)


def _bias_kernel(rb_ref, bp_ref, bs_ref):
    def lookup(bucket, h):
        val = jnp.zeros(bucket.shape, F32)
        for b in range(T5_BUCKETS):
            val = jnp.where(bucket == b, rb_ref[b, h], val)
        return val

    k = lax.broadcasted_iota(I32, (2 * LANES, LANES), 0)
    q = lax.broadcasted_iota(I32, (2 * LANES, LANES), 1)
    far_bucket = _t5_bucket(jnp.full((2 * LANES, LANES), 2 * LANES, I32))
    for v, dist in enumerate((q - k, LANES + q - k)):
        bucket = _t5_bucket(dist)
        for h in range(H_A):
            bp_ref[v, h // 2, :, (h % 2) * LANES:(h % 2 + 1) * LANES] = lookup(bucket, h) - lookup(far_bucket, h)

    t = lax.broadcasted_iota(I32, (SUBLANES, LANES), 0)
    u = lax.broadcasted_iota(I32, (SUBLANES, LANES), 1)
    sdists = (PAGE + t - u, t - u, jnp.full((SUBLANES, LANES), 2 * LANES, I32))
    for m in range(3):
        bucket = _t5_bucket(sdists[m])
        for h in range(H_A):
            bs_ref[m, h * SUBLANES:(h + 1) * SUBLANES, :] = lookup(bucket, h)


def _bias_tiles(rel_bias):
    return pl.pallas_call(
        _bias_kernel,
        out_shape=[jax.ShapeDtypeStruct((2, H_A // 2, 2 * LANES, 2 * LANES), F32),
                   jax.ShapeDtypeStruct((3, H_A * SUBLANES, LANES), F32)],
        in_specs=[pl.BlockSpec(memory_space=pltpu.SMEM)],
        out_specs=[pl.BlockSpec(memory_space=pltpu.VMEM), pl.BlockSpec(memory_space=pltpu.VMEM)],
        name="t5_bias",
    )(rel_bias)


def _sortable_key(score):
    bits = pltpu.bitcast(score, I32)
    return jnp.where(bits < 0, INT_MIN - bits, bits)


def _tree_reduce(x, op):
    while x.shape[0] > 1:
        half = x.shape[0] // 2
        y = op(x[:half], x[half:2 * half])
        x = y if x.shape[0] % 2 == 0 else jnp.concatenate([y, x[2 * half:]], axis=0)
    return x[0]


def _sublane_allreduce(x, op):
    for sh in (4, 2, 1):
        x = op(x, pltpu.roll(x, sh, 0))
    return x


def _topk_maskbias_t(keys_ref, maskb_ref, far_ref, nch, cw, n_sel, idx_bits, far_limit):
    ng = cw // SUBLANES
    sub = lax.broadcasted_iota(I32, (ng, SUBLANES, LANES), 0) * SUBLANES + lax.broadcasted_iota(
        I32, (ng, SUBLANES, LANES), 1)

    def count(indicator):
        def body(c, acc):
            k0 = pl.multiple_of(c * cw, cw)
            kk = keys_ref[pl.ds(k0, cw), :].reshape(ng, SUBLANES, LANES)
            return acc + _tree_reduce(indicator(kk, k0), jnp.add)
        acc = lax.fori_loop(0, nch, body, jnp.zeros((SUBLANES, LANES), F32))
        return _sublane_allreduce(acc, jnp.add)

    ncls = cw // 2
    assert ncls >= n_sel and ncls % SUBLANES == 0

    def class_max(c, gm):
        kk = keys_ref[pl.ds(pl.multiple_of(c * cw, cw), cw), :]
        return jnp.maximum(gm, jnp.maximum(kk[:ncls], kk[ncls:]))

    gm = lax.fori_loop(0, nch, class_max, jnp.full((ncls, LANES), INT_MIN, I32))
    gm = gm.reshape(ncls // SUBLANES, SUBLANES, LANES)
    lo_u = _sublane_allreduce(_tree_reduce(gm, jnp.minimum), jnp.minimum) ^ INT_MIN
    hi_u = _sublane_allreduce(_tree_reduce(gm, jnp.maximum), jnp.maximum) ^ INT_MIN
    span = hi_u - lo_u
    smear = span
    for sh in (1, 2, 4, 8, 16):
        smear = smear | lax.shift_right_logical(smear, jnp.full_like(smear, sh))
    nbits = jnp.max(lax.population_count(smear).astype(F32)).astype(I32)
    nbits = lax.shift_left(lax.shift_right_logical(nbits + 3, 2), 2)

    n_adm = count(lambda kk, k0: jnp.where(kk > INT_MIN, 1.0, 0.0))
    cur0 = count(lambda kk, k0: jnp.where(kk >= (lo_u ^ INT_MIN)[None], 1.0, 0.0))

    def all_settled(cur):
        return jnp.min(jnp.where(cur == n_sel, 1.0, jnp.where(n_adm < n_sel, 1.0, 0.0)))

    def four_bits(carry):
        it, d, cur, _ = carry
        for j in range(4):
            cand = d | jnp.left_shift(jnp.int32(1), nbits - 1 - (it + j))
            in_range = (cand ^ INT_MIN) <= (span ^ INT_MIN)
            cs = (lo_u + cand) ^ INT_MIN
            cnt = count(lambda kk, k0: jnp.where(kk >= cs[None], 1.0, 0.0))
            take = jnp.where(in_range, jnp.where(cnt >= n_sel, 1.0, 0.0), 0.0) > 0.5
            d = jnp.where(take, cand, d)
            cur = jnp.where(take, cnt, cur)
        return it + 4, d, cur, all_settled(cur)

    _, d, cur, _ = lax.while_loop(lambda c: jnp.logical_and(c[0] < nbits, c[3] < 0.5), four_bits,
                                  (jnp.int32(0), jnp.zeros((SUBLANES, LANES), I32), cur0, all_settled(cur0)))
    thr = (lo_u + d) ^ INT_MIN
    n_eq = count(lambda kk, k0: jnp.where(kk == thr[None], 1.0, 0.0))
    need = n_sel - (cur - n_eq)
    has_tie = jnp.max(jnp.where(thr == INT_MIN, 0.0, jnp.where(n_eq > need, 1.0, 0.0))) > 0.0

    def tie_search(_):
        def step(it, aj):
            cand = aj | jnp.left_shift(jnp.int32(1), idx_bits - 1 - it)
            cnt = count(lambda kk, k0: jnp.where(kk == thr[None], jnp.where((k0 + sub) < cand[None], 1.0, 0.0), 0.0))
            return jnp.where(cnt < need, cand, aj)
        return lax.fori_loop(0, idx_bits, step, jnp.zeros((SUBLANES, LANES), I32))

    cut = lax.cond(has_tie, tie_search, lambda _: jnp.full((SUBLANES, LANES), 2 ** 30, I32), 0)
    cut = jnp.where(thr == INT_MIN, -1, cut)

    def fin(c, _):
        k0 = pl.multiple_of(c * cw, cw)
        kk = keys_ref[pl.ds(k0, cw), :].reshape(ng, SUBLANES, LANES)
        kpos = k0 + sub
        tie = jnp.where(kpos <= cut[None], 0.0, NEG)
        mb = jnp.where(kk > thr[None], 0.0, jnp.where(kk == thr[None], tie, NEG))
        maskb_ref[pl.ds(k0, cw), :] = mb.reshape(cw, LANES)
        if far_ref is not None:
            far_ref[pl.ds(k0, cw), :] = jnp.where(kpos < far_limit, mb, NEG).reshape(cw, LANES)
        return 0

    lax.fori_loop(0, nch, fin, 0)


def _softmax_step(carry, s, v16, v_is_transposed=False):
    m, l, acc = carry
    mn = jnp.maximum(m, jnp.max(s, axis=1, keepdims=True))
    alpha = jnp.exp(m - mn)
    p = jnp.exp(s - mn)
    l = alpha * l + jnp.sum(p, axis=1, keepdims=True)
    if v_is_transposed:
        pv = lax.dot_general(p.astype(BF16), v16, _NT, preferred_element_type=F32)
    else:
        pv = jnp.dot(p.astype(BF16), v16, preferred_element_type=F32)
    return mn, l, alpha * acc + pv


_QB = 128
_KC = 512
_FW = 512


def _prompt_attn_step(step, nsteps, i, qa_ref, qi_ref, wi_ref, ki_ref, ka_ref, vat_ref, bias_ref, o_ref,
                      keys_sc, maskb_sc, far_sc, qs_sc, q2_sc, m_sc, l_sc, acc_sc):
    lane = lax.broadcasted_iota(I32, (_QB, LANES), 1)
    lo = lane < D_I
    nch = i // (_KC // _QB) + 1
    near0 = jnp.maximum(i - 1, 0) * _QB
    npair = H_A // 2
    cols2 = 2 * _QB

    @pl.when(step == 0)
    def _():
        for h in range(H_I):
            pair = qi_ref[:, (h // 2) * LANES:(h // 2 + 1) * LANES].astype(F32)
            keep = jnp.where(lo, pair, 0.0) if h % 2 == 0 else jnp.where(lo, 0.0, pair)
            qs_sc[h * _QB:(h + 1) * _QB, :] = keep.astype(BF16)

        qpos = i * _QB + lax.broadcasted_iota(I32, (_KC, LANES), 1)

        def scores(c, _):
            k0 = pl.multiple_of(c * _KC, _KC)
            s = lax.dot_general(ki_ref[pl.ds(k0, _KC), :], qs_sc[...], _NT, preferred_element_type=F32)
            sc = jnp.zeros((_KC, LANES), F32)
            for h in range(H_I):
                sc = sc + jnp.maximum(s[:, h * _QB:(h + 1) * _QB], 0.0) * wi_ref[h:h + 1, :]
            kpos = k0 + lax.broadcasted_iota(I32, (_KC, LANES), 0)
            keys_sc[pl.ds(k0, _KC), :] = jnp.where(kpos <= qpos, _sortable_key(sc), INT_MIN)
            return 0

        lax.fori_loop(0, nch, scores, 0)
        _topk_maskbias_t(keys_sc, maskb_sc, far_sc, nch, _KC, TOPK_MAX, 12, near0)

        for p in range(npair):
            qp = qa_ref[:, p * LANES:(p + 1) * LANES].astype(F32)
            q2_sc[p * cols2:(p + 1) * cols2, :] = jnp.concatenate(
                [jnp.where(lo, qp, 0.0), jnp.where(lo, 0.0, qp)], axis=0).astype(BF16)
        m_sc[...] = jnp.full(m_sc.shape, NEG, F32)
        l_sc[...] = jnp.zeros_like(l_sc)
        acc_sc[...] = jnp.zeros_like(acc_sc)

    def update_all(k0, width, mask_ref, biases):
        mb = mask_ref[pl.ds(k0, width), :]
        mb2 = jnp.concatenate([mb, mb], axis=1)

        def qk(p):
            s = lax.dot_general(ka_ref[pl.ds(k0, width), p * LANES:(p + 1) * LANES],
                                q2_sc[p * cols2:(p + 1) * cols2, :], _NT, preferred_element_type=F32) + mb2
            if biases is not None:
                s = s + biases[p]
            return s.reshape(width // SUBLANES, SUBLANES, cols2)

        def softmax(p, s):
            m_old = m_sc[p]
            m_new = jnp.maximum(m_old, _sublane_allreduce(_tree_reduce(s, jnp.maximum), jnp.maximum))
            alpha = jnp.exp(m_old - m_new)
            pexp = jnp.exp(s - m_new[None])
            m_sc[p] = m_new
            l_sc[p] = alpha * l_sc[p] + _tree_reduce(pexp, jnp.add)
            return pexp.reshape(width, cols2).astype(BF16), alpha

        def pv(p, p16, alpha):
            out = jnp.dot(vat_ref[p * LANES:(p + 1) * LANES, pl.ds(k0, width)], p16,
                          preferred_element_type=F32)
            acc = acc_sc[p].reshape(LANES // SUBLANES, SUBLANES, cols2) * alpha[None]
            acc_sc[p] = acc.reshape(LANES, cols2) + out

        logits = [qk(p) for p in range(npair)]
        probs = [softmax(p, logits[p]) for p in range(npair)]
        for p in range(npair):
            pv(p, *probs[p])

    def far(j, _):
        update_all(pl.multiple_of((step + j * nsteps) * _FW, _FW), _FW, far_sc, None)
        return 0

    nfar = (near0 + _FW - 1) // _FW
    lax.fori_loop(0, (nfar - step + nsteps - 1) // nsteps, far, 0)

    @pl.when(step == nsteps - 1)
    def _():
        first = jnp.minimum(i, 1)
        update_all(pl.multiple_of(near0, _QB), 2 * _QB, maskb_sc, [bias_ref[first, p] for p in range(npair)])
        for p in range(npair):
            l = _sublane_allreduce(l_sc[p], jnp.add)
            o = (acc_sc[p].reshape(LANES // SUBLANES, SUBLANES, cols2) / l[None]).reshape(LANES, cols2)
            ot = jnp.concatenate([o[:D_HA, :_QB], o[D_HA:, _QB:]], axis=0)
            o_ref[:, p * LANES:(p + 1) * LANES] = ot.T


_PPS = 16
_TP = SUBLANES
_TK = 2 * SUBLANES


class _PageStream:
    def __init__(self, pt_ref, hbm_refs, bufs, sems, pages_per_group):
        self.pt_ref, self.hbm_refs, self.bufs, self.sems, self.ppg = pt_ref, hbm_refs, bufs, sems, pages_per_group

    def _copies(self, seq, group, slot):
        for hbm, buf, sem in zip(self.hbm_refs, self.bufs, self.sems):
            for p in range(self.ppg):
                yield pltpu.make_async_copy(hbm.at[self.pt_ref[seq, group * self.ppg + p]], buf.at[slot, p],
                                            sem.at[slot])

    def start(self, seq, group, slot):
        for cp in self._copies(seq, group, slot):
            cp.start()

    def wait(self, seq, group, slot):
        for cp in self._copies(seq, group, slot):
            cp.wait()

    def advance(self, step, nsteps, groups_per_seq):
        slot = lax.rem(step, 2)

        @pl.when(step == 0)
        def _():
            self.start(0, 0, 0)

        @pl.when(step + 1 < nsteps)
        def _():
            nxt = step + 1
            self.start(nxt // groups_per_seq, lax.rem(nxt, groups_per_seq), 1 - slot)

        self.wait(step // groups_per_seq, lax.rem(step, groups_per_seq), slot)
        return slot


def _sample_scores_kernel(pt_ref, q_ref, w_ref, kn_ref, cki_ref, o_ref, buf, sem):
    n_pages = buf.shape[1]
    slot = _PageStream(pt_ref, [cki_ref], [buf], [sem], n_pages).advance(pl.program_id(0), pl.num_programs(0), 1)
    q = q_ref[...]

    def head_sum(s):
        n = s.shape[1]
        out = jnp.zeros((_TP, n), F32)
        for h in range(H_I):
            wh = w_ref[h * _TP:(h + 1) * _TP, :]
            out = out + jnp.maximum(s[h * _TP:(h + 1) * _TP, :], 0.0) * jnp.concatenate([wh] * (n // LANES), axis=1)
        return out

    for c in range(n_pages // _PPS):
        kc = jnp.concatenate([buf[slot, c * _PPS + p].astype(BF16) for p in range(_PPS)], axis=1)
        s = jnp.dot(q, kc, preferred_element_type=F32)
        o_ref[:, c * _PPS * PAGE:(c + 1) * _PPS * PAGE] = head_sum(s)

    kn = jnp.concatenate([kn_ref[...], jnp.zeros((LANES - _TK, D_I), BF16)], axis=0)
    sn = lax.dot_general(q, kn, _NT, preferred_element_type=F32)
    o_ref[:, o_ref.shape[1] - LANES:] = head_sum(sn)


def _sample_scores(page_table, q_st, w_b, ki_new, cache_kidx, n_pages):
    nseq = q_st.shape[0]
    lpad = n_pages * PAGE + LANES
    return pl.pallas_call(
        _sample_scores_kernel,
        out_shape=jax.ShapeDtypeStruct((nseq, _TP, lpad), F32),
        grid_spec=pltpu.PrefetchScalarGridSpec(
            num_scalar_prefetch=1, grid=(nseq,),
            in_specs=[pl.BlockSpec((None, H_I * _TP, D_I), lambda b, pt: (b, 0, 0)),
                      pl.BlockSpec((None, H_I * _TP, LANES), lambda b, pt: (b, 0, 0)),
                      pl.BlockSpec((None, _TK, D_I), lambda b, pt: (b, 0, 0)),
                      pl.BlockSpec(memory_space=pl.ANY)],
            out_specs=pl.BlockSpec((None, _TP, lpad), lambda b, pt: (b, 0, 0)),
            scratch_shapes=[pltpu.VMEM((2, n_pages, D_I, PAGE), F32), pltpu.SemaphoreType.DMA((2,))]),
        compiler_params=_cp(("arbitrary",)),
        name="sample_scores",
    )(page_table, q_st, w_b, ki_new, cache_kidx)


_SEL_CW = 640


def _sample_select_kernel(s_ref, o_ref, keys_sc, *, past, t_dec):
    lpad = s_ref.shape[0]
    t = lax.broadcasted_iota(I32, (_SEL_CW, LANES), 1) % _TP
    limit = jnp.where(t < t_dec, past + t, -1)
    for c in range(lpad // _SEL_CW):
        sl = slice(c * _SEL_CW, (c + 1) * _SEL_CW)
        kpos = c * _SEL_CW + lax.broadcasted_iota(I32, (_SEL_CW, LANES), 0)
        keys_sc[sl, :] = jnp.where(kpos <= limit, _sortable_key(s_ref[sl, :]), INT_MIN)
    _topk_maskbias_t(keys_sc, o_ref, None, lpad // _SEL_CW, _SEL_CW, TOPK_MAX, 14, None)


def _sample_select(scores_t, past, t_dec):
    lpad, cols = scores_t.shape
    return pl.pallas_call(
        functools.partial(_sample_select_kernel, past=past, t_dec=t_dec),
        out_shape=jax.ShapeDtypeStruct((lpad, cols), F32),
        grid=(cols // LANES,),
        in_specs=[pl.BlockSpec((lpad, LANES), lambda g: (0, g))],
        out_specs=pl.BlockSpec((lpad, LANES), lambda g: (0, g)),
        scratch_shapes=[pltpu.VMEM((lpad, LANES), I32)],
        compiler_params=_cp(("arbitrary",)),
        name="sample_select",
    )(scores_t)


def _sample_attn_step(s_id, last, q_ref, mb_ref, kn_ref, vn_ref, bias_ref, kp, vp, o_ref,
                      qblk_sc, m_sc, l_sc, acc_sc):
    rows = H_A * _TP
    span = _PPS * PAGE

    @pl.when(s_id == 0)
    def _():
        q = jnp.concatenate([q_ref[...]] * H_A, axis=0)
        r = lax.broadcasted_iota(I32, (rows, W_ATTN), 0) // _TP
        c = lax.broadcasted_iota(I32, (rows, W_ATTN), 1) // D_HA
        qblk_sc[...] = jnp.where(r == c, q, 0.0).astype(BF16)
        m_sc[...] = jnp.full(m_sc.shape, NEG, F32)
        l_sc[...] = jnp.zeros_like(l_sc)
        acc_sc[...] = jnp.zeros_like(acc_sc)

    def update(s, v16, v_is_transposed):
        carry = (m_sc[:, 0:1], l_sc[:, 0:1], acc_sc[...])
        m, l, acc = _softmax_step(carry, s, v16, v_is_transposed)
        m_sc[...] = jnp.broadcast_to(m, m_sc.shape)
        l_sc[...] = jnp.broadcast_to(l, l_sc.shape)
        acc_sc[...] = acc

    k16 = jnp.concatenate([pg[...].astype(BF16) for pg in kp], axis=1)
    v16 = jnp.concatenate([pg[...].astype(BF16) for pg in vp], axis=1)
    s = jnp.dot(qblk_sc[...], k16, preferred_element_type=F32)
    mb = mb_ref[:, pl.ds(pl.multiple_of(s_id * span, span), span)]
    last_tile = jnp.where(s_id == last, bias_ref[0], bias_ref[2])
    bias = jnp.concatenate([bias_ref[2]] * (_PPS - 1) + [last_tile], axis=1)
    update(s + jnp.concatenate([mb] * H_A, axis=0) + bias, v16, True)

    @pl.when(s_id == last)
    def _():
        pad = jnp.zeros((LANES - _TK, W_ATTN), BF16)
        kn = jnp.concatenate([kn_ref[...], pad], axis=0)
        vn = jnp.concatenate([vn_ref[...], pad], axis=0)
        sn = lax.dot_general(qblk_sc[...], kn, _NT, preferred_element_type=F32)
        mbn = mb_ref[:, mb_ref.shape[1] - LANES:]
        update(sn + jnp.concatenate([mbn] * H_A, axis=0) + bias_ref[1], vn, False)
        o = acc_sc[...] / l_sc[:, 0:1]
        c = lax.broadcasted_iota(I32, (_TP, W_ATTN), 1) // D_HA
        out = jnp.zeros((_TP, W_ATTN), F32)
        for h in range(H_A):
            out = jnp.where(c == h, o[h * _TP:(h + 1) * _TP, :], out)
        o_ref[...] = out


def _attn_kernel(pt_ref, qa_ref, qi_ref, wi_ref, ki_ref, ka_ref, vat_ref, bias_p_ref,
                 sq_ref, smb_ref, skn_ref, svn_ref, bias_s_ref, ck_ref, cv_ref, o_ref, so_ref, *scratch,
                 nq, nsteps, nseq):
    prompt_scratch, sample_scratch = scratch[:8], scratch[8:12]
    kbuf, vbuf, ksem, vsem = scratch[12:]
    step = pl.program_id(2)
    seq_id = pl.program_id(0) * nq + pl.program_id(1)
    stream = _PageStream(pt_ref, [ck_ref, cv_ref], [kbuf, vbuf], [ksem, vsem], _PPS)
    slot = stream.advance(seq_id * nsteps + step, nseq * nsteps, nsteps)
    _prompt_attn_step(step, nsteps, pl.program_id(1), qa_ref, qi_ref, wi_ref, ki_ref, ka_ref, vat_ref, bias_p_ref,
                      o_ref, *prompt_scratch)
    _sample_attn_step(step, nsteps - 1, sq_ref, smb_ref, skn_ref, svn_ref, bias_s_ref,
                      [kbuf.at[slot, p] for p in range(_PPS)], [vbuf.at[slot, p] for p in range(_PPS)], so_ref,
                      *sample_scratch)


def _attention(qa16, qi16, wi_t, ki16, ka16, va16_t, bias_p, nbatch, seq,
               page_table, q8, maskb, k_new, v_new, bias_s, cache_k, cache_v, n_pages):
    nq = seq // _QB
    nseq = q8.shape[0]
    steps = n_pages // _PPS
    lpad = maskb.shape[1]
    npair = H_A // 2
    assert nseq == nbatch * nq, "one decode sequence per prompt query block"
    blk = lambda w: pl.BlockSpec((_QB, w), lambda b, i, s, pt: (b * nq + i, 0))
    once = pl.Buffered(1)
    per_b = lambda w: pl.BlockSpec((seq, w), lambda b, i, s, pt: (b, 0), pipeline_mode=once)
    seq3 = lambda t: pl.BlockSpec((None, t, W_ATTN), lambda b, i, s, pt: (b * nq + i, 0, 0))
    hbm = pl.BlockSpec(memory_space=pl.ANY)
    return pl.pallas_call(
        functools.partial(_attn_kernel, nq=nq, nsteps=steps, nseq=nseq),
        out_shape=[jax.ShapeDtypeStruct((nbatch * seq, W_ATTN), F32),
                   jax.ShapeDtypeStruct((nseq, _TP, W_ATTN), F32)],
        grid_spec=pltpu.PrefetchScalarGridSpec(
            num_scalar_prefetch=1, grid=(nbatch, nq, steps),
            in_specs=[blk(W_ATTN), blk(H_I * D_I),
                      pl.BlockSpec((H_I, _QB), lambda b, i, s, pt: (0, b * nq + i)),
                      per_b(LANES), per_b(W_ATTN),
                      pl.BlockSpec((None, W_ATTN, seq), lambda b, i, s, pt: (b, 0, 0), pipeline_mode=once),
                      pl.BlockSpec(bias_p.shape, lambda b, i, s, pt: (0, 0, 0, 0), pipeline_mode=once),
                      seq3(_TP), pl.BlockSpec((_TP, lpad), lambda b, i, s, pt: (b * nq + i, 0)),
                      seq3(_TK), seq3(_TK),
                      pl.BlockSpec(bias_s.shape, lambda b, i, s, pt: (0, 0, 0), pipeline_mode=once),
                      hbm, hbm],
            out_specs=[blk(W_ATTN), seq3(_TP)],
            scratch_shapes=[pltpu.VMEM((seq, LANES), I32), pltpu.VMEM((seq, LANES), F32),
                            pltpu.VMEM((seq, LANES), F32),
                            pltpu.VMEM((H_I * _QB, LANES), BF16), pltpu.VMEM((H_A * _QB, LANES), BF16),
                            pltpu.VMEM((npair, SUBLANES, 2 * _QB), F32), pltpu.VMEM((npair, SUBLANES, 2 * _QB), F32),
                            pltpu.VMEM((npair, LANES, 2 * _QB), F32),
                            pltpu.VMEM((H_A * _TP, W_ATTN), BF16), pltpu.VMEM((H_A * _TP, LANES), F32),
                            pltpu.VMEM((H_A * _TP, LANES), F32), pltpu.VMEM((H_A * _TP, W_ATTN), F32),
                            pltpu.VMEM((2, _PPS, W_ATTN, PAGE), F32), pltpu.VMEM((2, _PPS, W_ATTN, PAGE), F32),
                            pltpu.SemaphoreType.DMA((2,)), pltpu.SemaphoreType.DMA((2,))]),
        compiler_params=_cp(("arbitrary", "arbitrary", "arbitrary")),
        name="attention",
    )(page_table, qa16, qi16, wi_t, ki16, ka16, va16_t, bias_p, q8, maskb, k_new, v_new, bias_s, cache_k, cache_v)


def _hgrn_chunk(q, k, v, logf, st, c):
    row = lax.broadcasted_iota(I32, (c, DK_B), 0)
    g = logf
    sh = 1
    while sh < c:
        g = g + jnp.where(row >= sh, pltpu.roll(g, sh, 0), 0.0)
        sh *= 2

    o = lax.dot_general((q * jnp.exp(g)).astype(BF16), st.astype(BF16), _NT, preferred_element_type=F32)

    if c > SUBLANES:
        rr = lax.broadcasted_iota(I32, (c, c), 0)
        cc = lax.broadcasted_iota(I32, (c, c), 1)
        a = jnp.zeros((c, c), F32)
        w = SUBLANES
        while w < c:
            nb = c // (2 * w)
            gb = jnp.broadcast_to(g.reshape(nb, 2 * w, DK_B)[:, w - 1:w, :], (nb, 2 * w, DK_B)).reshape(c, DK_B)
            right = (row % (2 * w)) >= w
            qt = jnp.where(right, q * jnp.exp(jnp.where(right, g - gb, 0.0)), 0.0)
            kt = jnp.where(right, 0.0, k * jnp.exp(jnp.where(right, 0.0, gb - g)))
            aw = lax.dot_general(qt.astype(BF16), kt.astype(BF16), _NT, preferred_element_type=F32)
            a = a + jnp.where((rr // (2 * w)) == (cc // (2 * w)), aw, 0.0)
            w *= 2
        o = o + jnp.dot(a.astype(BF16), v.astype(BF16), preferred_element_type=F32)

    o = o + jnp.sum(q * k, axis=1, keepdims=True) * v
    for d in range(1, SUBLANES):
        ok = (row % SUBLANES) >= d
        e = jnp.exp(jnp.where(ok, g - pltpu.roll(g, d, 0), 0.0))
        coef = jnp.sum(jnp.where(ok, q * pltpu.roll(k, d, 0) * e, 0.0), axis=1, keepdims=True)
        o = o + coef * pltpu.roll(v, d, 0)

    g_end = g[c - 1:c, :]
    kd = k * jnp.exp(g_end - g)
    st = st * jnp.exp(g_end) + lax.dot_general(v.astype(BF16), kd.astype(BF16), _TN, preferred_element_type=F32)
    return o, st


_HG = 2


def _hgrn_kernel(q_ref, f_ref, v_ref, s0_ref, o_ref, s_ref, *, nb, t_len, c):
    for j in range(nb):
        def body(ci, states):
            r = pl.ds(pl.multiple_of(ci * c, c), c)
            new_states = []
            for g in range(_HG):
                cs = slice(g * DK_B, (g + 1) * DK_B)
                f = f_ref[j, r, cs]
                o, st = _hgrn_chunk(q_ref[j, r, cs], 1.0 - f, v_ref[j, r, cs], jnp.log(f), states[g], c)
                o_ref[j, r, cs] = o
                new_states.append(st)
            return tuple(new_states)
        states = lax.fori_loop(0, t_len // c, body, tuple(s0_ref[j, g].T for g in range(_HG)))
        for g in range(_HG):
            s_ref[j, g] = states[g].T


def _hgrn(q, f, v, s0, nb, c):
    nseq, t_len, _ = q.shape
    tok = pl.BlockSpec((nb, t_len, _HG * DK_B), lambda b, h: (b, 0, h))
    st = pl.BlockSpec((nb, _HG, DK_B, DV_B), lambda b, h: (b, h, 0, 0))
    return pl.pallas_call(
        functools.partial(_hgrn_kernel, nb=nb, t_len=t_len, c=c),
        out_shape=[jax.ShapeDtypeStruct((nseq, t_len, W_HGRN), F32),
                   jax.ShapeDtypeStruct((nseq, H_B, DK_B, DV_B), F32)],
        grid=(nseq // nb, H_B // _HG),
        in_specs=[tok, tok, tok, st],
        out_specs=[tok, st],
        compiler_params=_cp(("arbitrary", "arbitrary")),
        name="hgrn",
    )(q, f, v, s0)


def _merge_kernel(oa_ref, oh_ref, sg_ref, x_ref, gt_ref, beta_ref, gh_ref, bd_ref, w_ref, o_ref):
    oa = oa_ref[...]
    a = oa * lax.rsqrt(_group_mean_sq(oa, bd_ref[...]) + RMS_EPS) * beta_ref[...]
    oh = oh_ref[...]
    parts = []
    for h in range(H_B):
        sl = slice(h * DV_B, (h + 1) * DV_B)
        parts.append(_rms(oh[:, sl], gh_ref[:, sl]))
    r = jnp.concatenate(parts, axis=1) * sg_ref[...]
    y = (jnp.dot(a.astype(BF16), w_ref[:W_ATTN, :], preferred_element_type=F32)
         + jnp.dot(r.astype(BF16), w_ref[W_ATTN:, :], preferred_element_type=F32))
    o_ref[...] = x_ref[...] + gt_ref[...] * y


def _merge(oa, oh, sg, x, mod, beta, gh, bd, w16, tm):
    n = x.shape[0]
    row = lambda w: pl.BlockSpec((tm, w), lambda i: (i, 0))
    full = lambda a: pl.BlockSpec(a.shape, lambda i: (0,) * a.ndim)
    return pl.pallas_call(
        _merge_kernel,
        out_shape=jax.ShapeDtypeStruct((n, D_MODEL), F32),
        grid=(n // tm,),
        in_specs=[row(W_ATTN), row(W_HGRN), row(W_HGRN), row(D_MODEL), mod.spec(5, tm, 1),
                  full(beta), full(gh), full(bd), full(w16)],
        out_specs=row(D_MODEL),
        compiler_params=_cp(("arbitrary",)),
        name="merge",
    )(oa, oh, sg, x, mod.arr, beta, gh, bd, w16)


def _pad_tokens(a, value=0.0, to=_TP):
    return jnp.pad(a, ((0, 0), (0, to - a.shape[1]), (0, 0)), constant_values=value)


def kernel(x_prompt, x_sample, cache_k, cache_v, cache_kidx, state_hgrn, page_table, c_prompt, c_sample,
           w_ada, b_ada, g_ffn1, ffn1_w_gate, ffn1_w_up, ffn1_w_down, g_mix, w_in, g_q, g_k,
           beta_attn, g_hgrn, w_out, g_ffn2, ffn2_w_gate, ffn2_w_up, ffn2_w_down, rel_bias, lb_logits):
    assert w_ada.shape[0] == 1, "single-layer problem"
    nb_p, seq, _ = x_prompt.shape
    nb_s, t_dec, _ = x_sample.shape
    n_phys = cache_k.shape[1]
    n_pages = page_table.shape[1]
    past = n_pages * PAGE
    assert t_dec <= _TP and n_pages % _PPS == 0 and (past + LANES) % _SEL_CW == 0

    row2 = lambda a: a.reshape(1, -1)
    w_in0 = w_in[0]
    pad_w = jnp.zeros((D_MODEL, LANES - H_I), F32)
    wpack = jnp.concatenate(
        [w_in0[:, :2048], w_in0[:, 2048:2112], w_in0[:, 2048:2112], w_in0[:, 2112:2120], pad_w,
         w_in0[:, 2120:]], axis=1).astype(BF16)
    assert wpack.shape[1] == _N_PACK
    gq = row2(jnp.tile(g_q[0], H_A))
    gk = row2(jnp.tile(g_k[0], H_A))
    gidx = jnp.arange(W_ATTN) // D_HA
    bd = jnp.where(gidx[:, None] == gidx[None, :], 1.0 / D_HA, 0.0).astype(BF16)
    ffn1 = (row2(g_ffn1[0]), ffn1_w_gate[0].astype(BF16), ffn1_w_up[0].astype(BF16), ffn1_w_down[0].astype(BF16))
    ffn2 = (row2(g_ffn2[0]), ffn2_w_gate[0].astype(BF16), ffn2_w_up[0].astype(BF16), ffn2_w_down[0].astype(BF16))
    w_out16 = w_out[0].astype(BF16)

    n_c = nb_p + nb_s
    c_pad = (-n_c) % SUBLANES
    c_all = jnp.concatenate([c_prompt, c_sample, jnp.zeros((c_pad, D_MODEL), F32)], axis=0)
    mods = _ada(c_all, w_ada[0].astype(BF16), row2(b_ada[0]))
    bias_p, bias_s = _bias_tiles(rel_bias)

    def front(x, mod, tm, tf, rows_per_group):
        x1 = _ffn(x, mod, 0, *ffn1, tm, tf)
        return x1, _mix(x1, mod, row2(g_mix[0]), wpack, gq, gk, lb_logits, bd, tm, rows_per_group)

    def back(x1, mod, tm, tf, oa, oh, sg):
        x2 = _merge(oa, oh, sg, x1, mod, row2(beta_attn[0]), row2(g_hgrn[0]), bd, w_out16, tm)
        return _ffn(x2, mod, 6, *ffn2, tm, tf)

    tm_p, tf = 512, D_FF // 2
    n_tok = nb_s * t_dec
    mod_p = _Mod(mods[:, :nb_p].reshape(N_MOD, nb_p, 1, D_MODEL), False, seq // tm_p)
    mod_s = _Mod(jnp.repeat(mods[:, nb_p:nb_p + nb_s], t_dec, axis=1), True)
    x1p, proj_p = front(x_prompt.reshape(nb_p * seq, D_MODEL), mod_p, tm_p, tf, seq)
    x1s, proj_s = front(x_sample.reshape(n_tok, D_MODEL), mod_s, n_tok, tf, n_tok)
    qa16, kp_t, ka16, vp_t, va16_t, qi16, ki_p, ki16, wi, qb, f, vb, sg_p = proj_p
    qa16s, ks_t, ka16s, vs_t, va16s_t, qi16s, ki_s, ki16s, wis, qbs, fs, vbs, sg_s = proj_s
    va16s = va16s_t[0].T

    ck = jnp.transpose(cache_k[0], (0, 2, 3, 1)).reshape(n_phys, W_ATTN, PAGE)
    cv = jnp.transpose(cache_v[0], (0, 2, 3, 1)).reshape(n_phys, W_ATTN, PAGE)
    cki = jnp.transpose(cache_kidx[0], (0, 2, 1))
    seq3 = lambda a: a.reshape(nb_s, t_dec, a.shape[-1])
    q_st = _pad_tokens(seq3(qi16s)).reshape(nb_s, _TP, H_I, D_I).transpose(0, 2, 1, 3).reshape(nb_s, H_I * _TP, D_I)
    w_st = _pad_tokens(seq3(wis[:, :H_I])).transpose(0, 2, 1).reshape(nb_s, H_I * _TP, 1)
    w_b = jnp.broadcast_to(w_st, (nb_s, H_I * _TP, LANES))
    scores = _sample_scores(page_table, q_st, w_b, _pad_tokens(seq3(ki16s[:, :D_I]), to=_TK), cki, n_pages)
    maskb = _sample_select(scores.reshape(nb_s * _TP, -1).T, past, t_dec).T

    oa_p, oa8 = _attention(qa16, qi16, wi[:, :H_I].T, ki16, ka16, va16_t, bias_p, nb_p, seq,
                           page_table, _pad_tokens(seq3(qa16s.astype(F32))), maskb,
                           _pad_tokens(seq3(ka16s), to=_TK), _pad_tokens(seq3(va16s), to=_TK), bias_s, ck, cv, n_pages)
    r3 = lambda a: a.reshape(nb_p, seq, W_HGRN)
    oh_p, sp = _hgrn(r3(qb), r3(f), r3(vb), jnp.zeros((nb_p, H_B, DK_B, DV_B), F32), 1, 64)
    oh8, ss = _hgrn(_pad_tokens(seq3(qbs)), _pad_tokens(seq3(fs), 1.0), _pad_tokens(seq3(vbs)), state_hgrn[0], 8, _TP)

    yp = back(x1p, mod_p, tm_p, tf, oa_p, oh_p.reshape(nb_p * seq, W_HGRN), sg_p)
    ys = back(x1s, mod_s, n_tok, tf, oa8[:, :t_dec].reshape(n_tok, W_ATTN),
              oh8[:, :t_dec].reshape(n_tok, W_HGRN), sg_s)

    def rows_out(a_t, nseq, t_len):
        a = a_t.reshape(-1, H_A, D_HA, a_t.shape[-1]).transpose(0, 3, 1, 2)
        return a.reshape(1, nseq, t_len, H_A, D_HA)

    return (yp.reshape(nb_p, seq, D_MODEL).astype(x_prompt.dtype),
            ys.reshape(nb_s, t_dec, D_MODEL).astype(x_sample.dtype),
            rows_out(kp_t, nb_p, seq), rows_out(vp_t, nb_p, seq),
            ki_p[:, :D_I].reshape(1, nb_p, seq, D_I), sp[None],
            rows_out(ks_t, nb_s, t_dec), rows_out(vs_t, nb_s, t_dec),
            ki_s[:, :D_I].reshape(1, nb_s, t_dec, D_I), ss[None])
```

```python
import functools
import math

import jax
import jax.numpy as jnp
from jax import lax
from jax.experimental import pallas as pl
from jax.experimental.pallas import tpu as pltpu

F32 = jnp.float32
BF16 = jnp.bfloat16
I32 = jnp.int32

D_MODEL = 1024
W_ATTN = 512
W_HGRN = 512
D_HA = 64
H_A = 8
H_I = 8
D_I = 64
TOPK_MAX = 256
H_B = 4
DK_B = 128
DV_B = 128
D_FF = 2816
T5_BUCKETS = 32
T5_MAX_DIST = 128
RMS_EPS = 1e-6
N_MOD = 9
PAGE = 128

LANES = 128
SUBLANES = 8
NEG = -1e30
INT_MIN = -(2 ** 31)
VMEM_LIMIT = 56 * 1024 * 1024

_NT = (((1,), (1,)), ((), ()))
_TN = (((0,), (0,)), ((), ()))


def _cp(sem):
    return pltpu.CompilerParams(dimension_semantics=sem, vmem_limit_bytes=VMEM_LIMIT)


def _silu(x):
    return x / (1.0 + jnp.exp(-x))


def _sigmoid(x):
    return 1.0 / (1.0 + jnp.exp(-x))


def _rms(x, g):
    return x * lax.rsqrt(jnp.mean(x * x, axis=-1, keepdims=True) + RMS_EPS) * g


def _group_mean_sq(x, bd):
    xx = x * x
    hi = xx.astype(BF16)
    lo = (xx - hi.astype(F32)).astype(BF16)
    return (jnp.dot(hi, bd, preferred_element_type=F32)
            + jnp.dot(lo, bd, preferred_element_type=F32))


def _ada_kernel(c_ref, w_ref, b_ref, o_ref):
    a = _silu(c_ref[...]).astype(BF16)
    o_ref[...] = jnp.dot(a, w_ref[...], preferred_element_type=F32) + b_ref[...]


def _ada(c_all, w16, b):
    rows = c_all.shape[0]
    return pl.pallas_call(
        _ada_kernel,
        out_shape=jax.ShapeDtypeStruct((N_MOD, rows, D_MODEL), F32),
        grid=(N_MOD,),
        in_specs=[pl.BlockSpec((rows, D_MODEL), lambda j: (0, 0)),
                  pl.BlockSpec((D_MODEL, D_MODEL), lambda j: (0, j)),
                  pl.BlockSpec((1, D_MODEL), lambda j: (0, j))],
        out_specs=pl.BlockSpec((None, rows, D_MODEL), lambda j: (j, 0, 0)),
        compiler_params=_cp(("arbitrary",)),
        name="ada",
    )(c_all, w16, b)


class _Mod:
    def __init__(self, arr, per_token, tiles_per_seq=1):
        self.arr = arr
        self.per_token = per_token
        self.tiles_per_seq = tiles_per_seq

    def spec(self, j, tm, ngrid):
        if self.per_token:
            if ngrid == 1:
                return pl.BlockSpec((None, tm, D_MODEL), lambda i: (j, i, 0))
            return pl.BlockSpec((None, tm, D_MODEL), lambda i, k: (j, i, 0))
        tps = self.tiles_per_seq
        if ngrid == 1:
            return pl.BlockSpec((None, None, 1, D_MODEL), lambda i: (j, i // tps, 0, 0))
        return pl.BlockSpec((None, None, 1, D_MODEL), lambda i, k: (j, i // tps, 0, 0))


def _ffn_steps(load_x, x_res, sh_ref, sc_ref, gt_ref, g_ref, wg_ref, wu_ref, wd_ref, o_ref, h_sc, acc_sc):
    k = pl.program_id(1)

    @pl.when(k == 0)
    def _():
        h = _rms(load_x(), g_ref[...]) * (1.0 + sc_ref[...]) + sh_ref[...]
        h_sc[...] = h.astype(BF16)
        acc_sc[...] = jnp.zeros_like(acc_sc)

    h = h_sc[...]
    a = jnp.dot(h, wg_ref[...], preferred_element_type=F32)
    b = jnp.dot(h, wu_ref[...], preferred_element_type=F32)
    act = (_silu(a) * b).astype(BF16)
    acc_sc[...] += jnp.dot(act, wd_ref[...], preferred_element_type=F32)

    @pl.when(k == pl.num_programs(1) - 1)
    def _():
        o_ref[...] = x_res[...] + 0.5 * gt_ref[...] * acc_sc[...]


def _ffn_kernel(x_ref, *rest):
    _ffn_steps(lambda: x_ref[...], x_ref, *rest)


def _merge_ffn_kernel(oa_ref, oh_ref, sg_ref, x_ref, gtm_ref, beta_ref, gh_ref, bd_ref, wo_ref, *rest):
    x_sc = rest[-1]

    def load_x():
        oa = oa_ref[...]
        a = oa * lax.rsqrt(_group_mean_sq(oa, bd_ref[...]) + RMS_EPS) * beta_ref[...]
        oh = oh_ref[...]
        parts = []
        for h in range(H_B):
            sl = slice(h * DV_B, (h + 1) * DV_B)
            parts.append(_rms(oh[:, sl], gh_ref[:, sl]))
        r = jnp.concatenate(parts, axis=1) * sg_ref[...]
        y = (jnp.dot(a.astype(BF16), wo_ref[:W_ATTN, :], preferred_element_type=F32)
             + jnp.dot(r.astype(BF16), wo_ref[W_ATTN:, :], preferred_element_type=F32))
        x_sc[...] = x_ref[...] + gtm_ref[...] * y
        return x_sc[...]

    _ffn_steps(load_x, x_sc, *rest[:-1])


def _ffn_specs(mod, j0, tm, tf):
    return [mod.spec(j0, tm, 2), mod.spec(j0 + 1, tm, 2), mod.spec(j0 + 2, tm, 2),
            pl.BlockSpec((1, D_MODEL), lambda i, k: (0, 0)),
            pl.BlockSpec((D_MODEL, tf), lambda i, k: (0, k)),
            pl.BlockSpec((D_MODEL, tf), lambda i, k: (0, k)),
            pl.BlockSpec((tf, D_MODEL), lambda i, k: (k, 0))]


def _ffn(x, mod, j0, g, wg, wu, wd, tm, tf):
    n = x.shape[0]
    row = pl.BlockSpec((tm, D_MODEL), lambda i, k: (i, 0))
    return pl.pallas_call(
        _ffn_kernel,
        out_shape=jax.ShapeDtypeStruct((n, D_MODEL), F32),
        grid=(n // tm, D_FF // tf),
        in_specs=[row] + _ffn_specs(mod, j0, tm, tf),
        out_specs=row,
        scratch_shapes=[pltpu.VMEM((tm, D_MODEL), BF16), pltpu.VMEM((tm, D_MODEL), F32)],
        compiler_params=_cp(("arbitrary", "arbitrary")),
        name="ffn",
    )(x, mod.arr, mod.arr, mod.arr, g, wg, wu, wd)


def _merge_ffn(oa, oh, sg, x, mod, beta, gh, bd, wo16, g, wg, wu, wd, tm, tf):
    n = x.shape[0]
    row = lambda w: pl.BlockSpec((tm, w), lambda i, k: (i, 0))
    full = lambda a: pl.BlockSpec(a.shape, lambda i, k: (0,) * a.ndim)
    return pl.pallas_call(
        _merge_ffn_kernel,
        out_shape=jax.ShapeDtypeStruct((n, D_MODEL), F32),
        grid=(n // tm, D_FF // tf),
        in_specs=[row(W_ATTN), row(W_HGRN), row(W_HGRN), row(D_MODEL), mod.spec(5, tm, 2),
                  full(beta), full(gh), full(bd), full(wo16)] + _ffn_specs(mod, 6, tm, tf),
        out_specs=row(D_MODEL),
        scratch_shapes=[pltpu.VMEM((tm, D_MODEL), BF16), pltpu.VMEM((tm, D_MODEL), F32),
                        pltpu.VMEM((tm, D_MODEL), F32)],
        compiler_params=_cp(("arbitrary", "arbitrary")),
        name="merge_ffn",
    )(oa, oh, sg, x, mod.arr, beta, gh, bd, wo16, mod.arr, mod.arr, mod.arr, g, wg, wu, wd)


_C_QA, _C_KA, _C_VA, _C_QI = 0, 512, 1024, 1536
_C_KI, _C_WI = 2048, 2176
_C_QB, _C_FB, _C_IB, _C_GB = 2304, 2816, 3328, 3840
_N_PACK = 4352


def _mix_kernel(x_ref, sh_ref, sc_ref, g_ref, w_ref, gq_ref, gk_ref, lbl_ref, bd_ref,
                qa_o, ka_o, ka16_o, va_o, va16_o, qi_o, ki_o, ki16_o, wi_o, qb_o, f_o, vb_o, sg_o):
    h = (_rms(x_ref[...], g_ref[...]) * (1.0 + sc_ref[...]) + sh_ref[...]).astype(BF16)

    def proj(c0, width):
        return jnp.dot(h, w_ref[:, c0:c0 + width], preferred_element_type=F32)

    bd = bd_ref[...]
    qa = proj(_C_QA, W_ATTN)
    qa = qa * lax.rsqrt(_group_mean_sq(qa, bd) + RMS_EPS) * gq_ref[...]
    qa_o[...] = (qa * (D_HA ** -0.5)).astype(BF16)
    ka = proj(_C_KA, W_ATTN)
    ka = ka * lax.rsqrt(_group_mean_sq(ka, bd) + RMS_EPS) * gk_ref[...]
    ka_o[...] = ka.T
    ka16_o[...] = ka.astype(BF16)
    va_t = proj(_C_VA, W_ATTN).T
    va_o[...] = va_t
    va16_o[...] = va_t.astype(BF16)
    qi_o[...] = (proj(_C_QI, H_I * D_I) * (D_I ** -0.5)).astype(BF16)
    ki = proj(_C_KI, LANES)
    ki_o[...] = ki
    ki16_o[...] = ki.astype(BF16)
    wi_o[...] = proj(_C_WI, LANES) * (H_I ** -0.5)
    qb_o[...] = _silu(proj(_C_QB, W_HGRN))
    l0 = lbl_ref[0:1, :]
    l1 = lbl_ref[1:2, :]
    mx = jnp.maximum(l0, l1)
    e0 = jnp.exp(l0 - mx)
    e1 = jnp.exp(l1 - mx)
    lb = e0 / (e0 + e1)
    f_o[...] = lb + (1.0 - lb) * _sigmoid(proj(_C_FB, W_HGRN))
    vb_o[...] = proj(_C_IB, W_HGRN)
    sg_o[...] = _silu(proj(_C_GB, W_HGRN))


def _mix(x, mod, g, wpack, gq, gk, lbl, bd, tm, rows_per_group):
    n = x.shape[0]
    tpg = rows_per_group // tm
    row = lambda w: pl.BlockSpec((tm, w), lambda i: (i, 0))
    full = lambda a: pl.BlockSpec(a.shape, lambda i: (0,) * a.ndim)
    featmajor = pl.BlockSpec((None, W_ATTN, tm), lambda i: (i // tpg, 0, i % tpg))
    outs = [(W_ATTN, BF16), None, (W_ATTN, BF16), None, None,
            (H_I * D_I, BF16), (LANES, F32), (LANES, BF16), (LANES, F32),
            (W_HGRN, F32), (W_HGRN, F32), (W_HGRN, F32), (W_HGRN, F32)]
    fm_dtypes = {1: F32, 3: F32, 4: BF16}
    return pl.pallas_call(
        _mix_kernel,
        out_shape=[jax.ShapeDtypeStruct((n // rows_per_group, W_ATTN, rows_per_group), fm_dtypes[k]) if o is None
                   else jax.ShapeDtypeStruct((n, o[0]), o[1]) for k, o in enumerate(outs)],
        grid=(n // tm,),
        in_specs=[row(D_MODEL), mod.spec(3, tm, 1), mod.spec(4, tm, 1), full(g), full(wpack),
                  full(gq), full(gk), full(lbl), full(bd)],
        out_specs=[featmajor if o is None else row(o[0]) for o in outs],
        compiler_params=_cp(("arbitrary",)),
        name="mix",
    )(x, mod.arr, mod.arr, g, wpack, gq, gk, lbl, bd)


def _t5_bucket(dist):
    n = jnp.maximum(dist, 0)
    max_exact = T5_BUCKETS // 2
    nf = jnp.maximum(n, 1).astype(F32)
    large = max_exact + (jnp.log(nf / max_exact) / math.log(T5_MAX_DIST / max_exact)
                         * (T5_BUCKETS - max_exact)).astype(I32)
    large = jnp.minimum(large, T5_BUCKETS - 1)
    return jnp.where(n < max_exact, n, large)


def _bias_kernel(rb_ref, bp_ref, bs_ref):
    def lookup(bucket, h):
        val = jnp.zeros(bucket.shape, F32)
        for b in range(T5_BUCKETS):
            val = jnp.where(bucket == b, rb_ref[b, h], val)
        return val

    k = lax.broadcasted_iota(I32, (2 * LANES, LANES), 0)
    q = lax.broadcasted_iota(I32, (2 * LANES, LANES), 1)
    far_bucket = _t5_bucket(jnp.full((2 * LANES, LANES), 2 * LANES, I32))
    for v, dist in enumerate((q - k, LANES + q - k)):
        bucket = _t5_bucket(dist)
        for h in range(H_A):
            bp_ref[v, h // 2, :, (h % 2) * LANES:(h % 2 + 1) * LANES] = lookup(bucket, h) - lookup(far_bucket, h)

    t = lax.broadcasted_iota(I32, (SUBLANES, LANES), 0)
    u = lax.broadcasted_iota(I32, (SUBLANES, LANES), 1)
    sdists = (PAGE + t - u, t - u, jnp.full((SUBLANES, LANES), 2 * LANES, I32))
    for m in range(3):
        bucket = _t5_bucket(sdists[m])
        for h in range(H_A):
            bs_ref[m, h * SUBLANES:(h + 1) * SUBLANES, :] = lookup(bucket, h)


def _bias_tiles(rel_bias):
    return pl.pallas_call(
        _bias_kernel,
        out_shape=[jax.ShapeDtypeStruct((2, H_A // 2, 2 * LANES, 2 * LANES), F32),
                   jax.ShapeDtypeStruct((3, H_A * SUBLANES, LANES), F32)],
        in_specs=[pl.BlockSpec(memory_space=pltpu.SMEM)],
        out_specs=[pl.BlockSpec(memory_space=pltpu.VMEM), pl.BlockSpec(memory_space=pltpu.VMEM)],
        name="t5_bias",
    )(rel_bias)


def _sortable_key(score):
    bits = pltpu.bitcast(score, I32)
    return jnp.where(bits < 0, INT_MIN - bits, bits)


def _tree_reduce(x, op):
    while x.shape[0] > 1:
        half = x.shape[0] // 2
        y = op(x[:half], x[half:2 * half])
        x = y if x.shape[0] % 2 == 0 else jnp.concatenate([y, x[2 * half:]], axis=0)
    return x[0]


def _sublane_allreduce(x, op):
    for sh in (4, 2, 1):
        x = op(x, pltpu.roll(x, sh, 0))
    return x


def _topk_maskbias_t(keys_ref, maskb_ref, far_ref, nch, cw, n_sel, idx_bits, far_limit):
    ng = cw // SUBLANES
    sub = lax.broadcasted_iota(I32, (ng, SUBLANES, LANES), 0) * SUBLANES + lax.broadcasted_iota(
        I32, (ng, SUBLANES, LANES), 1)

    def count(indicator):
        def body(c, acc):
            k0 = pl.multiple_of(c * cw, cw)
            kk = keys_ref[pl.ds(k0, cw), :].reshape(ng, SUBLANES, LANES)
            return acc + _tree_reduce(indicator(kk, k0), jnp.add)
        acc = lax.fori_loop(0, nch, body, jnp.zeros((SUBLANES, LANES), F32))
        return _sublane_allreduce(acc, jnp.add)

    def bisect(it, ans):
        cand = ans | jnp.left_shift(jnp.int32(1), 31 - it)
        cs = cand ^ INT_MIN
        cnt = count(lambda kk, k0: jnp.where(kk >= cs[None], 1.0, 0.0))
        return jnp.where(cnt >= n_sel, cand, ans)

    thr = lax.fori_loop(0, 32, bisect, jnp.zeros((SUBLANES, LANES), I32)) ^ INT_MIN
    need = n_sel - count(lambda kk, k0: jnp.where(kk > thr[None], 1.0, 0.0))
    n_eq = count(lambda kk, k0: jnp.where(kk == thr[None], 1.0, 0.0))
    has_tie = jnp.max(jnp.where(thr == INT_MIN, 0.0, jnp.where(n_eq > need, 1.0, 0.0))) > 0.0

    def tie_search(_):
        def step(it, aj):
            cand = aj | jnp.left_shift(jnp.int32(1), idx_bits - 1 - it)
            cnt = count(lambda kk, k0: jnp.where(kk == thr[None], jnp.where((k0 + sub) < cand[None], 1.0, 0.0), 0.0))
            return jnp.where(cnt < need, cand, aj)
        return lax.fori_loop(0, idx_bits, step, jnp.zeros((SUBLANES, LANES), I32))

    cut = lax.cond(has_tie, tie_search, lambda _: jnp.full((SUBLANES, LANES), 2 ** 30, I32), 0)
    cut = jnp.where(thr == INT_MIN, -1, cut)

    def fin(c, _):
        k0 = pl.multiple_of(c * cw, cw)
        kk = keys_ref[pl.ds(k0, cw), :].reshape(ng, SUBLANES, LANES)
        kpos = k0 + sub
        tie = jnp.where(kpos <= cut[None], 0.0, NEG)
        mb = jnp.where(kk > thr[None], 0.0, jnp.where(kk == thr[None], tie, NEG))
        maskb_ref[pl.ds(k0, cw), :] = mb.reshape(cw, LANES)
        if far_ref is not None:
            far_ref[pl.ds(k0, cw), :] = jnp.where(kpos < far_limit, mb, NEG).reshape(cw, LANES)
        return 0

    lax.fori_loop(0, nch, fin, 0)


def _softmax_step(carry, s, v16, v_is_transposed=False):
    m, l, acc = carry
    mn = jnp.maximum(m, jnp.max(s, axis=1, keepdims=True))
    alpha = jnp.exp(m - mn)
    p = jnp.exp(s - mn)
    l = alpha * l + jnp.sum(p, axis=1, keepdims=True)
    if v_is_transposed:
        pv = lax.dot_general(p.astype(BF16), v16, _NT, preferred_element_type=F32)
    else:
        pv = jnp.dot(p.astype(BF16), v16, preferred_element_type=F32)
    return mn, l, alpha * acc + pv


_QB = 128
_KC = 512
_FW = 512


def _prompt_attn_step(step, nsteps, i, qa_ref, qi_ref, wi_ref, ki_ref, ka_ref, vat_ref, bias_ref, o_ref,
                      keys_sc, maskb_sc, far_sc, qs_sc, q2_sc, m_sc, l_sc, acc_sc):
    lane = lax.broadcasted_iota(I32, (_QB, LANES), 1)
    lo = lane < D_I
    nch = i // (_KC // _QB) + 1
    near0 = jnp.maximum(i - 1, 0) * _QB
    npair = H_A // 2
    cols2 = 2 * _QB

    @pl.when(step == 0)
    def _():
        for h in range(H_I):
            pair = qi_ref[:, (h // 2) * LANES:(h // 2 + 1) * LANES].astype(F32)
            keep = jnp.where(lo, pair, 0.0) if h % 2 == 0 else jnp.where(lo, 0.0, pair)
            qs_sc[h * _QB:(h + 1) * _QB, :] = keep.astype(BF16)

        qpos = i * _QB + lax.broadcasted_iota(I32, (_KC, LANES), 1)

        def scores(c, _):
            k0 = pl.multiple_of(c * _KC, _KC)
            s = lax.dot_general(ki_ref[pl.ds(k0, _KC), :], qs_sc[...], _NT, preferred_element_type=F32)
            sc = jnp.zeros((_KC, LANES), F32)
            for h in range(H_I):
                sc = sc + jnp.maximum(s[:, h * _QB:(h + 1) * _QB], 0.0) * wi_ref[h:h + 1, :]
            kpos = k0 + lax.broadcasted_iota(I32, (_KC, LANES), 0)
            keys_sc[pl.ds(k0, _KC), :] = jnp.where(kpos <= qpos, _sortable_key(sc), INT_MIN)
            return 0

        lax.fori_loop(0, nch, scores, 0)
        _topk_maskbias_t(keys_sc, maskb_sc, far_sc, nch, _KC, TOPK_MAX, 12, near0)

        for p in range(npair):
            qp = qa_ref[:, p * LANES:(p + 1) * LANES].astype(F32)
            q2_sc[p * cols2:(p + 1) * cols2, :] = jnp.concatenate(
                [jnp.where(lo, qp, 0.0), jnp.where(lo, 0.0, qp)], axis=0).astype(BF16)
        m_sc[...] = jnp.full(m_sc.shape, NEG, F32)
        l_sc[...] = jnp.zeros_like(l_sc)
        acc_sc[...] = jnp.zeros_like(acc_sc)

    def update_all(k0, width, mask_ref, biases):
        mb = mask_ref[pl.ds(k0, width), :]
        mb2 = jnp.concatenate([mb, mb], axis=1)

        def qk(p):
            s = lax.dot_general(ka_ref[pl.ds(k0, width), p * LANES:(p + 1) * LANES],
                                q2_sc[p * cols2:(p + 1) * cols2, :], _NT, preferred_element_type=F32) + mb2
            if biases is not None:
                s = s + biases[p]
            return s.reshape(width // SUBLANES, SUBLANES, cols2)

        def softmax(p, s):
            m_old = m_sc[p]
            m_new = jnp.maximum(m_old, _sublane_allreduce(_tree_reduce(s, jnp.maximum), jnp.maximum))
            alpha = jnp.exp(m_old - m_new)
            pexp = jnp.exp(s - m_new[None])
            m_sc[p] = m_new
            l_sc[p] = alpha * l_sc[p] + _tree_reduce(pexp, jnp.add)
            return pexp.reshape(width, cols2).astype(BF16), alpha

        def pv(p, p16, alpha):
            out = jnp.dot(vat_ref[p * LANES:(p + 1) * LANES, pl.ds(k0, width)], p16,
                          preferred_element_type=F32)
            acc = acc_sc[p].reshape(LANES // SUBLANES, SUBLANES, cols2) * alpha[None]
            acc_sc[p] = acc.reshape(LANES, cols2) + out

        logits = [qk(p) for p in range(npair)]
        probs = [softmax(p, logits[p]) for p in range(npair)]
        for p in range(npair):
            pv(p, *probs[p])

    def far(j, _):
        update_all(pl.multiple_of((step + j * nsteps) * _FW, _FW), _FW, far_sc, None)
        return 0

    nfar = (near0 + _FW - 1) // _FW
    lax.fori_loop(0, (nfar - step + nsteps - 1) // nsteps, far, 0)

    @pl.when(step == nsteps - 1)
    def _():
        first = jnp.minimum(i, 1)
        update_all(pl.multiple_of(near0, _QB), 2 * _QB, maskb_sc, [bias_ref[first, p] for p in range(npair)])
        for p in range(npair):
            l = _sublane_allreduce(l_sc[p], jnp.add)
            o = (acc_sc[p].reshape(LANES // SUBLANES, SUBLANES, cols2) / l[None]).reshape(LANES, cols2)
            ot = jnp.concatenate([o[:D_HA, :_QB], o[D_HA:, _QB:]], axis=0)
            o_ref[:, p * LANES:(p + 1) * LANES] = ot.T


_PPS = 16
_TP = SUBLANES
_TK = 2 * SUBLANES


class _PageStream:
    def __init__(self, pt_ref, hbm_refs, bufs, sems, pages_per_group):
        self.pt_ref, self.hbm_refs, self.bufs, self.sems, self.ppg = pt_ref, hbm_refs, bufs, sems, pages_per_group

    def _copies(self, seq, group, slot):
        for hbm, buf, sem in zip(self.hbm_refs, self.bufs, self.sems):
            for p in range(self.ppg):
                yield pltpu.make_async_copy(hbm.at[self.pt_ref[seq, group * self.ppg + p]], buf.at[slot, p],
                                            sem.at[slot])

    def start(self, seq, group, slot):
        for cp in self._copies(seq, group, slot):
            cp.start()

    def wait(self, seq, group, slot):
        for cp in self._copies(seq, group, slot):
            cp.wait()

    def advance(self, step, nsteps, groups_per_seq):
        slot = lax.rem(step, 2)

        @pl.when(step == 0)
        def _():
            self.start(0, 0, 0)

        @pl.when(step + 1 < nsteps)
        def _():
            nxt = step + 1
            self.start(nxt // groups_per_seq, lax.rem(nxt, groups_per_seq), 1 - slot)

        self.wait(step // groups_per_seq, lax.rem(step, groups_per_seq), slot)
        return slot


def _sample_scores_kernel(pt_ref, q_ref, w_ref, kn_ref, cki_ref, o_ref, buf, sem):
    n_pages = buf.shape[1]
    slot = _PageStream(pt_ref, [cki_ref], [buf], [sem], n_pages).advance(pl.program_id(0), pl.num_programs(0), 1)
    q = q_ref[...]

    def head_sum(s):
        n = s.shape[1]
        out = jnp.zeros((_TP, n), F32)
        for h in range(H_I):
            wh = w_ref[h * _TP:(h + 1) * _TP, :]
            out = out + jnp.maximum(s[h * _TP:(h + 1) * _TP, :], 0.0) * jnp.concatenate([wh] * (n // LANES), axis=1)
        return out

    for c in range(n_pages // _PPS):
        kc = jnp.concatenate([buf[slot, c * _PPS + p].astype(BF16) for p in range(_PPS)], axis=1)
        s = jnp.dot(q, kc, preferred_element_type=F32)
        o_ref[:, c * _PPS * PAGE:(c + 1) * _PPS * PAGE] = head_sum(s)

    kn = jnp.concatenate([kn_ref[...], jnp.zeros((LANES - _TK, D_I), BF16)], axis=0)
    sn = lax.dot_general(q, kn, _NT, preferred_element_type=F32)
    o_ref[:, o_ref.shape[1] - LANES:] = head_sum(sn)


def _sample_scores(page_table, q_st, w_b, ki_new, cache_kidx, n_pages):
    nseq = q_st.shape[0]
    lpad = n_pages * PAGE + LANES
    return pl.pallas_call(
        _sample_scores_kernel,
        out_shape=jax.ShapeDtypeStruct((nseq, _TP, lpad), F32),
        grid_spec=pltpu.PrefetchScalarGridSpec(
            num_scalar_prefetch=1, grid=(nseq,),
            in_specs=[pl.BlockSpec((None, H_I * _TP, D_I), lambda b, pt: (b, 0, 0)),
                      pl.BlockSpec((None, H_I * _TP, LANES), lambda b, pt: (b, 0, 0)),
                      pl.BlockSpec((None, _TK, D_I), lambda b, pt: (b, 0, 0)),
                      pl.BlockSpec(memory_space=pl.ANY)],
            out_specs=pl.BlockSpec((None, _TP, lpad), lambda b, pt: (b, 0, 0)),
            scratch_shapes=[pltpu.VMEM((2, n_pages, D_I, PAGE), F32), pltpu.SemaphoreType.DMA((2,))]),
        compiler_params=_cp(("arbitrary",)),
        name="sample_scores",
    )(page_table, q_st, w_b, ki_new, cache_kidx)


_SEL_CW = 640


def _sample_select_kernel(s_ref, o_ref, keys_sc, *, past, t_dec):
    lpad = s_ref.shape[0]
    t = lax.broadcasted_iota(I32, (_SEL_CW, LANES), 1) % _TP
    limit = jnp.where(t < t_dec, past + t, -1)
    for c in range(lpad // _SEL_CW):
        sl = slice(c * _SEL_CW, (c + 1) * _SEL_CW)
        kpos = c * _SEL_CW + lax.broadcasted_iota(I32, (_SEL_CW, LANES), 0)
        keys_sc[sl, :] = jnp.where(kpos <= limit, _sortable_key(s_ref[sl, :]), INT_MIN)
    _topk_maskbias_t(keys_sc, o_ref, None, lpad // _SEL_CW, _SEL_CW, TOPK_MAX, 14, None)


def _sample_select(scores_t, past, t_dec):
    lpad, cols = scores_t.shape
    return pl.pallas_call(
        functools.partial(_sample_select_kernel, past=past, t_dec=t_dec),
        out_shape=jax.ShapeDtypeStruct((lpad, cols), F32),
        grid=(cols // LANES,),
        in_specs=[pl.BlockSpec((lpad, LANES), lambda g: (0, g))],
        out_specs=pl.BlockSpec((lpad, LANES), lambda g: (0, g)),
        scratch_shapes=[pltpu.VMEM((lpad, LANES), I32)],
        compiler_params=_cp(("arbitrary",)),
        name="sample_select",
    )(scores_t)


def _sample_attn_step(s_id, last, q_ref, mb_ref, kn_ref, vn_ref, bias_ref, kp, vp, o_ref,
                      qblk_sc, m_sc, l_sc, acc_sc):
    rows = H_A * _TP
    span = _PPS * PAGE

    @pl.when(s_id == 0)
    def _():
        q = jnp.concatenate([q_ref[...]] * H_A, axis=0)
        r = lax.broadcasted_iota(I32, (rows, W_ATTN), 0) // _TP
        c = lax.broadcasted_iota(I32, (rows, W_ATTN), 1) // D_HA
        qblk_sc[...] = jnp.where(r == c, q, 0.0).astype(BF16)
        m_sc[...] = jnp.full(m_sc.shape, NEG, F32)
        l_sc[...] = jnp.zeros_like(l_sc)
        acc_sc[...] = jnp.zeros_like(acc_sc)

    def update(s, v16, v_is_transposed):
        carry = (m_sc[:, 0:1], l_sc[:, 0:1], acc_sc[...])
        m, l, acc = _softmax_step(carry, s, v16, v_is_transposed)
        m_sc[...] = jnp.broadcast_to(m, m_sc.shape)
        l_sc[...] = jnp.broadcast_to(l, l_sc.shape)
        acc_sc[...] = acc

    k16 = jnp.concatenate([pg[...].astype(BF16) for pg in kp], axis=1)
    v16 = jnp.concatenate([pg[...].astype(BF16) for pg in vp], axis=1)
    s = jnp.dot(qblk_sc[...], k16, preferred_element_type=F32)
    mb = mb_ref[:, pl.ds(pl.multiple_of(s_id * span, span), span)]
    last_tile = jnp.where(s_id == last, bias_ref[0], bias_ref[2])
    bias = jnp.concatenate([bias_ref[2]] * (_PPS - 1) + [last_tile], axis=1)
    update(s + jnp.concatenate([mb] * H_A, axis=0) + bias, v16, True)

    @pl.when(s_id == last)
    def _():
        pad = jnp.zeros((LANES - _TK, W_ATTN), BF16)
        kn = jnp.concatenate([kn_ref[...], pad], axis=0)
        vn = jnp.concatenate([vn_ref[...], pad], axis=0)
        sn = lax.dot_general(qblk_sc[...], kn, _NT, preferred_element_type=F32)
        mbn = mb_ref[:, mb_ref.shape[1] - LANES:]
        update(sn + jnp.concatenate([mbn] * H_A, axis=0) + bias_ref[1], vn, False)
        o = acc_sc[...] / l_sc[:, 0:1]
        c = lax.broadcasted_iota(I32, (_TP, W_ATTN), 1) // D_HA
        out = jnp.zeros((_TP, W_ATTN), F32)
        for h in range(H_A):
            out = jnp.where(c == h, o[h * _TP:(h + 1) * _TP, :], out)
        o_ref[...] = out


def _attn_kernel(pt_ref, qa_ref, qi_ref, wi_ref, ki_ref, ka_ref, vat_ref, bias_p_ref,
                 sq_ref, smb_ref, skn_ref, svn_ref, bias_s_ref, ck_ref, cv_ref, o_ref, so_ref, *scratch,
                 nq, nsteps, nseq):
    prompt_scratch, sample_scratch = scratch[:8], scratch[8:12]
    kbuf, vbuf, ksem, vsem = scratch[12:]
    step = pl.program_id(2)
    seq_id = pl.program_id(0) * nq + pl.program_id(1)
    stream = _PageStream(pt_ref, [ck_ref, cv_ref], [kbuf, vbuf], [ksem, vsem], _PPS)
    slot = stream.advance(seq_id * nsteps + step, nseq * nsteps, nsteps)
    _prompt_attn_step(step, nsteps, pl.program_id(1), qa_ref, qi_ref, wi_ref, ki_ref, ka_ref, vat_ref, bias_p_ref,
                      o_ref, *prompt_scratch)
    _sample_attn_step(step, nsteps - 1, sq_ref, smb_ref, skn_ref, svn_ref, bias_s_ref,
                      [kbuf.at[slot, p] for p in range(_PPS)], [vbuf.at[slot, p] for p in range(_PPS)], so_ref,
                      *sample_scratch)


def _attention(qa16, qi16, wi_t, ki16, ka16, va16_t, bias_p, nbatch, seq,
               page_table, q8, maskb, k_new, v_new, bias_s, cache_k, cache_v, n_pages):
    nq = seq // _QB
    nseq = q8.shape[0]
    steps = n_pages // _PPS
    lpad = maskb.shape[1]
    npair = H_A // 2
    assert nseq == nbatch * nq, "one decode sequence per prompt query block"
    blk = lambda w: pl.BlockSpec((_QB, w), lambda b, i, s, pt: (b * nq + i, 0))
    once = pl.Buffered(1)
    per_b = lambda w: pl.BlockSpec((seq, w), lambda b, i, s, pt: (b, 0), pipeline_mode=once)
    seq3 = lambda t: pl.BlockSpec((None, t, W_ATTN), lambda b, i, s, pt: (b * nq + i, 0, 0))
    hbm = pl.BlockSpec(memory_space=pl.ANY)
    return pl.pallas_call(
        functools.partial(_attn_kernel, nq=nq, nsteps=steps, nseq=nseq),
        out_shape=[jax.ShapeDtypeStruct((nbatch * seq, W_ATTN), F32),
                   jax.ShapeDtypeStruct((nseq, _TP, W_ATTN), F32)],
        grid_spec=pltpu.PrefetchScalarGridSpec(
            num_scalar_prefetch=1, grid=(nbatch, nq, steps),
            in_specs=[blk(W_ATTN), blk(H_I * D_I),
                      pl.BlockSpec((H_I, _QB), lambda b, i, s, pt: (0, b * nq + i)),
                      per_b(LANES), per_b(W_ATTN),
                      pl.BlockSpec((None, W_ATTN, seq), lambda b, i, s, pt: (b, 0, 0), pipeline_mode=once),
                      pl.BlockSpec(bias_p.shape, lambda b, i, s, pt: (0, 0, 0, 0), pipeline_mode=once),
                      seq3(_TP), pl.BlockSpec((_TP, lpad), lambda b, i, s, pt: (b * nq + i, 0)),
                      seq3(_TK), seq3(_TK),
                      pl.BlockSpec(bias_s.shape, lambda b, i, s, pt: (0, 0, 0), pipeline_mode=once),
                      hbm, hbm],
            out_specs=[blk(W_ATTN), seq3(_TP)],
            scratch_shapes=[pltpu.VMEM((seq, LANES), I32), pltpu.VMEM((seq, LANES), F32),
                            pltpu.VMEM((seq, LANES), F32),
                            pltpu.VMEM((H_I * _QB, LANES), BF16), pltpu.VMEM((H_A * _QB, LANES), BF16),
                            pltpu.VMEM((npair, SUBLANES, 2 * _QB), F32), pltpu.VMEM((npair, SUBLANES, 2 * _QB), F32),
                            pltpu.VMEM((npair, LANES, 2 * _QB), F32),
                            pltpu.VMEM((H_A * _TP, W_ATTN), BF16), pltpu.VMEM((H_A * _TP, LANES), F32),
                            pltpu.VMEM((H_A * _TP, LANES), F32), pltpu.VMEM((H_A * _TP, W_ATTN), F32),
                            pltpu.VMEM((2, _PPS, W_ATTN, PAGE), F32), pltpu.VMEM((2, _PPS, W_ATTN, PAGE), F32),
                            pltpu.SemaphoreType.DMA((2,)), pltpu.SemaphoreType.DMA((2,))]),
        compiler_params=_cp(("arbitrary", "arbitrary", "arbitrary")),
        name="attention",
    )(page_table, qa16, qi16, wi_t, ki16, ka16, va16_t, bias_p, q8, maskb, k_new, v_new, bias_s, cache_k, cache_v)


def _hgrn_chunk(q, k, v, logf, st, c):
    blk = SUBLANES // 2 if c > SUBLANES else SUBLANES
    row = lax.broadcasted_iota(I32, (c, DK_B), 0)
    g = logf
    sh = 1
    while sh < c:
        g = g + jnp.where(row >= sh, pltpu.roll(g, sh, 0), 0.0)
        sh *= 2

    o = lax.dot_general((q * jnp.exp(g)).astype(BF16), st.astype(BF16), _NT, preferred_element_type=F32)

    if c > SUBLANES:
        rr = lax.broadcasted_iota(I32, (c, c), 0)
        cc = lax.broadcasted_iota(I32, (c, c), 1)
        a = jnp.zeros((c, c), F32)
        w = blk
        while w < c:
            nb = c // (2 * w)
            gb = jnp.broadcast_to(g.reshape(nb, 2 * w, DK_B)[:, w - 1:w, :], (nb, 2 * w, DK_B)).reshape(c, DK_B)
            right = (row % (2 * w)) >= w
            qt = jnp.where(right, q * jnp.exp(jnp.where(right, g - gb, 0.0)), 0.0)
            kt = jnp.where(right, 0.0, k * jnp.exp(jnp.where(right, 0.0, gb - g)))
            aw = lax.dot_general(qt.astype(BF16), kt.astype(BF16), _NT, preferred_element_type=F32)
            a = a + jnp.where((rr // (2 * w)) == (cc // (2 * w)), aw, 0.0)
            w *= 2
        o = o + jnp.dot(a.astype(BF16), v.astype(BF16), preferred_element_type=F32)

    o = o + jnp.sum(q * k, axis=1, keepdims=True) * v
    for d in range(1, blk):
        ok = (row % blk) >= d
        e = jnp.exp(jnp.where(ok, g - pltpu.roll(g, d, 0), 0.0))
        coef = jnp.sum(jnp.where(ok, q * pltpu.roll(k, d, 0) * e, 0.0), axis=1, keepdims=True)
        o = o + coef * pltpu.roll(v, d, 0)

    g_end = g[c - 1:c, :]
    kd = k * jnp.exp(g_end - g)
    st = st * jnp.exp(g_end) + lax.dot_general(v.astype(BF16), kd.astype(BF16), _TN, preferred_element_type=F32)
    return o, st


_HT = 1024


def _hgrn_kernel(q_ref, f_ref, v_ref, s0_ref, o_ref, s_ref, st_sc, *, nb, tb, c):
    tstep = pl.program_id(1)

    @pl.when(tstep == 0)
    def _():
        for j in range(nb):
            for g in range(H_B):
                st_sc[j, g] = s0_ref[j, g].T

    for j in range(nb):
        def body(ci, states):
            r = pl.ds(pl.multiple_of(ci * c, c), c)
            new_states = []
            for g in range(H_B):
                cs = slice(g * DK_B, (g + 1) * DK_B)
                f = f_ref[j, r, cs]
                o, st = _hgrn_chunk(q_ref[j, r, cs], 1.0 - f, v_ref[j, r, cs], jnp.log(f), states[g], c)
                o_ref[j, r, cs] = o
                new_states.append(st)
            return tuple(new_states)
        states = lax.fori_loop(0, tb // c, body, tuple(st_sc[j, g] for g in range(H_B)))
        for g in range(H_B):
            st_sc[j, g] = states[g]

    @pl.when(tstep == pl.num_programs(1) - 1)
    def _():
        for j in range(nb):
            for g in range(H_B):
                s_ref[j, g] = st_sc[j, g].T


def _hgrn(q, f, v, s0, nb, c):
    nseq, t_len, _ = q.shape
    tb = min(t_len, _HT)
    tok = pl.BlockSpec((nb, tb, W_HGRN), lambda b, t: (b, t, 0))
    st = pl.BlockSpec((nb, H_B, DK_B, DV_B), lambda b, t: (b, 0, 0, 0))
    return pl.pallas_call(
        functools.partial(_hgrn_kernel, nb=nb, tb=tb, c=c),
        out_shape=[jax.ShapeDtypeStruct((nseq, t_len, W_HGRN), F32),
                   jax.ShapeDtypeStruct((nseq, H_B, DK_B, DV_B), F32)],
        grid=(nseq // nb, t_len // tb),
        in_specs=[tok, tok, tok, st],
        out_specs=[tok, st],
        scratch_shapes=[pltpu.VMEM((nb, H_B, DV_B, DK_B), F32)],
        compiler_params=_cp(("arbitrary", "arbitrary")),
        name="hgrn",
    )(q, f, v, s0)


def _pad_tokens(a, value=0.0, to=_TP):
    return jnp.pad(a, ((0, 0), (0, to - a.shape[1]), (0, 0)), constant_values=value)


def kernel(x_prompt, x_sample, cache_k, cache_v, cache_kidx, state_hgrn, page_table, c_prompt, c_sample,
           w_ada, b_ada, g_ffn1, ffn1_w_gate, ffn1_w_up, ffn1_w_down, g_mix, w_in, g_q, g_k,
           beta_attn, g_hgrn, w_out, g_ffn2, ffn2_w_gate, ffn2_w_up, ffn2_w_down, rel_bias, lb_logits):
    assert w_ada.shape[0] == 1, "single-layer problem"
    nb_p, seq, _ = x_prompt.shape
    nb_s, t_dec, _ = x_sample.shape
    n_phys = cache_k.shape[1]
    n_pages = page_table.shape[1]
    past = n_pages * PAGE
    assert t_dec <= _TP and n_pages % _PPS == 0 and (past + LANES) % _SEL_CW == 0

    row2 = lambda a: a.reshape(1, -1)
    w_in0 = w_in[0]
    pad_w = jnp.zeros((D_MODEL, LANES - H_I), F32)
    wpack = jnp.concatenate(
        [w_in0[:, :2048], w_in0[:, 2048:2112], w_in0[:, 2048:2112], w_in0[:, 2112:2120], pad_w,
         w_in0[:, 2120:]], axis=1).astype(BF16)
    assert wpack.shape[1] == _N_PACK
    gq = row2(jnp.tile(g_q[0], H_A))
    gk = row2(jnp.tile(g_k[0], H_A))
    gidx = jnp.arange(W_ATTN) // D_HA
    bd = jnp.where(gidx[:, None] == gidx[None, :], 1.0 / D_HA, 0.0).astype(BF16)
    ffn1 = (row2(g_ffn1[0]), ffn1_w_gate[0].astype(BF16), ffn1_w_up[0].astype(BF16), ffn1_w_down[0].astype(BF16))
    ffn2 = (row2(g_ffn2[0]), ffn2_w_gate[0].astype(BF16), ffn2_w_up[0].astype(BF16), ffn2_w_down[0].astype(BF16))
    w_out16 = w_out[0].astype(BF16)

    n_c = nb_p + nb_s
    c_pad = (-n_c) % SUBLANES
    c_all = jnp.concatenate([c_prompt, c_sample, jnp.zeros((c_pad, D_MODEL), F32)], axis=0)
    mods = _ada(c_all, w_ada[0].astype(BF16), row2(b_ada[0]))
    bias_p, bias_s = _bias_tiles(rel_bias)

    def front(x, mod, tm, tf, rows_per_group):
        x1 = _ffn(x, mod, 0, *ffn1, tm, tf)
        return x1, _mix(x1, mod, row2(g_mix[0]), wpack, gq, gk, lb_logits, bd, tm, rows_per_group)

    def back(x1, mod, tm, tf, oa, oh, sg):
        return _merge_ffn(oa, oh, sg, x1, mod, row2(beta_attn[0]), row2(g_hgrn[0]), bd, w_out16, *ffn2, tm, tf)

    tm_p, tf = 512, D_FF // 2
    n_tok = nb_s * t_dec
    mod_p = _Mod(mods[:, :nb_p].reshape(N_MOD, nb_p, 1, D_MODEL), False, seq // tm_p)
    mod_s = _Mod(jnp.repeat(mods[:, nb_p:nb_p + nb_s], t_dec, axis=1), True)
    x1p, proj_p = front(x_prompt.reshape(nb_p * seq, D_MODEL), mod_p, tm_p, tf, seq)
    x1s, proj_s = front(x_sample.reshape(n_tok, D_MODEL), mod_s, n_tok, tf, n_tok)
    qa16, kp_t, ka16, vp_t, va16_t, qi16, ki_p, ki16, wi, qb, f, vb, sg_p = proj_p
    qa16s, ks_t, ka16s, vs_t, va16s_t, qi16s, ki_s, ki16s, wis, qbs, fs, vbs, sg_s = proj_s
    va16s = va16s_t[0].T

    ck = jnp.transpose(cache_k[0], (0, 2, 3, 1)).reshape(n_phys, W_ATTN, PAGE)
    cv = jnp.transpose(cache_v[0], (0, 2, 3, 1)).reshape(n_phys, W_ATTN, PAGE)
    cki = jnp.transpose(cache_kidx[0], (0, 2, 1))
    seq3 = lambda a: a.reshape(nb_s, t_dec, a.shape[-1])
    q_st = _pad_tokens(seq3(qi16s)).reshape(nb_s, _TP, H_I, D_I).transpose(0, 2, 1, 3).reshape(nb_s, H_I * _TP, D_I)
    w_st = _pad_tokens(seq3(wis[:, :H_I])).transpose(0, 2, 1).reshape(nb_s, H_I * _TP, 1)
    w_b = jnp.broadcast_to(w_st, (nb_s, H_I * _TP, LANES))
    scores = _sample_scores(page_table, q_st, w_b, _pad_tokens(seq3(ki16s[:, :D_I]), to=_TK), cki, n_pages)
    maskb = _sample_select(scores.reshape(nb_s * _TP, -1).T, past, t_dec).T

    oa_p, oa8 = _attention(qa16, qi16, wi[:, :H_I].T, ki16, ka16, va16_t, bias_p, nb_p, seq,
                           page_table, _pad_tokens(seq3(qa16s.astype(F32))), maskb,
                           _pad_tokens(seq3(ka16s), to=_TK), _pad_tokens(seq3(va16s), to=_TK), bias_s, ck, cv, n_pages)
    r3 = lambda a: a.reshape(nb_p, seq, W_HGRN)
    oh_p, sp = _hgrn(r3(qb), r3(f), r3(vb), jnp.zeros((nb_p, H_B, DK_B, DV_B), F32), 1, 64)
    oh8, ss = _hgrn(_pad_tokens(seq3(qbs)), _pad_tokens(seq3(fs), 1.0), _pad_tokens(seq3(vbs)), state_hgrn[0], 8, _TP)

    yp = back(x1p, mod_p, tm_p, tf, oa_p, oh_p.reshape(nb_p * seq, W_HGRN), sg_p)
    ys = back(x1s, mod_s, n_tok, tf, oa8[:, :t_dec].reshape(n_tok, W_ATTN),
              oh8[:, :t_dec].reshape(n_tok, W_HGRN), sg_s)

    def rows_out(a_t, nseq, t_len):
        a = a_t.reshape(-1, H_A, D_HA, a_t.shape[-1]).transpose(0, 3, 1, 2)
        return a.reshape(1, nseq, t_len, H_A, D_HA)

    return (yp.reshape(nb_p, seq, D_MODEL).astype(x_prompt.dtype),
            ys.reshape(nb_s, t_dec, D_MODEL).astype(x_sample.dtype),
            rows_out(kp_t, nb_p, seq), rows_out(vp_t, nb_p, seq),
            ki_p[:, :D_I].reshape(1, nb_p, seq, D_I), sp[None],
            rows_out(ks_t, nb_s, t_dec), rows_out(vs_t, nb_s, t_dec),
            ki_s[:, :D_I].reshape(1, nb_s, t_dec, D_I), ss[None])
```

```python
import functools
import math

import jax
import jax.numpy as jnp
from jax import lax
from jax.experimental import pallas as pl
from jax.experimental.pallas import tpu as pltpu

F32 = jnp.float32
BF16 = jnp.bfloat16
I32 = jnp.int32

D_MODEL = 1024
W_ATTN = 512
W_HGRN = 512
D_HA = 64
H_A = 8
H_I = 8
D_I = 64
TOPK_MAX = 256
H_B = 4
DK_B = 128
DV_B = 128
D_FF = 2816
T5_BUCKETS = 32
T5_MAX_DIST = 128
RMS_EPS = 1e-6
N_MOD = 9
PAGE = 128

LANES = 128
SUBLANES = 8
NEG = -1e30
INT_MIN = -(2 ** 31)
VMEM_LIMIT = 56 * 1024 * 1024

_NT = (((1,), (1,)), ((), ()))
_TN = (((0,), (0,)), ((), ()))


def _cp(sem):
    return pltpu.CompilerParams(dimension_semantics=sem, vmem_limit_bytes=VMEM_LIMIT)


def _silu(x):
    return x / (1.0 + jnp.exp(-x))


def _sigmoid(x):
    return 1.0 / (1.0 + jnp.exp(-x))


def _rms(x, g):
    return x * lax.rsqrt(jnp.mean(x * x, axis=-1, keepdims=True) + RMS_EPS) * g


def _group_mean_sq(x, bd):
    xx = x * x
    hi = xx.astype(BF16)
    lo = (xx - hi.astype(F32)).astype(BF16)
    return (jnp.dot(hi, bd, preferred_element_type=F32)
            + jnp.dot(lo, bd, preferred_element_type=F32))


def _ada_kernel(c_ref, w_ref, b_ref, o_ref):
    a = _silu(c_ref[...]).astype(BF16)
    o_ref[...] = jnp.dot(a, w_ref[...], preferred_element_type=F32) + b_ref[...]


def _ada(c_all, w16, b):
    rows = c_all.shape[0]
    return pl.pallas_call(
        _ada_kernel,
        out_shape=jax.ShapeDtypeStruct((N_MOD, rows, D_MODEL), F32),
        grid=(N_MOD,),
        in_specs=[pl.BlockSpec((rows, D_MODEL), lambda j: (0, 0)),
                  pl.BlockSpec((D_MODEL, D_MODEL), lambda j: (0, j)),
                  pl.BlockSpec((1, D_MODEL), lambda j: (0, j))],
        out_specs=pl.BlockSpec((None, rows, D_MODEL), lambda j: (j, 0, 0)),
        compiler_params=_cp(("arbitrary",)),
        name="ada",
    )(c_all, w16, b)


class _Mod:
    def __init__(self, arr, per_token, tiles_per_seq=1):
        self.arr = arr
        self.per_token = per_token
        self.tiles_per_seq = tiles_per_seq

    def spec(self, j, tm, ngrid):
        if self.per_token:
            if ngrid == 1:
                return pl.BlockSpec((None, tm, D_MODEL), lambda i: (j, i, 0))
            return pl.BlockSpec((None, tm, D_MODEL), lambda i, k: (j, i, 0))
        tps = self.tiles_per_seq
        if ngrid == 1:
            return pl.BlockSpec((None, None, 1, D_MODEL), lambda i: (j, i // tps, 0, 0))
        return pl.BlockSpec((None, None, 1, D_MODEL), lambda i, k: (j, i // tps, 0, 0))


def _ffn_steps(load_x, x_res, sh_ref, sc_ref, gt_ref, g_ref, wg_ref, wu_ref, wd_ref, o_ref, h_sc, acc_sc):
    k = pl.program_id(1)

    @pl.when(k == 0)
    def _():
        h = _rms(load_x(), g_ref[...]) * (1.0 + sc_ref[...]) + sh_ref[...]
        h_sc[...] = h.astype(BF16)
        acc_sc[...] = jnp.zeros_like(acc_sc)

    h = h_sc[...]
    a = jnp.dot(h, wg_ref[...], preferred_element_type=F32)
    b = jnp.dot(h, wu_ref[...], preferred_element_type=F32)
    act = (_silu(a) * b).astype(BF16)
    acc_sc[...] += jnp.dot(act, wd_ref[...], preferred_element_type=F32)

    @pl.when(k == pl.num_programs(1) - 1)
    def _():
        o_ref[...] = x_res[...] + 0.5 * gt_ref[...] * acc_sc[...]


def _ffn_kernel(x_ref, *rest):
    _ffn_steps(lambda: x_ref[...], x_ref, *rest)


def _merge_ffn_kernel(oa_ref, oh_ref, sg_ref, x_ref, gtm_ref, beta_ref, gh_ref, bd_ref, wo_ref, *rest):
    x_sc = rest[-1]

    def load_x():
        oa = oa_ref[...]
        a = oa * lax.rsqrt(_group_mean_sq(oa, bd_ref[...]) + RMS_EPS) * beta_ref[...]
        oh = oh_ref[...]
        parts = []
        for h in range(H_B):
            sl = slice(h * DV_B, (h + 1) * DV_B)
            parts.append(_rms(oh[:, sl], gh_ref[:, sl]))
        r = jnp.concatenate(parts, axis=1) * sg_ref[...]
        y = (jnp.dot(a.astype(BF16), wo_ref[:W_ATTN, :], preferred_element_type=F32)
             + jnp.dot(r.astype(BF16), wo_ref[W_ATTN:, :], preferred_element_type=F32))
        x_sc[...] = x_ref[...] + gtm_ref[...] * y
        return x_sc[...]

    _ffn_steps(load_x, x_sc, *rest[:-1])


def _ffn_specs(mod, j0, tm, tf):
    return [mod.spec(j0, tm, 2), mod.spec(j0 + 1, tm, 2), mod.spec(j0 + 2, tm, 2),
            pl.BlockSpec((1, D_MODEL), lambda i, k: (0, 0)),
            pl.BlockSpec((D_MODEL, tf), lambda i, k: (0, k)),
            pl.BlockSpec((D_MODEL, tf), lambda i, k: (0, k)),
            pl.BlockSpec((tf, D_MODEL), lambda i, k: (k, 0))]


def _ffn(x, mod, j0, g, wg, wu, wd, tm, tf):
    n = x.shape[0]
    row = pl.BlockSpec((tm, D_MODEL), lambda i, k: (i, 0))
    return pl.pallas_call(
        _ffn_kernel,
        out_shape=jax.ShapeDtypeStruct((n, D_MODEL), F32),
        grid=(n // tm, D_FF // tf),
        in_specs=[row] + _ffn_specs(mod, j0, tm, tf),
        out_specs=row,
        scratch_shapes=[pltpu.VMEM((tm, D_MODEL), BF16), pltpu.VMEM((tm, D_MODEL), F32)],
        compiler_params=_cp(("arbitrary", "arbitrary")),
        name="ffn",
    )(x, mod.arr, mod.arr, mod.arr, g, wg, wu, wd)


def _merge_ffn(oa, oh, sg, x, mod, beta, gh, bd, wo16, g, wg, wu, wd, tm, tf):
    n = x.shape[0]
    row = lambda w: pl.BlockSpec((tm, w), lambda i, k: (i, 0))
    full = lambda a: pl.BlockSpec(a.shape, lambda i, k: (0,) * a.ndim)
    return pl.pallas_call(
        _merge_ffn_kernel,
        out_shape=jax.ShapeDtypeStruct((n, D_MODEL), F32),
        grid=(n // tm, D_FF // tf),
        in_specs=[row(W_ATTN), row(W_HGRN), row(W_HGRN), row(D_MODEL), mod.spec(5, tm, 2),
                  full(beta), full(gh), full(bd), full(wo16)] + _ffn_specs(mod, 6, tm, tf),
        out_specs=row(D_MODEL),
        scratch_shapes=[pltpu.VMEM((tm, D_MODEL), BF16), pltpu.VMEM((tm, D_MODEL), F32),
                        pltpu.VMEM((tm, D_MODEL), F32)],
        compiler_params=_cp(("arbitrary", "arbitrary")),
        name="merge_ffn",
    )(oa, oh, sg, x, mod.arr, beta, gh, bd, wo16, mod.arr, mod.arr, mod.arr, g, wg, wu, wd)


_C_QA, _C_KA, _C_VA, _C_QI = 0, 512, 1024, 1536
_C_KI, _C_WI = 2048, 2176
_C_QB, _C_FB, _C_IB, _C_GB = 2304, 2816, 3328, 3840
_N_PACK = 4352


def _mix_kernel(x_ref, sh_ref, sc_ref, g_ref, w_ref, gq_ref, gk_ref, lbl_ref, bd_ref,
                qa_o, ka_o, ka16_o, va_o, va16_o, qi_o, ki_o, ki16_o, wi_o, qb_o, f_o, vb_o, sg_o):
    h = (_rms(x_ref[...], g_ref[...]) * (1.0 + sc_ref[...]) + sh_ref[...]).astype(BF16)

    def proj(c0, width):
        return jnp.dot(h, w_ref[:, c0:c0 + width], preferred_element_type=F32)

    bd = bd_ref[...]
    qa = proj(_C_QA, W_ATTN)
    qa = qa * lax.rsqrt(_group_mean_sq(qa, bd) + RMS_EPS) * gq_ref[...]
    qa_o[...] = (qa * (D_HA ** -0.5)).astype(BF16)
    ka = proj(_C_KA, W_ATTN)
    ka = ka * lax.rsqrt(_group_mean_sq(ka, bd) + RMS_EPS) * gk_ref[...]
    ka_o[...] = ka.T
    ka16_o[...] = ka.astype(BF16)
    va_t = proj(_C_VA, W_ATTN).T
    va_o[...] = va_t
    va16_o[...] = va_t.astype(BF16)
    qi_o[...] = (proj(_C_QI, H_I * D_I) * (D_I ** -0.5)).astype(BF16)
    ki = proj(_C_KI, LANES)
    ki_o[...] = ki
    ki16_o[...] = ki.astype(BF16)
    wi_o[...] = proj(_C_WI, LANES) * (H_I ** -0.5)
    qb_o[...] = _silu(proj(_C_QB, W_HGRN))
    l0 = lbl_ref[0:1, :]
    l1 = lbl_ref[1:2, :]
    mx = jnp.maximum(l0, l1)
    e0 = jnp.exp(l0 - mx)
    e1 = jnp.exp(l1 - mx)
    lb = e0 / (e0 + e1)
    f_o[...] = lb + (1.0 - lb) * _sigmoid(proj(_C_FB, W_HGRN))
    vb_o[...] = proj(_C_IB, W_HGRN)
    sg_o[...] = _silu(proj(_C_GB, W_HGRN))


def _mix(x, mod, g, wpack, gq, gk, lbl, bd, tm, rows_per_group):
    n = x.shape[0]
    tpg = rows_per_group // tm
    row = lambda w: pl.BlockSpec((tm, w), lambda i: (i, 0))
    full = lambda a: pl.BlockSpec(a.shape, lambda i: (0,) * a.ndim)
    featmajor = pl.BlockSpec((None, W_ATTN, tm), lambda i: (i // tpg, 0, i % tpg))
    outs = [(W_ATTN, BF16), None, (W_ATTN, BF16), None, None,
            (H_I * D_I, BF16), (LANES, F32), (LANES, BF16), (LANES, F32),
            (W_HGRN, F32), (W_HGRN, F32), (W_HGRN, F32), (W_HGRN, F32)]
    fm_dtypes = {1: F32, 3: F32, 4: BF16}
    return pl.pallas_call(
        _mix_kernel,
        out_shape=[jax.ShapeDtypeStruct((n // rows_per_group, W_ATTN, rows_per_group), fm_dtypes[k]) if o is None
                   else jax.ShapeDtypeStruct((n, o[0]), o[1]) for k, o in enumerate(outs)],
        grid=(n // tm,),
        in_specs=[row(D_MODEL), mod.spec(3, tm, 1), mod.spec(4, tm, 1), full(g), full(wpack),
                  full(gq), full(gk), full(lbl), full(bd)],
        out_specs=[featmajor if o is None else row(o[0]) for o in outs],
        compiler_params=_cp(("arbitrary",)),
        name="mix",
    )(x, mod.arr, mod.arr, g, wpack, gq, gk, lbl, bd)


def _t5_bucket(dist):
    n = jnp.maximum(dist, 0)
    max_exact = T5_BUCKETS // 2
    nf = jnp.maximum(n, 1).astype(F32)
    large = max_exact + (jnp.log(nf / max_exact) / math.log(T5_MAX_DIST / max_exact)
                         * (T5_BUCKETS - max_exact)).astype(I32)
    large = jnp.minimum(large, T5_BUCKETS - 1)
    return jnp.where(n < max_exact, n, large)


def _bias_kernel(rb_ref, bp_ref, bs_ref):
    def lookup(bucket, h):
        val = jnp.zeros(bucket.shape, F32)
        for b in range(T5_BUCKETS):
            val = jnp.where(bucket == b, rb_ref[b, h], val)
        return val

    k = lax.broadcasted_iota(I32, (2 * LANES, LANES), 0)
    q = lax.broadcasted_iota(I32, (2 * LANES, LANES), 1)
    far_bucket = _t5_bucket(jnp.full((2 * LANES, LANES), 2 * LANES, I32))
    for v, dist in enumerate((q - k, LANES + q - k)):
        bucket = _t5_bucket(dist)
        for h in range(H_A):
            bp_ref[v, h // 2, :, (h % 2) * LANES:(h % 2 + 1) * LANES] = lookup(bucket, h) - lookup(far_bucket, h)

    t = lax.broadcasted_iota(I32, (SUBLANES, LANES), 0)
    u = lax.broadcasted_iota(I32, (SUBLANES, LANES), 1)
    sdists = (PAGE + t - u, t - u, jnp.full((SUBLANES, LANES), 2 * LANES, I32))
    for m in range(3):
        bucket = _t5_bucket(sdists[m])
        for h in range(H_A):
            bs_ref[m, h * SUBLANES:(h + 1) * SUBLANES, :] = lookup(bucket, h)


def _bias_tiles(rel_bias):
    return pl.pallas_call(
        _bias_kernel,
        out_shape=[jax.ShapeDtypeStruct((2, H_A // 2, 2 * LANES, 2 * LANES), F32),
                   jax.ShapeDtypeStruct((3, H_A * SUBLANES, LANES), F32)],
        in_specs=[pl.BlockSpec(memory_space=pltpu.SMEM)],
        out_specs=[pl.BlockSpec(memory_space=pltpu.VMEM), pl.BlockSpec(memory_space=pltpu.VMEM)],
        name="t5_bias",
    )(rel_bias)


def _sortable_key(score):
    bits = pltpu.bitcast(score, I32)
    return jnp.where(bits < 0, INT_MIN - bits, bits)


def _tree_reduce(x, op):
    while x.shape[0] > 1:
        half = x.shape[0] // 2
        y = op(x[:half], x[half:2 * half])
        x = y if x.shape[0] % 2 == 0 else jnp.concatenate([y, x[2 * half:]], axis=0)
    return x[0]


def _sublane_allreduce(x, op):
    for sh in (4, 2, 1):
        x = op(x, pltpu.roll(x, sh, 0))
    return x


def _topk_maskbias_t(keys_ref, maskb_ref, far_ref, nch, cw, n_sel, idx_bits, far_limit):
    ng = cw // SUBLANES
    sub = lax.broadcasted_iota(I32, (ng, SUBLANES, LANES), 0) * SUBLANES + lax.broadcasted_iota(
        I32, (ng, SUBLANES, LANES), 1)

    def count(indicator):
        def body(c, acc):
            k0 = pl.multiple_of(c * cw, cw)
            kk = keys_ref[pl.ds(k0, cw), :].reshape(ng, SUBLANES, LANES)
            return acc + _tree_reduce(indicator(kk, k0), jnp.add)
        acc = lax.fori_loop(0, nch, body, jnp.zeros((SUBLANES, LANES), F32))
        return _sublane_allreduce(acc, jnp.add)

    def bisect(it, ans):
        cand = ans | jnp.left_shift(jnp.int32(1), 31 - it)
        cs = cand ^ INT_MIN
        cnt = count(lambda kk, k0: jnp.where(kk >= cs[None], 1.0, 0.0))
        return jnp.where(cnt >= n_sel, cand, ans)

    thr = lax.fori_loop(0, 32, bisect, jnp.zeros((SUBLANES, LANES), I32)) ^ INT_MIN
    need = n_sel - count(lambda kk, k0: jnp.where(kk > thr[None], 1.0, 0.0))
    n_eq = count(lambda kk, k0: jnp.where(kk == thr[None], 1.0, 0.0))
    has_tie = jnp.max(jnp.where(thr == INT_MIN, 0.0, jnp.where(n_eq > need, 1.0, 0.0))) > 0.0

    def tie_search(_):
        def step(it, aj):
            cand = aj | jnp.left_shift(jnp.int32(1), idx_bits - 1 - it)
            cnt = count(lambda kk, k0: jnp.where(kk == thr[None], jnp.where((k0 + sub) < cand[None], 1.0, 0.0), 0.0))
            return jnp.where(cnt < need, cand, aj)
        return lax.fori_loop(0, idx_bits, step, jnp.zeros((SUBLANES, LANES), I32))

    cut = lax.cond(has_tie, tie_search, lambda _: jnp.full((SUBLANES, LANES), 2 ** 30, I32), 0)
    cut = jnp.where(thr == INT_MIN, -1, cut)

    def fin(c, _):
        k0 = pl.multiple_of(c * cw, cw)
        kk = keys_ref[pl.ds(k0, cw), :].reshape(ng, SUBLANES, LANES)
        kpos = k0 + sub
        tie = jnp.where(kpos <= cut[None], 0.0, NEG)
        mb = jnp.where(kk > thr[None], 0.0, jnp.where(kk == thr[None], tie, NEG))
        maskb_ref[pl.ds(k0, cw), :] = mb.reshape(cw, LANES)
        if far_ref is not None:
            far_ref[pl.ds(k0, cw), :] = jnp.where(kpos < far_limit, mb, NEG).reshape(cw, LANES)
        return 0

    lax.fori_loop(0, nch, fin, 0)


def _softmax_step(carry, s, v16, v_is_transposed=False):
    m, l, acc = carry
    mn = jnp.maximum(m, jnp.max(s, axis=1, keepdims=True))
    alpha = jnp.exp(m - mn)
    p = jnp.exp(s - mn)
    l = alpha * l + jnp.sum(p, axis=1, keepdims=True)
    if v_is_transposed:
        pv = lax.dot_general(p.astype(BF16), v16, _NT, preferred_element_type=F32)
    else:
        pv = jnp.dot(p.astype(BF16), v16, preferred_element_type=F32)
    return mn, l, alpha * acc + pv


_QB = 128
_KC = 512
_FW = 1024


def _prompt_attn_step(step, nsteps, i, qa_ref, qi_ref, wi_ref, ki_ref, ka_ref, vat_ref, bias_ref, o_ref,
                      keys_sc, maskb_sc, far_sc, qs_sc, q2_sc, m_sc, l_sc, acc_sc):
    lane = lax.broadcasted_iota(I32, (_QB, LANES), 1)
    lo = lane < D_I
    nch = i // (_KC // _QB) + 1
    near0 = jnp.maximum(i - 1, 0) * _QB
    npair = H_A // 2
    cols2 = 2 * _QB

    @pl.when(step == 0)
    def _():
        for h in range(H_I):
            pair = qi_ref[:, (h // 2) * LANES:(h // 2 + 1) * LANES].astype(F32)
            keep = jnp.where(lo, pair, 0.0) if h % 2 == 0 else jnp.where(lo, 0.0, pair)
            qs_sc[h * _QB:(h + 1) * _QB, :] = keep.astype(BF16)

        qpos = i * _QB + lax.broadcasted_iota(I32, (_KC, LANES), 1)

        def scores(c, _):
            k0 = pl.multiple_of(c * _KC, _KC)
            s = lax.dot_general(ki_ref[pl.ds(k0, _KC), :], qs_sc[...], _NT, preferred_element_type=F32)
            sc = jnp.zeros((_KC, LANES), F32)
            for h in range(H_I):
                sc = sc + jnp.maximum(s[:, h * _QB:(h + 1) * _QB], 0.0) * wi_ref[h:h + 1, :]
            kpos = k0 + lax.broadcasted_iota(I32, (_KC, LANES), 0)
            keys_sc[pl.ds(k0, _KC), :] = jnp.where(kpos <= qpos, _sortable_key(sc), INT_MIN)
            return 0

        lax.fori_loop(0, nch, scores, 0)
        _topk_maskbias_t(keys_sc, maskb_sc, far_sc, nch, _KC, TOPK_MAX, 12, near0)

        @pl.when(((near0 + _FW - 1) // _FW) * _FW > nch * _KC)
        def _():
            far_sc[pl.ds(pl.multiple_of(nch * _KC, _KC), _KC), :] = jnp.full((_KC, LANES), NEG, F32)

        for p in range(npair):
            qp = qa_ref[:, p * LANES:(p + 1) * LANES].astype(F32)
            q2_sc[p * cols2:(p + 1) * cols2, :] = jnp.concatenate(
                [jnp.where(lo, qp, 0.0), jnp.where(lo, 0.0, qp)], axis=0).astype(BF16)
        m_sc[...] = jnp.full(m_sc.shape, NEG, F32)
        l_sc[...] = jnp.zeros_like(l_sc)
        acc_sc[...] = jnp.zeros_like(acc_sc)

    def update_all(k0, width, mask_ref, biases):
        mb = mask_ref[pl.ds(k0, width), :]
        mb2 = jnp.concatenate([mb, mb], axis=1)

        def qk(p):
            s = lax.dot_general(ka_ref[pl.ds(k0, width), p * LANES:(p + 1) * LANES],
                                q2_sc[p * cols2:(p + 1) * cols2, :], _NT, preferred_element_type=F32) + mb2
            if biases is not None:
                s = s + biases[p]
            return s.reshape(width // SUBLANES, SUBLANES, cols2)

        def softmax(p, s):
            m_old = m_sc[p]
            m_new = jnp.maximum(m_old, _sublane_allreduce(_tree_reduce(s, jnp.maximum), jnp.maximum))
            alpha = jnp.exp(m_old - m_new)
            pexp = jnp.exp(s - m_new[None])
            m_sc[p] = m_new
            l_sc[p] = alpha * l_sc[p] + _tree_reduce(pexp, jnp.add)
            return pexp.reshape(width, cols2).astype(BF16), alpha

        def pv(p, p16, alpha):
            out = jnp.dot(vat_ref[p * LANES:(p + 1) * LANES, pl.ds(k0, width)], p16,
                          preferred_element_type=F32)
            acc = acc_sc[p].reshape(LANES // SUBLANES, SUBLANES, cols2) * alpha[None]
            acc_sc[p] = acc.reshape(LANES, cols2) + out

        logits = [qk(p) for p in range(npair)]
        probs = [softmax(p, logits[p]) for p in range(npair)]
        for p in range(npair):
            pv(p, *probs[p])

    def far(j, _):
        update_all(pl.multiple_of((step + j * nsteps) * _FW, _FW), _FW, far_sc, None)
        return 0

    nfar = (near0 + _FW - 1) // _FW
    lax.fori_loop(0, (nfar - step + nsteps - 1) // nsteps, far, 0)

    @pl.when(step == nsteps - 1)
    def _():
        first = jnp.minimum(i, 1)
        update_all(pl.multiple_of(near0, _QB), 2 * _QB, maskb_sc, [bias_ref[first, p] for p in range(npair)])
        for p in range(npair):
            l = _sublane_allreduce(l_sc[p], jnp.add)
            o = (acc_sc[p].reshape(LANES // SUBLANES, SUBLANES, cols2) / l[None]).reshape(LANES, cols2)
            ot = jnp.concatenate([o[:D_HA, :_QB], o[D_HA:, _QB:]], axis=0)
            o_ref[:, p * LANES:(p + 1) * LANES] = ot.T


_PPS = 16
_TP = SUBLANES
_TK = 2 * SUBLANES


class _PageStream:
    def __init__(self, pt_ref, hbm_refs, bufs, sems, pages_per_group):
        self.pt_ref, self.hbm_refs, self.bufs, self.sems, self.ppg = pt_ref, hbm_refs, bufs, sems, pages_per_group

    def _copies(self, seq, group, slot):
        for hbm, buf, sem in zip(self.hbm_refs, self.bufs, self.sems):
            for p in range(self.ppg):
                yield pltpu.make_async_copy(hbm.at[self.pt_ref[seq, group * self.ppg + p]], buf.at[slot, p],
                                            sem.at[slot])

    def start(self, seq, group, slot):
        for cp in self._copies(seq, group, slot):
            cp.start()

    def wait(self, seq, group, slot):
        for cp in self._copies(seq, group, slot):
            cp.wait()

    def advance(self, step, nsteps, groups_per_seq):
        slot = lax.rem(step, 2)

        @pl.when(step == 0)
        def _():
            self.start(0, 0, 0)

        @pl.when(step + 1 < nsteps)
        def _():
            nxt = step + 1
            self.start(nxt // groups_per_seq, lax.rem(nxt, groups_per_seq), 1 - slot)

        self.wait(step // groups_per_seq, lax.rem(step, groups_per_seq), slot)
        return slot


def _sample_scores_kernel(pt_ref, q_ref, w_ref, kn_ref, cki_ref, o_ref, buf, sem):
    n_pages = buf.shape[1]
    slot = _PageStream(pt_ref, [cki_ref], [buf], [sem], n_pages).advance(pl.program_id(0), pl.num_programs(0), 1)
    q = q_ref[...]

    t_dec = o_ref.shape[0]

    def head_sum(s):
        n = s.shape[1]
        out = jnp.zeros((_TP, n), F32)
        for h in range(H_I):
            wh = w_ref[h * _TP:(h + 1) * _TP, :]
            out = out + jnp.maximum(s[h * _TP:(h + 1) * _TP, :], 0.0) * jnp.concatenate([wh] * (n // LANES), axis=1)
        return out[:t_dec]

    for c in range(n_pages // _PPS):
        kc = jnp.concatenate([buf[slot, c * _PPS + p].astype(BF16) for p in range(_PPS)], axis=1)
        s = jnp.dot(q, kc, preferred_element_type=F32)
        o_ref[:, c * _PPS * PAGE:(c + 1) * _PPS * PAGE] = head_sum(s)

    kn = jnp.concatenate([kn_ref[...], jnp.zeros((LANES - _TK, D_I), BF16)], axis=0)
    sn = lax.dot_general(q, kn, _NT, preferred_element_type=F32)
    o_ref[:, o_ref.shape[1] - LANES:] = head_sum(sn)


def _sample_scores(page_table, q_st, w_b, ki_new, cache_kidx, n_pages, t_dec):
    nseq = q_st.shape[0]
    lpad = n_pages * PAGE + LANES
    return pl.pallas_call(
        _sample_scores_kernel,
        out_shape=jax.ShapeDtypeStruct((nseq, t_dec, lpad), F32),
        grid_spec=pltpu.PrefetchScalarGridSpec(
            num_scalar_prefetch=1, grid=(nseq,),
            in_specs=[pl.BlockSpec((None, H_I * _TP, D_I), lambda b, pt: (b, 0, 0)),
                      pl.BlockSpec((None, H_I * _TP, LANES), lambda b, pt: (b, 0, 0)),
                      pl.BlockSpec((None, _TK, D_I), lambda b, pt: (b, 0, 0)),
                      pl.BlockSpec(memory_space=pl.ANY)],
            out_specs=pl.BlockSpec((None, t_dec, lpad), lambda b, pt: (b, 0, 0)),
            scratch_shapes=[pltpu.VMEM((2, n_pages, D_I, PAGE), F32), pltpu.SemaphoreType.DMA((2,))]),
        compiler_params=_cp(("arbitrary",)),
        name="sample_scores",
    )(page_table, q_st, w_b, ki_new, cache_kidx)


_SEL_CW = 640


def _sample_select_kernel(s_ref, o_ref, keys_sc, *, past, t_dec):
    lpad = s_ref.shape[0]
    assert LANES % t_dec == 0
    limit = past + lax.broadcasted_iota(I32, (_SEL_CW, LANES), 1) % t_dec
    for c in range(lpad // _SEL_CW):
        sl = slice(c * _SEL_CW, (c + 1) * _SEL_CW)
        kpos = c * _SEL_CW + lax.broadcasted_iota(I32, (_SEL_CW, LANES), 0)
        keys_sc[sl, :] = jnp.where(kpos <= limit, _sortable_key(s_ref[sl, :]), INT_MIN)
    _topk_maskbias_t(keys_sc, o_ref, None, lpad // _SEL_CW, _SEL_CW, TOPK_MAX, 14, None)


def _sample_select(scores_t, past, t_dec):
    lpad, cols = scores_t.shape
    return pl.pallas_call(
        functools.partial(_sample_select_kernel, past=past, t_dec=t_dec),
        out_shape=jax.ShapeDtypeStruct((lpad, cols), F32),
        grid=(cols // LANES,),
        in_specs=[pl.BlockSpec((lpad, LANES), lambda g: (0, g))],
        out_specs=pl.BlockSpec((lpad, LANES), lambda g: (0, g)),
        scratch_shapes=[pltpu.VMEM((lpad, LANES), I32)],
        compiler_params=_cp(("arbitrary",)),
        name="sample_select",
    )(scores_t)


def _sample_attn_step(s_id, last, q_ref, mb_ref, kn_ref, vn_ref, bias_ref, kp, vp, o_ref,
                      qblk_sc, m_sc, l_sc, acc_sc):
    rows = H_A * _TP
    span = _PPS * PAGE

    @pl.when(s_id == 0)
    def _():
        q = jnp.concatenate([q_ref[...]] * H_A, axis=0)
        r = lax.broadcasted_iota(I32, (rows, W_ATTN), 0) // _TP
        c = lax.broadcasted_iota(I32, (rows, W_ATTN), 1) // D_HA
        qblk_sc[...] = jnp.where(r == c, q, 0.0).astype(BF16)
        m_sc[...] = jnp.full(m_sc.shape, NEG, F32)
        l_sc[...] = jnp.zeros_like(l_sc)
        acc_sc[...] = jnp.zeros_like(acc_sc)

    def update(s, v16, v_is_transposed):
        carry = (m_sc[:, 0:1], l_sc[:, 0:1], acc_sc[...])
        m, l, acc = _softmax_step(carry, s, v16, v_is_transposed)
        m_sc[...] = jnp.broadcast_to(m, m_sc.shape)
        l_sc[...] = jnp.broadcast_to(l, l_sc.shape)
        acc_sc[...] = acc

    k16 = jnp.concatenate([pg[...].astype(BF16) for pg in kp], axis=1)
    v16 = jnp.concatenate([pg[...].astype(BF16) for pg in vp], axis=1)
    s = jnp.dot(qblk_sc[...], k16, preferred_element_type=F32)
    mb = mb_ref[:, pl.ds(pl.multiple_of(s_id * span, span), span)]
    last_tile = jnp.where(s_id == last, bias_ref[0], bias_ref[2])
    bias = jnp.concatenate([bias_ref[2]] * (_PPS - 1) + [last_tile], axis=1)
    update(s + jnp.concatenate([mb] * H_A, axis=0) + bias, v16, True)

    @pl.when(s_id == last)
    def _():
        pad = jnp.zeros((LANES - _TK, W_ATTN), BF16)
        kn = jnp.concatenate([kn_ref[...], pad], axis=0)
        vn = jnp.concatenate([vn_ref[...], pad], axis=0)
        sn = lax.dot_general(qblk_sc[...], kn, _NT, preferred_element_type=F32)
        mbn = mb_ref[:, mb_ref.shape[1] - LANES:]
        update(sn + jnp.concatenate([mbn] * H_A, axis=0) + bias_ref[1], vn, False)
        o = acc_sc[...] / l_sc[:, 0:1]
        c = lax.broadcasted_iota(I32, (_TP, W_ATTN), 1) // D_HA
        out = jnp.zeros((_TP, W_ATTN), F32)
        for h in range(H_A):
            out = jnp.where(c == h, o[h * _TP:(h + 1) * _TP, :], out)
        o_ref[...] = out


def _attn_kernel(pt_ref, qa_ref, qi_ref, wi_ref, ki_ref, ka_ref, vat_ref, bias_p_ref,
                 sq_ref, smb_ref, skn_ref, svn_ref, bias_s_ref, ck_ref, cv_ref, o_ref, so_ref, *scratch,
                 nq, nsteps, nseq):
    prompt_scratch, sample_scratch = scratch[:8], scratch[8:12]
    kbuf, vbuf, ksem, vsem = scratch[12:]
    step = pl.program_id(2)
    seq_id = pl.program_id(0) * nq + pl.program_id(1)
    stream = _PageStream(pt_ref, [ck_ref, cv_ref], [kbuf, vbuf], [ksem, vsem], _PPS)
    slot = stream.advance(seq_id * nsteps + step, nseq * nsteps, nsteps)
    _prompt_attn_step(step, nsteps, pl.program_id(1), qa_ref, qi_ref, wi_ref, ki_ref, ka_ref, vat_ref, bias_p_ref,
                      o_ref, *prompt_scratch)
    _sample_attn_step(step, nsteps - 1, sq_ref, smb_ref, skn_ref, svn_ref, bias_s_ref,
                      [kbuf.at[slot, p] for p in range(_PPS)], [vbuf.at[slot, p] for p in range(_PPS)], so_ref,
                      *sample_scratch)


def _attention(qa16, qi16, wi_t, ki16, ka16, va16_t, bias_p, nbatch, seq,
               page_table, q8, maskb, k_new, v_new, bias_s, cache_k, cache_v, n_pages):
    nq = seq // _QB
    nseq = q8.shape[0]
    steps = n_pages // _PPS
    lpad = maskb.shape[1]
    npair = H_A // 2
    assert nseq == nbatch * nq, "one decode sequence per prompt query block"
    blk = lambda w: pl.BlockSpec((_QB, w), lambda b, i, s, pt: (b * nq + i, 0))
    once = pl.Buffered(1)
    per_b = lambda w: pl.BlockSpec((seq, w), lambda b, i, s, pt: (b, 0), pipeline_mode=once)
    seq3 = lambda t: pl.BlockSpec((None, t, W_ATTN), lambda b, i, s, pt: (b * nq + i, 0, 0))
    hbm = pl.BlockSpec(memory_space=pl.ANY)
    return pl.pallas_call(
        functools.partial(_attn_kernel, nq=nq, nsteps=steps, nseq=nseq),
        out_shape=[jax.ShapeDtypeStruct((nbatch * seq, W_ATTN), F32),
                   jax.ShapeDtypeStruct((nseq, _TP, W_ATTN), F32)],
        grid_spec=pltpu.PrefetchScalarGridSpec(
            num_scalar_prefetch=1, grid=(nbatch, nq, steps),
            in_specs=[blk(W_ATTN), blk(H_I * D_I),
                      pl.BlockSpec((H_I, _QB), lambda b, i, s, pt: (0, b * nq + i)),
                      per_b(LANES), per_b(W_ATTN),
                      pl.BlockSpec((None, W_ATTN, seq), lambda b, i, s, pt: (b, 0, 0), pipeline_mode=once),
                      pl.BlockSpec(bias_p.shape, lambda b, i, s, pt: (0, 0, 0, 0), pipeline_mode=once),
                      seq3(_TP), pl.BlockSpec((_TP, lpad), lambda b, i, s, pt: (b * nq + i, 0)),
                      seq3(_TK), seq3(_TK),
                      pl.BlockSpec(bias_s.shape, lambda b, i, s, pt: (0, 0, 0), pipeline_mode=once),
                      hbm, hbm],
            out_specs=[blk(W_ATTN), seq3(_TP)],
            scratch_shapes=[pltpu.VMEM((seq, LANES), I32), pltpu.VMEM((seq, LANES), F32),
                            pltpu.VMEM((seq, LANES), F32),
                            pltpu.VMEM((H_I * _QB, LANES), BF16), pltpu.VMEM((H_A * _QB, LANES), BF16),
                            pltpu.VMEM((npair, SUBLANES, 2 * _QB), F32), pltpu.VMEM((npair, SUBLANES, 2 * _QB), F32),
                            pltpu.VMEM((npair, LANES, 2 * _QB), F32),
                            pltpu.VMEM((H_A * _TP, W_ATTN), BF16), pltpu.VMEM((H_A * _TP, LANES), F32),
                            pltpu.VMEM((H_A * _TP, LANES), F32), pltpu.VMEM((H_A * _TP, W_ATTN), F32),
                            pltpu.VMEM((2, _PPS, W_ATTN, PAGE), F32), pltpu.VMEM((2, _PPS, W_ATTN, PAGE), F32),
                            pltpu.SemaphoreType.DMA((2,)), pltpu.SemaphoreType.DMA((2,))]),
        compiler_params=_cp(("arbitrary", "arbitrary", "arbitrary")),
        name="attention",
    )(page_table, qa16, qi16, wi_t, ki16, ka16, va16_t, bias_p, q8, maskb, k_new, v_new, bias_s, cache_k, cache_v)


def _hgrn_chunk(q, k, v, logf, st, c):
    blk = SUBLANES // 2 if c > SUBLANES else SUBLANES
    row = lax.broadcasted_iota(I32, (c, DK_B), 0)
    g = logf
    sh = 1
    while sh < c:
        g = g + jnp.where(row >= sh, pltpu.roll(g, sh, 0), 0.0)
        sh *= 2

    o = lax.dot_general((q * jnp.exp(g)).astype(BF16), st.astype(BF16), _NT, preferred_element_type=F32)

    if c > SUBLANES:
        rr = lax.broadcasted_iota(I32, (c, c), 0)
        cc = lax.broadcasted_iota(I32, (c, c), 1)
        a = jnp.zeros((c, c), F32)
        w = blk
        while w < c:
            nb = c // (2 * w)
            gb = jnp.broadcast_to(g.reshape(nb, 2 * w, DK_B)[:, w - 1:w, :], (nb, 2 * w, DK_B)).reshape(c, DK_B)
            right = (row % (2 * w)) >= w
            qt = jnp.where(right, q * jnp.exp(jnp.where(right, g - gb, 0.0)), 0.0)
            kt = jnp.where(right, 0.0, k * jnp.exp(jnp.where(right, 0.0, gb - g)))
            aw = lax.dot_general(qt.astype(BF16), kt.astype(BF16), _NT, preferred_element_type=F32)
            a = a + jnp.where((rr // (2 * w)) == (cc // (2 * w)), aw, 0.0)
            w *= 2
        o = o + jnp.dot(a.astype(BF16), v.astype(BF16), preferred_element_type=F32)

    o = o + jnp.sum(q * k, axis=1, keepdims=True) * v
    for d in range(1, blk):
        ok = (row % blk) >= d
        e = jnp.exp(jnp.where(ok, g - pltpu.roll(g, d, 0), 0.0))
        coef = jnp.sum(jnp.where(ok, q * pltpu.roll(k, d, 0) * e, 0.0), axis=1, keepdims=True)
        o = o + coef * pltpu.roll(v, d, 0)

    g_end = g[c - 1:c, :]
    kd = k * jnp.exp(g_end - g)
    st = st * jnp.exp(g_end) + lax.dot_general(v.astype(BF16), kd.astype(BF16), _TN, preferred_element_type=F32)
    return o, st


_HT = 1024


def _hgrn_kernel(q_ref, f_ref, v_ref, s0_ref, o_ref, s_ref, st_sc, *, nb, tb, c):
    tstep = pl.program_id(1)

    @pl.when(tstep == 0)
    def _():
        for j in range(nb):
            for g in range(H_B):
                st_sc[j, g] = s0_ref[j, g].T

    for j in range(nb):
        def body(ci, states):
            r = pl.ds(pl.multiple_of(ci * c, c), c)
            new_states = []
            for g in range(H_B):
                cs = slice(g * DK_B, (g + 1) * DK_B)
                f = f_ref[j, r, cs]
                o, st = _hgrn_chunk(q_ref[j, r, cs], 1.0 - f, v_ref[j, r, cs], jnp.log(f), states[g], c)
                o_ref[j, r, cs] = o
                new_states.append(st)
            return tuple(new_states)
        states = lax.fori_loop(0, tb // c, body, tuple(st_sc[j, g] for g in range(H_B)))
        for g in range(H_B):
            st_sc[j, g] = states[g]

    @pl.when(tstep == pl.num_programs(1) - 1)
    def _():
        for j in range(nb):
            for g in range(H_B):
                s_ref[j, g] = st_sc[j, g].T


def _hgrn(q, f, v, s0, nb, c):
    nseq, t_len, _ = q.shape
    tb = min(t_len, _HT)
    tok = pl.BlockSpec((nb, tb, W_HGRN), lambda b, t: (b, t, 0))
    st = pl.BlockSpec((nb, H_B, DK_B, DV_B), lambda b, t: (b, 0, 0, 0))
    return pl.pallas_call(
        functools.partial(_hgrn_kernel, nb=nb, tb=tb, c=c),
        out_shape=[jax.ShapeDtypeStruct((nseq, t_len, W_HGRN), F32),
                   jax.ShapeDtypeStruct((nseq, H_B, DK_B, DV_B), F32)],
        grid=(nseq // nb, t_len // tb),
        in_specs=[tok, tok, tok, st],
        out_specs=[tok, st],
        scratch_shapes=[pltpu.VMEM((nb, H_B, DV_B, DK_B), F32)],
        compiler_params=_cp(("arbitrary", "arbitrary")),
        name="hgrn",
    )(q, f, v, s0)


def _pad_tokens(a, value=0.0, to=_TP):
    return jnp.pad(a, ((0, 0), (0, to - a.shape[1]), (0, 0)), constant_values=value)


def kernel(x_prompt, x_sample, cache_k, cache_v, cache_kidx, state_hgrn, page_table, c_prompt, c_sample,
           w_ada, b_ada, g_ffn1, ffn1_w_gate, ffn1_w_up, ffn1_w_down, g_mix, w_in, g_q, g_k,
           beta_attn, g_hgrn, w_out, g_ffn2, ffn2_w_gate, ffn2_w_up, ffn2_w_down, rel_bias, lb_logits):
    assert w_ada.shape[0] == 1, "single-layer problem"
    nb_p, seq, _ = x_prompt.shape
    nb_s, t_dec, _ = x_sample.shape
    n_phys = cache_k.shape[1]
    n_pages = page_table.shape[1]
    past = n_pages * PAGE
    assert t_dec <= _TP and n_pages % _PPS == 0 and (past + LANES) % _SEL_CW == 0

    row2 = lambda a: a.reshape(1, -1)
    w_in0 = w_in[0]
    pad_w = jnp.zeros((D_MODEL, LANES - H_I), F32)
    wpack = jnp.concatenate(
        [w_in0[:, :2048], w_in0[:, 2048:2112], w_in0[:, 2048:2112], w_in0[:, 2112:2120], pad_w,
         w_in0[:, 2120:]], axis=1).astype(BF16)
    assert wpack.shape[1] == _N_PACK
    gq = row2(jnp.tile(g_q[0], H_A))
    gk = row2(jnp.tile(g_k[0], H_A))
    gidx = jnp.arange(W_ATTN) // D_HA
    bd = jnp.where(gidx[:, None] == gidx[None, :], 1.0 / D_HA, 0.0).astype(BF16)
    ffn1 = (row2(g_ffn1[0]), ffn1_w_gate[0].astype(BF16), ffn1_w_up[0].astype(BF16), ffn1_w_down[0].astype(BF16))
    ffn2 = (row2(g_ffn2[0]), ffn2_w_gate[0].astype(BF16), ffn2_w_up[0].astype(BF16), ffn2_w_down[0].astype(BF16))
    w_out16 = w_out[0].astype(BF16)

    n_c = nb_p + nb_s
    c_pad = (-n_c) % SUBLANES
    c_all = jnp.concatenate([c_prompt, c_sample, jnp.zeros((c_pad, D_MODEL), F32)], axis=0)
    mods = _ada(c_all, w_ada[0].astype(BF16), row2(b_ada[0]))
    bias_p, bias_s = _bias_tiles(rel_bias)

    def front(x, mod, tm, tf, rows_per_group):
        x1 = _ffn(x, mod, 0, *ffn1, tm, tf)
        return x1, _mix(x1, mod, row2(g_mix[0]), wpack, gq, gk, lb_logits, bd, tm, rows_per_group)

    def back(x1, mod, tm, tf, oa, oh, sg):
        return _merge_ffn(oa, oh, sg, x1, mod, row2(beta_attn[0]), row2(g_hgrn[0]), bd, w_out16, *ffn2, tm, tf)

    tm_p, tf = 512, D_FF // 2
    n_tok = nb_s * t_dec
    mod_p = _Mod(mods[:, :nb_p].reshape(N_MOD, nb_p, 1, D_MODEL), False, seq // tm_p)
    mod_s = _Mod(jnp.repeat(mods[:, nb_p:nb_p + nb_s], t_dec, axis=1), True)
    x1p, proj_p = front(x_prompt.reshape(nb_p * seq, D_MODEL), mod_p, tm_p, tf, seq)
    x1s, proj_s = front(x_sample.reshape(n_tok, D_MODEL), mod_s, n_tok, tf, n_tok)
    qa16, kp_t, ka16, vp_t, va16_t, qi16, ki_p, ki16, wi, qb, f, vb, sg_p = proj_p
    qa16s, ks_t, ka16s, vs_t, va16s_t, qi16s, ki_s, ki16s, wis, qbs, fs, vbs, sg_s = proj_s
    va16s = va16s_t[0].T

    ck = jnp.transpose(cache_k[0], (0, 2, 3, 1)).reshape(n_phys, W_ATTN, PAGE)
    cv = jnp.transpose(cache_v[0], (0, 2, 3, 1)).reshape(n_phys, W_ATTN, PAGE)
    cki = jnp.transpose(cache_kidx[0], (0, 2, 1))
    seq3 = lambda a: a.reshape(nb_s, t_dec, a.shape[-1])
    q_st = _pad_tokens(seq3(qi16s)).reshape(nb_s, _TP, H_I, D_I).transpose(0, 2, 1, 3).reshape(nb_s, H_I * _TP, D_I)
    w_st = _pad_tokens(seq3(wis[:, :H_I])).transpose(0, 2, 1).reshape(nb_s, H_I * _TP, 1)
    w_b = jnp.broadcast_to(w_st, (nb_s, H_I * _TP, LANES))
    scores = _sample_scores(page_table, q_st, w_b, _pad_tokens(seq3(ki16s[:, :D_I]), to=_TK), cki, n_pages, t_dec)
    mask_t = _sample_select(scores.reshape(n_tok, -1).T, past, t_dec)
    maskb = _pad_tokens(mask_t.T.reshape(nb_s, t_dec, -1), NEG).reshape(nb_s * _TP, -1)

    oa_p, oa8 = _attention(qa16, qi16, wi[:, :H_I].T, ki16, ka16, va16_t, bias_p, nb_p, seq,
                           page_table, _pad_tokens(seq3(qa16s.astype(F32))), maskb,
                           _pad_tokens(seq3(ka16s), to=_TK), _pad_tokens(seq3(va16s), to=_TK), bias_s, ck, cv, n_pages)
    r3 = lambda a: a.reshape(nb_p, seq, W_HGRN)
    oh_p, sp = _hgrn(r3(qb), r3(f), r3(vb), jnp.zeros((nb_p, H_B, DK_B, DV_B), F32), 1, 64)
    oh8, ss = _hgrn(_pad_tokens(seq3(qbs)), _pad_tokens(seq3(fs), 1.0), _pad_tokens(seq3(vbs)), state_hgrn[0], 8, _TP)

    yp = back(x1p, mod_p, tm_p, tf, oa_p, oh_p.reshape(nb_p * seq, W_HGRN), sg_p)
    ys = back(x1s, mod_s, n_tok, tf, oa8[:, :t_dec].reshape(n_tok, W_ATTN),
              oh8[:, :t_dec].reshape(n_tok, W_HGRN), sg_s)

    def rows_out(a_t, nseq, t_len):
        a = a_t.reshape(-1, H_A, D_HA, a_t.shape[-1]).transpose(0, 3, 1, 2)
        return a.reshape(1, nseq, t_len, H_A, D_HA)

    return (yp.reshape(nb_p, seq, D_MODEL).astype(x_prompt.dtype),
            ys.reshape(nb_s, t_dec, D_MODEL).astype(x_sample.dtype),
            rows_out(kp_t, nb_p, seq), rows_out(vp_t, nb_p, seq),
            ki_p[:, :D_I].reshape(1, nb_p, seq, D_I), sp[None],
            rows_out(ks_t, nb_s, t_dec), rows_out(vs_t, nb_s, t_dec),
            ki_s[:, :D_I].reshape(1, nb_s, t_dec, D_I), ss[None])
```

```python
import functools
import math

import jax
import jax.numpy as jnp
from jax import lax
from jax.experimental import pallas as pl
from jax.experimental.pallas import tpu as pltpu

F32 = jnp.float32
BF16 = jnp.bfloat16
I32 = jnp.int32

D_MODEL = 1024
W_ATTN = 512
W_HGRN = 512
D_HA = 64
H_A = 8
H_I = 8
D_I = 64
TOPK_MAX = 256
H_B = 4
DK_B = 128
DV_B = 128
D_FF = 2816
T5_BUCKETS = 32
T5_MAX_DIST = 128
RMS_EPS = 1e-6
N_MOD = 9
PAGE = 128

LANES = 128
SUBLANES = 8
NEG = -1e30
INT_MIN = -(2 ** 31)
VMEM_LIMIT = 56 * 1024 * 1024

_NT = (((1,), (1,)), ((), ()))
_TN = (((0,), (0,)), ((), ()))


def _cp(sem):
    return pltpu.CompilerParams(dimension_semantics=sem, vmem_limit_bytes=VMEM_LIMIT)


def _silu(x):
    return x / (1.0 + jnp.exp(-x))


def _sigmoid(x):
    return 1.0 / (1.0 + jnp.exp(-x))


def _rms(x, g):
    return x * lax.rsqrt(jnp.mean(x * x, axis=-1, keepdims=True) + RMS_EPS) * g


def _group_mean_sq(x, bd):
    xx = x * x
    hi = xx.astype(BF16)
    lo = (xx - hi.astype(F32)).astype(BF16)
    return (jnp.dot(hi, bd, preferred_element_type=F32)
            + jnp.dot(lo, bd, preferred_element_type=F32))


def _ada_kernel(c_ref, w_ref, b_ref, o_ref):
    a = _silu(c_ref[...]).astype(BF16)
    o_ref[...] = jnp.dot(a, w_ref[...].astype(BF16), preferred_element_type=F32) + b_ref[...]


def _ada(c_all, w, b):
    rows = c_all.shape[0]
    return pl.pallas_call(
        _ada_kernel,
        out_shape=jax.ShapeDtypeStruct((N_MOD, rows, D_MODEL), F32),
        grid=(N_MOD,),
        in_specs=[pl.BlockSpec((rows, D_MODEL), lambda j: (0, 0)),
                  pl.BlockSpec((D_MODEL, D_MODEL), lambda j: (0, j)),
                  pl.BlockSpec((1, D_MODEL), lambda j: (0, j))],
        out_specs=pl.BlockSpec((None, rows, D_MODEL), lambda j: (j, 0, 0)),
        compiler_params=_cp(("arbitrary",)),
        name="ada",
    )(c_all, w, b)


class _Mod:
    def __init__(self, arr, per_token, tiles_per_seq=1):
        self.arr = arr
        self.per_token = per_token
        self.tiles_per_seq = tiles_per_seq

    def spec(self, j, tm, ngrid):
        if self.per_token:
            if ngrid == 1:
                return pl.BlockSpec((None, tm, D_MODEL), lambda i: (j, i, 0))
            return pl.BlockSpec((None, tm, D_MODEL), lambda i, k: (j, i, 0))
        tps = self.tiles_per_seq
        if ngrid == 1:
            return pl.BlockSpec((None, None, 1, D_MODEL), lambda i: (j, i // tps, 0, 0))
        return pl.BlockSpec((None, None, 1, D_MODEL), lambda i, k: (j, i // tps, 0, 0))


def _ffn_steps(load_x, x_res, sh_ref, sc_ref, gt_ref, g_ref, wg_ref, wu_ref, wd_ref, o_ref, h_sc, acc_sc):
    k = pl.program_id(1)

    @pl.when(k == 0)
    def _():
        h = _rms(load_x(), g_ref[...]) * (1.0 + sc_ref[...]) + sh_ref[...]
        h_sc[...] = h.astype(BF16)
        acc_sc[...] = jnp.zeros_like(acc_sc)

    h = h_sc[...]
    a = jnp.dot(h, wg_ref[...], preferred_element_type=F32)
    b = jnp.dot(h, wu_ref[...], preferred_element_type=F32)
    act = (_silu(a) * b).astype(BF16)
    acc_sc[...] += jnp.dot(act, wd_ref[...], preferred_element_type=F32)

    @pl.when(k == pl.num_programs(1) - 1)
    def _():
        o_ref[...] = x_res[...] + 0.5 * gt_ref[...] * acc_sc[...]


def _ffn_kernel(x_ref, *rest):
    _ffn_steps(lambda: x_ref[...], x_ref, *rest)


def _merge_ffn_kernel(oa_ref, oh_ref, sg_ref, x_ref, gtm_ref, beta_ref, gh_ref, bd_ref, wo_ref, *rest):
    x_sc = rest[-1]

    def load_x():
        oa = oa_ref[...]
        a = oa * lax.rsqrt(_group_mean_sq(oa, bd_ref[...]) + RMS_EPS) * beta_ref[...]
        oh = oh_ref[...]
        parts = []
        for h in range(H_B):
            sl = slice(h * DV_B, (h + 1) * DV_B)
            parts.append(_rms(oh[:, sl], gh_ref[:, sl]))
        r = jnp.concatenate(parts, axis=1) * sg_ref[...]
        y = (jnp.dot(a.astype(BF16), wo_ref[:W_ATTN, :], preferred_element_type=F32)
             + jnp.dot(r.astype(BF16), wo_ref[W_ATTN:, :], preferred_element_type=F32))
        x_sc[...] = x_ref[...] + gtm_ref[...] * y
        return x_sc[...]

    _ffn_steps(load_x, x_sc, *rest[:-1])


def _ffn_specs(mod, j0, tm, tf):
    return [mod.spec(j0, tm, 2), mod.spec(j0 + 1, tm, 2), mod.spec(j0 + 2, tm, 2),
            pl.BlockSpec((1, D_MODEL), lambda i, k: (0, 0)),
            pl.BlockSpec((D_MODEL, tf), lambda i, k: (0, k)),
            pl.BlockSpec((D_MODEL, tf), lambda i, k: (0, k)),
            pl.BlockSpec((tf, D_MODEL), lambda i, k: (k, 0))]


def _ffn(x, mod, j0, g, wg, wu, wd, tm, tf):
    n = x.shape[0]
    row = pl.BlockSpec((tm, D_MODEL), lambda i, k: (i, 0))
    return pl.pallas_call(
        _ffn_kernel,
        out_shape=jax.ShapeDtypeStruct((n, D_MODEL), F32),
        grid=(n // tm, D_FF // tf),
        in_specs=[row] + _ffn_specs(mod, j0, tm, tf),
        out_specs=row,
        scratch_shapes=[pltpu.VMEM((tm, D_MODEL), BF16), pltpu.VMEM((tm, D_MODEL), F32)],
        compiler_params=_cp(("arbitrary", "arbitrary")),
        name="ffn",
    )(x, mod.arr, mod.arr, mod.arr, g, wg, wu, wd)


def _merge_ffn(oa, oh, sg, x, mod, beta, gh, bd, wo16, g, wg, wu, wd, tm, tf):
    n = x.shape[0]
    row = lambda w: pl.BlockSpec((tm, w), lambda i, k: (i, 0))
    full = lambda a: pl.BlockSpec(a.shape, lambda i, k: (0,) * a.ndim)
    return pl.pallas_call(
        _merge_ffn_kernel,
        out_shape=jax.ShapeDtypeStruct((n, D_MODEL), F32),
        grid=(n // tm, D_FF // tf),
        in_specs=[row(W_ATTN), row(W_HGRN), row(W_HGRN), row(D_MODEL), mod.spec(5, tm, 2),
                  full(beta), full(gh), full(bd), full(wo16)] + _ffn_specs(mod, 6, tm, tf),
        out_specs=row(D_MODEL),
        scratch_shapes=[pltpu.VMEM((tm, D_MODEL), BF16), pltpu.VMEM((tm, D_MODEL), F32),
                        pltpu.VMEM((tm, D_MODEL), F32)],
        compiler_params=_cp(("arbitrary", "arbitrary")),
        name="merge_ffn",
    )(oa, oh, sg, x, mod.arr, beta, gh, bd, wo16, mod.arr, mod.arr, mod.arr, g, wg, wu, wd)


_C_QA, _C_KA, _C_VA, _C_QI = 0, 512, 1024, 1536
_C_KI, _C_WI = 2048, 2176
_C_QB, _C_FB, _C_IB, _C_GB = 2304, 2816, 3328, 3840
_N_PACK = 4352


def _mix_kernel(x_ref, sh_ref, sc_ref, g_ref, w_ref, gq_ref, gk_ref, lbl_ref, bd_ref,
                qa_o, ka_o, ka16_o, va_o, va16_o, qi_o, ki_o, ki16_o, wi_o, qb_o, f_o, vb_o, sg_o):
    h = (_rms(x_ref[...], g_ref[...]) * (1.0 + sc_ref[...]) + sh_ref[...]).astype(BF16)

    def proj(c0, width):
        return jnp.dot(h, w_ref[:, c0:c0 + width], preferred_element_type=F32)

    bd = bd_ref[...]
    qa = proj(_C_QA, W_ATTN)
    qa = qa * lax.rsqrt(_group_mean_sq(qa, bd) + RMS_EPS) * gq_ref[...]
    qa_o[...] = (qa * (D_HA ** -0.5)).astype(BF16)
    ka = proj(_C_KA, W_ATTN)
    ka = ka * lax.rsqrt(_group_mean_sq(ka, bd) + RMS_EPS) * gk_ref[...]
    ka_o[...] = ka.T
    ka16_o[...] = ka.astype(BF16)
    va_t = proj(_C_VA, W_ATTN).T
    va_o[...] = va_t
    va16_o[...] = va_t.astype(BF16)
    qi_o[...] = (proj(_C_QI, H_I * D_I) * (D_I ** -0.5)).astype(BF16)
    ki = proj(_C_KI, LANES)
    ki_o[...] = ki
    ki16_o[...] = ki.astype(BF16)
    wi_o[...] = proj(_C_WI, LANES) * (H_I ** -0.5)
    qb_o[...] = _silu(proj(_C_QB, W_HGRN))
    l0 = lbl_ref[0:1, :]
    l1 = lbl_ref[1:2, :]
    mx = jnp.maximum(l0, l1)
    e0 = jnp.exp(l0 - mx)
    e1 = jnp.exp(l1 - mx)
    lb = e0 / (e0 + e1)
    f_o[...] = lb + (1.0 - lb) * _sigmoid(proj(_C_FB, W_HGRN))
    vb_o[...] = proj(_C_IB, W_HGRN)
    sg_o[...] = _silu(proj(_C_GB, W_HGRN))


def _mix(x, mod, g, wpack, gq, gk, lbl, bd, tm, rows_per_group):
    n = x.shape[0]
    tpg = rows_per_group // tm
    row = lambda w: pl.BlockSpec((tm, w), lambda i: (i, 0))
    full = lambda a: pl.BlockSpec(a.shape, lambda i: (0,) * a.ndim)
    featmajor = pl.BlockSpec((None, W_ATTN, tm), lambda i: (i // tpg, 0, i % tpg))
    outs = [(W_ATTN, BF16), None, (W_ATTN, BF16), None, None,
            (H_I * D_I, BF16), (LANES, F32), (LANES, BF16), (LANES, F32),
            (W_HGRN, F32), (W_HGRN, F32), (W_HGRN, F32), (W_HGRN, F32)]
    fm_dtypes = {1: F32, 3: F32, 4: BF16}
    return pl.pallas_call(
        _mix_kernel,
        out_shape=[jax.ShapeDtypeStruct((n // rows_per_group, W_ATTN, rows_per_group), fm_dtypes[k]) if o is None
                   else jax.ShapeDtypeStruct((n, o[0]), o[1]) for k, o in enumerate(outs)],
        grid=(n // tm,),
        in_specs=[row(D_MODEL), mod.spec(3, tm, 1), mod.spec(4, tm, 1), full(g), full(wpack),
                  full(gq), full(gk), full(lbl), full(bd)],
        out_specs=[featmajor if o is None else row(o[0]) for o in outs],
        compiler_params=_cp(("arbitrary",)),
        name="mix",
    )(x, mod.arr, mod.arr, g, wpack, gq, gk, lbl, bd)


def _t5_bucket(dist):
    n = jnp.maximum(dist, 0)
    max_exact = T5_BUCKETS // 2
    nf = jnp.maximum(n, 1).astype(F32)
    large = max_exact + (jnp.log(nf / max_exact) / math.log(T5_MAX_DIST / max_exact)
                         * (T5_BUCKETS - max_exact)).astype(I32)
    large = jnp.minimum(large, T5_BUCKETS - 1)
    return jnp.where(n < max_exact, n, large)


def _bias_kernel(rb_ref, bp_ref, bs_ref):
    def lookup(bucket, h):
        val = jnp.zeros(bucket.shape, F32)
        for b in range(T5_BUCKETS):
            val = jnp.where(bucket == b, rb_ref[b, h], val)
        return val

    k = lax.broadcasted_iota(I32, (2 * LANES, LANES), 0)
    q = lax.broadcasted_iota(I32, (2 * LANES, LANES), 1)
    far_bucket = _t5_bucket(jnp.full((2 * LANES, LANES), 2 * LANES, I32))
    for v, dist in enumerate((q - k, LANES + q - k)):
        bucket = _t5_bucket(dist)
        for h in range(H_A):
            bp_ref[v, h // 2, :, (h % 2) * LANES:(h % 2 + 1) * LANES] = lookup(bucket, h) - lookup(far_bucket, h)

    t = lax.broadcasted_iota(I32, (SUBLANES, LANES), 0)
    u = lax.broadcasted_iota(I32, (SUBLANES, LANES), 1)
    sdists = (PAGE + t - u, t - u, jnp.full((SUBLANES, LANES), 2 * LANES, I32))
    for m in range(3):
        bucket = _t5_bucket(sdists[m])
        for h in range(H_A):
            bs_ref[m, h * SUBLANES:(h + 1) * SUBLANES, :] = lookup(bucket, h)


def _bias_tiles(rel_bias):
    return pl.pallas_call(
        _bias_kernel,
        out_shape=[jax.ShapeDtypeStruct((2, H_A // 2, 2 * LANES, 2 * LANES), F32),
                   jax.ShapeDtypeStruct((3, H_A * SUBLANES, LANES), F32)],
        in_specs=[pl.BlockSpec(memory_space=pltpu.SMEM)],
        out_specs=[pl.BlockSpec(memory_space=pltpu.VMEM), pl.BlockSpec(memory_space=pltpu.VMEM)],
        name="t5_bias",
    )(rel_bias)


def _sortable_key(score):
    bits = pltpu.bitcast(score, I32)
    return jnp.where(bits < 0, INT_MIN - bits, bits)


def _tree_reduce(x, op):
    while x.shape[0] > 1:
        half = x.shape[0] // 2
        y = op(x[:half], x[half:2 * half])
        x = y if x.shape[0] % 2 == 0 else jnp.concatenate([y, x[2 * half:]], axis=0)
    return x[0]


def _sublane_allreduce(x, op):
    for sh in (4, 2, 1):
        x = op(x, pltpu.roll(x, sh, 0))
    return x


def _topk_maskbias_t(keys_ref, maskb_ref, far_ref, nch, cw, n_sel, idx_bits, far_limit):
    ng = cw // SUBLANES
    sub = lax.broadcasted_iota(I32, (ng, SUBLANES, LANES), 0) * SUBLANES + lax.broadcasted_iota(
        I32, (ng, SUBLANES, LANES), 1)

    def count(indicator):
        def body(c, acc):
            k0 = pl.multiple_of(c * cw, cw)
            kk = keys_ref[pl.ds(k0, cw), :].reshape(ng, SUBLANES, LANES)
            return acc + _tree_reduce(indicator(kk, k0), jnp.add)
        acc = lax.fori_loop(0, nch, body, jnp.zeros((SUBLANES, LANES), F32))
        return _sublane_allreduce(acc, jnp.add)

    def bisect(it, ans):
        cand = ans | jnp.left_shift(jnp.int32(1), 31 - it)
        cs = cand ^ INT_MIN
        cnt = count(lambda kk, k0: jnp.where(kk >= cs[None], 1.0, 0.0))
        return jnp.where(cnt >= n_sel, cand, ans)

    thr = lax.fori_loop(0, 32, bisect, jnp.zeros((SUBLANES, LANES), I32)) ^ INT_MIN
    need = n_sel - count(lambda kk, k0: jnp.where(kk > thr[None], 1.0, 0.0))
    n_eq = count(lambda kk, k0: jnp.where(kk == thr[None], 1.0, 0.0))
    has_tie = jnp.max(jnp.where(thr == INT_MIN, 0.0, jnp.where(n_eq > need, 1.0, 0.0))) > 0.0

    def tie_search(_):
        def step(it, aj):
            cand = aj | jnp.left_shift(jnp.int32(1), idx_bits - 1 - it)
            cnt = count(lambda kk, k0: jnp.where(kk == thr[None], jnp.where((k0 + sub) < cand[None], 1.0, 0.0), 0.0))
            return jnp.where(cnt < need, cand, aj)
        return lax.fori_loop(0, idx_bits, step, jnp.zeros((SUBLANES, LANES), I32))

    cut = lax.cond(has_tie, tie_search, lambda _: jnp.full((SUBLANES, LANES), 2 ** 30, I32), 0)
    cut = jnp.where(thr == INT_MIN, -1, cut)

    def fin(c, _):
        k0 = pl.multiple_of(c * cw, cw)
        kk = keys_ref[pl.ds(k0, cw), :].reshape(ng, SUBLANES, LANES)
        kpos = k0 + sub
        tie = jnp.where(kpos <= cut[None], 0.0, NEG)
        mb = jnp.where(kk > thr[None], 0.0, jnp.where(kk == thr[None], tie, NEG))
        maskb_ref[pl.ds(k0, cw), :] = mb.reshape(cw, LANES)
        if far_ref is not None:
            far_ref[pl.ds(k0, cw), :] = jnp.where(kpos < far_limit, mb, NEG).reshape(cw, LANES)
        return 0

    lax.fori_loop(0, nch, fin, 0)


def _softmax_step(carry, s, v16, v_is_transposed=False):
    m, l, acc = carry
    mn = jnp.maximum(m, jnp.max(s, axis=1, keepdims=True))
    alpha = jnp.exp(m - mn)
    p = jnp.exp(s - mn)
    l = alpha * l + jnp.sum(p, axis=1, keepdims=True)
    if v_is_transposed:
        pv = lax.dot_general(p.astype(BF16), v16, _NT, preferred_element_type=F32)
    else:
        pv = jnp.dot(p.astype(BF16), v16, preferred_element_type=F32)
    return mn, l, alpha * acc + pv


_QB = 128
_KC = 512
_FW = 512


def _prompt_attn_step(step, nsteps, i, qa_ref, qi_ref, wi_ref, ki_ref, ka_ref, vat_ref, bias_ref, o_ref,
                      keys_sc, maskb_sc, far_sc, qs_sc, q2_sc, m_sc, l_sc, acc_sc):
    lane = lax.broadcasted_iota(I32, (_QB, LANES), 1)
    lo = lane < D_I
    nch = i // (_KC // _QB) + 1
    near0 = jnp.maximum(i - 1, 0) * _QB
    npair = H_A // 2
    cols2 = 2 * _QB

    @pl.when(step == 0)
    def _():
        for h in range(H_I):
            pair = qi_ref[:, (h // 2) * LANES:(h // 2 + 1) * LANES].astype(F32)
            keep = jnp.where(lo, pair, 0.0) if h % 2 == 0 else jnp.where(lo, 0.0, pair)
            qs_sc[h * _QB:(h + 1) * _QB, :] = keep.astype(BF16)

        qpos = i * _QB + lax.broadcasted_iota(I32, (_KC, LANES), 1)

        def scores(c, _):
            k0 = pl.multiple_of(c * _KC, _KC)
            s = lax.dot_general(ki_ref[pl.ds(k0, _KC), :], qs_sc[...], _NT, preferred_element_type=F32)
            sc = jnp.zeros((_KC, LANES), F32)
            for h in range(H_I):
                sc = sc + jnp.maximum(s[:, h * _QB:(h + 1) * _QB], 0.0) * wi_ref[h:h + 1, :]
            kpos = k0 + lax.broadcasted_iota(I32, (_KC, LANES), 0)
            keys_sc[pl.ds(k0, _KC), :] = jnp.where(kpos <= qpos, _sortable_key(sc), INT_MIN)
            return 0

        lax.fori_loop(0, nch, scores, 0)
        _topk_maskbias_t(keys_sc, maskb_sc, far_sc, nch, _KC, TOPK_MAX, 12, near0)

        for p in range(npair):
            qp = qa_ref[:, p * LANES:(p + 1) * LANES].astype(F32)
            q2_sc[p * cols2:(p + 1) * cols2, :] = jnp.concatenate(
                [jnp.where(lo, qp, 0.0), jnp.where(lo, 0.0, qp)], axis=0).astype(BF16)
        m_sc[...] = jnp.full(m_sc.shape, NEG, F32)
        l_sc[...] = jnp.zeros_like(l_sc)
        acc_sc[...] = jnp.zeros_like(acc_sc)

    def update_all(k0, width, mask_ref, biases):
        mb = mask_ref[pl.ds(k0, width), :]
        mb2 = jnp.concatenate([mb, mb], axis=1)

        def qk(p):
            s = lax.dot_general(ka_ref[pl.ds(k0, width), p * LANES:(p + 1) * LANES],
                                q2_sc[p * cols2:(p + 1) * cols2, :], _NT, preferred_element_type=F32) + mb2
            if biases is not None:
                s = s + biases[p]
            return s.reshape(width // SUBLANES, SUBLANES, cols2)

        def softmax(p, s):
            m_old = m_sc[p]
            m_new = jnp.maximum(m_old, _sublane_allreduce(_tree_reduce(s, jnp.maximum), jnp.maximum))
            alpha = jnp.exp(m_old - m_new)
            pexp = jnp.exp(s - m_new[None])
            m_sc[p] = m_new
            l_sc[p] = alpha * l_sc[p] + _tree_reduce(pexp, jnp.add)
            return pexp.reshape(width, cols2).astype(BF16), alpha

        def pv(p, p16, alpha):
            out = jnp.dot(vat_ref[p * LANES:(p + 1) * LANES, pl.ds(k0, width)], p16,
                          preferred_element_type=F32)
            acc = acc_sc[p].reshape(LANES // SUBLANES, SUBLANES, cols2) * alpha[None]
            acc_sc[p] = acc.reshape(LANES, cols2) + out

        logits = [qk(p) for p in range(npair)]
        probs = [softmax(p, logits[p]) for p in range(npair)]
        for p in range(npair):
            pv(p, *probs[p])

    def far(j, _):
        update_all(pl.multiple_of((step + j * nsteps) * _FW, _FW), _FW, far_sc, None)
        return 0

    nfar = (near0 + _FW - 1) // _FW
    lax.fori_loop(0, (nfar - step + nsteps - 1) // nsteps, far, 0)

    @pl.when(step == nsteps - 1)
    def _():
        first = jnp.minimum(i, 1)
        update_all(pl.multiple_of(near0, _QB), 2 * _QB, maskb_sc, [bias_ref[first, p] for p in range(npair)])
        for p in range(npair):
            l = _sublane_allreduce(l_sc[p], jnp.add)
            o = (acc_sc[p].reshape(LANES // SUBLANES, SUBLANES, cols2) / l[None]).reshape(LANES, cols2)
            ot = jnp.concatenate([o[:D_HA, :_QB], o[D_HA:, _QB:]], axis=0)
            o_ref[:, p * LANES:(p + 1) * LANES] = ot.T


_PPS = 16
_TP = SUBLANES
_TK = 2 * SUBLANES


class _PageStream:
    def __init__(self, pt_ref, hbm_refs, bufs, sems, pages_per_group):
        self.pt_ref, self.hbm_refs, self.bufs, self.sems, self.ppg = pt_ref, hbm_refs, bufs, sems, pages_per_group

    def _copies(self, seq, group, slot):
        for hbm, buf, sem in zip(self.hbm_refs, self.bufs, self.sems):
            for p in range(self.ppg):
                yield pltpu.make_async_copy(hbm.at[self.pt_ref[seq, group * self.ppg + p]], buf.at[slot, p],
                                            sem.at[slot])

    def start(self, seq, group, slot):
        for cp in self._copies(seq, group, slot):
            cp.start()

    def wait(self, seq, group, slot):
        for cp in self._copies(seq, group, slot):
            cp.wait()

    def advance(self, step, nsteps, groups_per_seq):
        slot = lax.rem(step, 2)

        @pl.when(step == 0)
        def _():
            self.start(0, 0, 0)

        @pl.when(step + 1 < nsteps)
        def _():
            nxt = step + 1
            self.start(nxt // groups_per_seq, lax.rem(nxt, groups_per_seq), 1 - slot)

        self.wait(step // groups_per_seq, lax.rem(step, groups_per_seq), slot)
        return slot


def _sample_scores_kernel(pt_ref, q_ref, w_ref, kn_ref, cki_ref, o_ref, buf, sem):
    n_pages = buf.shape[1]
    slot = _PageStream(pt_ref, [cki_ref], [buf], [sem], n_pages).advance(pl.program_id(0), pl.num_programs(0), 1)
    q = q_ref[...]

    t_dec = o_ref.shape[0]

    def head_sum(s):
        n = s.shape[1]
        out = jnp.zeros((_TP, n), F32)
        for h in range(H_I):
            wh = w_ref[h * _TP:(h + 1) * _TP, :]
            out = out + jnp.maximum(s[h * _TP:(h + 1) * _TP, :], 0.0) * jnp.concatenate([wh] * (n // LANES), axis=1)
        return out[:t_dec]

    for c in range(n_pages // _PPS):
        kc = jnp.concatenate([buf[slot, c * _PPS + p].astype(BF16) for p in range(_PPS)], axis=1)
        s = jnp.dot(q, kc, preferred_element_type=F32)
        o_ref[:, c * _PPS * PAGE:(c + 1) * _PPS * PAGE] = head_sum(s)

    kn = jnp.concatenate([kn_ref[...], jnp.zeros((LANES - _TK, D_I), BF16)], axis=0)
    sn = lax.dot_general(q, kn, _NT, preferred_element_type=F32)
    o_ref[:, o_ref.shape[1] - LANES:] = head_sum(sn)


def _sample_scores(page_table, q_st, w_b, ki_new, cache_kidx, n_pages, t_dec):
    nseq = q_st.shape[0]
    lpad = n_pages * PAGE + LANES
    return pl.pallas_call(
        _sample_scores_kernel,
        out_shape=jax.ShapeDtypeStruct((nseq, t_dec, lpad), F32),
        grid_spec=pltpu.PrefetchScalarGridSpec(
            num_scalar_prefetch=1, grid=(nseq,),
            in_specs=[pl.BlockSpec((None, H_I * _TP, D_I), lambda b, pt: (b, 0, 0)),
                      pl.BlockSpec((None, H_I * _TP, LANES), lambda b, pt: (b, 0, 0)),
                      pl.BlockSpec((None, _TK, D_I), lambda b, pt: (b, 0, 0)),
                      pl.BlockSpec(memory_space=pl.ANY)],
            out_specs=pl.BlockSpec((None, t_dec, lpad), lambda b, pt: (b, 0, 0)),
            scratch_shapes=[pltpu.VMEM((2, n_pages, D_I, PAGE), F32), pltpu.SemaphoreType.DMA((2,))]),
        compiler_params=_cp(("arbitrary",)),
        name="sample_scores",
    )(page_table, q_st, w_b, ki_new, cache_kidx)


_SEL_CW = 640


def _sample_select_kernel(s_ref, o_ref, keys_sc, *, past, t_dec):
    lpad = s_ref.shape[0]
    assert LANES % t_dec == 0
    limit = past + lax.broadcasted_iota(I32, (_SEL_CW, LANES), 1) % t_dec
    for c in range(lpad // _SEL_CW):
        sl = slice(c * _SEL_CW, (c + 1) * _SEL_CW)
        kpos = c * _SEL_CW + lax.broadcasted_iota(I32, (_SEL_CW, LANES), 0)
        keys_sc[sl, :] = jnp.where(kpos <= limit, _sortable_key(s_ref[sl, :]), INT_MIN)
    _topk_maskbias_t(keys_sc, o_ref, None, lpad // _SEL_CW, _SEL_CW, TOPK_MAX, 14, None)


def _sample_select(scores_t, past, t_dec):
    lpad, cols = scores_t.shape
    return pl.pallas_call(
        functools.partial(_sample_select_kernel, past=past, t_dec=t_dec),
        out_shape=jax.ShapeDtypeStruct((lpad, cols), F32),
        grid=(cols // LANES,),
        in_specs=[pl.BlockSpec((lpad, LANES), lambda g: (0, g))],
        out_specs=pl.BlockSpec((lpad, LANES), lambda g: (0, g)),
        scratch_shapes=[pltpu.VMEM((lpad, LANES), I32)],
        compiler_params=_cp(("arbitrary",)),
        name="sample_select",
    )(scores_t)


def _sample_attn_step(s_id, last, q_ref, mb_ref, kn_ref, vn_ref, bias_ref, kp, vp, o_ref,
                      qblk_sc, m_sc, l_sc, acc_sc):
    rows = H_A * _TP
    span = _PPS * PAGE

    @pl.when(s_id == 0)
    def _():
        q = jnp.concatenate([q_ref[...]] * H_A, axis=0)
        r = lax.broadcasted_iota(I32, (rows, W_ATTN), 0) // _TP
        c = lax.broadcasted_iota(I32, (rows, W_ATTN), 1) // D_HA
        qblk_sc[...] = jnp.where(r == c, q, 0.0).astype(BF16)
        m_sc[...] = jnp.full(m_sc.shape, NEG, F32)
        l_sc[...] = jnp.zeros_like(l_sc)
        acc_sc[...] = jnp.zeros_like(acc_sc)

    def update(s, v16, v_is_transposed):
        carry = (m_sc[:, 0:1], l_sc[:, 0:1], acc_sc[...])
        m, l, acc = _softmax_step(carry, s, v16, v_is_transposed)
        m_sc[...] = jnp.broadcast_to(m, m_sc.shape)
        l_sc[...] = jnp.broadcast_to(l, l_sc.shape)
        acc_sc[...] = acc

    k16 = jnp.concatenate([pg[...].astype(BF16) for pg in kp], axis=1)
    v16 = jnp.concatenate([pg[...].astype(BF16) for pg in vp], axis=1)
    s = jnp.dot(qblk_sc[...], k16, preferred_element_type=F32)
    mb = mb_ref[:, pl.ds(pl.multiple_of(s_id * span, span), span)]
    last_tile = jnp.where(s_id == last, bias_ref[0], bias_ref[2])
    bias = jnp.concatenate([bias_ref[2]] * (_PPS - 1) + [last_tile], axis=1)
    update(s + jnp.concatenate([mb] * H_A, axis=0) + bias, v16, True)

    @pl.when(s_id == last)
    def _():
        pad = jnp.zeros((LANES - _TK, W_ATTN), BF16)
        kn = jnp.concatenate([kn_ref[...], pad], axis=0)
        vn = jnp.concatenate([vn_ref[...], pad], axis=0)
        sn = lax.dot_general(qblk_sc[...], kn, _NT, preferred_element_type=F32)
        mbn = mb_ref[:, mb_ref.shape[1] - LANES:]
        update(sn + jnp.concatenate([mbn] * H_A, axis=0) + bias_ref[1], vn, False)
        o = acc_sc[...] / l_sc[:, 0:1]
        c = lax.broadcasted_iota(I32, (_TP, W_ATTN), 1) // D_HA
        out = jnp.zeros((_TP, W_ATTN), F32)
        for h in range(H_A):
            out = jnp.where(c == h, o[h * _TP:(h + 1) * _TP, :], out)
        o_ref[...] = out


def _attn_kernel(pt_ref, qa_ref, qi_ref, wi_ref, ki_ref, ka_ref, vat_ref, bias_p_ref,
                 sq_ref, smb_ref, skn_ref, svn_ref, bias_s_ref, ck_ref, cv_ref, o_ref, so_ref, *scratch,
                 nq, nsteps, nseq):
    prompt_scratch, sample_scratch = scratch[:8], scratch[8:12]
    kbuf, vbuf, ksem, vsem = scratch[12:]
    step = pl.program_id(2)
    seq_id = pl.program_id(0) * nq + pl.program_id(1)
    stream = _PageStream(pt_ref, [ck_ref, cv_ref], [kbuf, vbuf], [ksem, vsem], _PPS)
    slot = stream.advance(seq_id * nsteps + step, nseq * nsteps, nsteps)
    _prompt_attn_step(step, nsteps, pl.program_id(1), qa_ref, qi_ref, wi_ref, ki_ref, ka_ref, vat_ref, bias_p_ref,
                      o_ref, *prompt_scratch)
    _sample_attn_step(step, nsteps - 1, sq_ref, smb_ref, skn_ref, svn_ref, bias_s_ref,
                      [kbuf.at[slot, p] for p in range(_PPS)], [vbuf.at[slot, p] for p in range(_PPS)], so_ref,
                      *sample_scratch)


def _attention(qa16, qi16, wi_t, ki16, ka16, va16_t, bias_p, nbatch, seq,
               page_table, q8, maskb, k_new, v_new, bias_s, cache_k, cache_v, n_pages):
    nq = seq // _QB
    nseq = q8.shape[0]
    steps = n_pages // _PPS
    lpad = maskb.shape[1]
    npair = H_A // 2
    assert nseq == nbatch * nq, "one decode sequence per prompt query block"
    blk = lambda w: pl.BlockSpec((_QB, w), lambda b, i, s, pt: (b * nq + i, 0))
    once = pl.Buffered(1)
    per_b = lambda w: pl.BlockSpec((seq, w), lambda b, i, s, pt: (b, 0), pipeline_mode=once)
    seq3 = lambda t: pl.BlockSpec((None, t, W_ATTN), lambda b, i, s, pt: (b * nq + i, 0, 0))
    hbm = pl.BlockSpec(memory_space=pl.ANY)
    return pl.pallas_call(
        functools.partial(_attn_kernel, nq=nq, nsteps=steps, nseq=nseq),
        out_shape=[jax.ShapeDtypeStruct((nbatch * seq, W_ATTN), F32),
                   jax.ShapeDtypeStruct((nseq, _TP, W_ATTN), F32)],
        grid_spec=pltpu.PrefetchScalarGridSpec(
            num_scalar_prefetch=1, grid=(nbatch, nq, steps),
            in_specs=[blk(W_ATTN), blk(H_I * D_I),
                      pl.BlockSpec((H_I, _QB), lambda b, i, s, pt: (0, b * nq + i)),
                      per_b(LANES), per_b(W_ATTN),
                      pl.BlockSpec((None, W_ATTN, seq), lambda b, i, s, pt: (b, 0, 0), pipeline_mode=once),
                      pl.BlockSpec(bias_p.shape, lambda b, i, s, pt: (0, 0, 0, 0), pipeline_mode=once),
                      seq3(_TP), pl.BlockSpec((_TP, lpad), lambda b, i, s, pt: (b * nq + i, 0)),
                      seq3(_TK), seq3(_TK),
                      pl.BlockSpec(bias_s.shape, lambda b, i, s, pt: (0, 0, 0), pipeline_mode=once),
                      hbm, hbm],
            out_specs=[blk(W_ATTN), seq3(_TP)],
            scratch_shapes=[pltpu.VMEM((seq, LANES), I32), pltpu.VMEM((seq, LANES), F32),
                            pltpu.VMEM((seq, LANES), F32),
                            pltpu.VMEM((H_I * _QB, LANES), BF16), pltpu.VMEM((H_A * _QB, LANES), BF16),
                            pltpu.VMEM((npair, SUBLANES, 2 * _QB), F32), pltpu.VMEM((npair, SUBLANES, 2 * _QB), F32),
                            pltpu.VMEM((npair, LANES, 2 * _QB), F32),
                            pltpu.VMEM((H_A * _TP, W_ATTN), BF16), pltpu.VMEM((H_A * _TP, LANES), F32),
                            pltpu.VMEM((H_A * _TP, LANES), F32), pltpu.VMEM((H_A * _TP, W_ATTN), F32),
                            pltpu.VMEM((2, _PPS, W_ATTN, PAGE), F32), pltpu.VMEM((2, _PPS, W_ATTN, PAGE), F32),
                            pltpu.SemaphoreType.DMA((2,)), pltpu.SemaphoreType.DMA((2,))]),
        compiler_params=_cp(("arbitrary", "arbitrary", "arbitrary")),
        name="attention",
    )(page_table, qa16, qi16, wi_t, ki16, ka16, va16_t, bias_p, q8, maskb, k_new, v_new, bias_s, cache_k, cache_v)


def _hgrn_chunk(q, k, v, logf, st, c):
    blk = SUBLANES // 2 if c > SUBLANES else SUBLANES
    row = lax.broadcasted_iota(I32, (c, DK_B), 0)
    g = logf
    sh = 1
    while sh < c:
        g = g + jnp.where(row >= sh, pltpu.roll(g, sh, 0), 0.0)
        sh *= 2

    o = lax.dot_general((q * jnp.exp(g)).astype(BF16), st.astype(BF16), _NT, preferred_element_type=F32)

    if c > SUBLANES:
        rr = lax.broadcasted_iota(I32, (c, c), 0)
        cc = lax.broadcasted_iota(I32, (c, c), 1)
        a = jnp.zeros((c, c), F32)
        w = blk
        while w < c:
            nb = c // (2 * w)
            gb = jnp.broadcast_to(g.reshape(nb, 2 * w, DK_B)[:, w - 1:w, :], (nb, 2 * w, DK_B)).reshape(c, DK_B)
            right = (row % (2 * w)) >= w
            qt = jnp.where(right, q * jnp.exp(jnp.where(right, g - gb, 0.0)), 0.0)
            kt = jnp.where(right, 0.0, k * jnp.exp(jnp.where(right, 0.0, gb - g)))
            aw = lax.dot_general(qt.astype(BF16), kt.astype(BF16), _NT, preferred_element_type=F32)
            a = a + jnp.where((rr // (2 * w)) == (cc // (2 * w)), aw, 0.0)
            w *= 2
        o = o + jnp.dot(a.astype(BF16), v.astype(BF16), preferred_element_type=F32)

    o = o + jnp.sum(q * k, axis=1, keepdims=True) * v
    for d in range(1, blk):
        ok = (row % blk) >= d
        e = jnp.exp(jnp.where(ok, g - pltpu.roll(g, d, 0), 0.0))
        coef = jnp.sum(jnp.where(ok, q * pltpu.roll(k, d, 0) * e, 0.0), axis=1, keepdims=True)
        o = o + coef * pltpu.roll(v, d, 0)

    g_end = g[c - 1:c, :]
    kd = k * jnp.exp(g_end - g)
    st = st * jnp.exp(g_end) + lax.dot_general(v.astype(BF16), kd.astype(BF16), _TN, preferred_element_type=F32)
    return o, st


_HT = 1024


def _hgrn_kernel(q_ref, f_ref, v_ref, s0_ref, o_ref, s_ref, st_sc, *, nb, tb, c):
    tstep = pl.program_id(1)

    @pl.when(tstep == 0)
    def _():
        for j in range(nb):
            for g in range(H_B):
                st_sc[j, g] = s0_ref[j, g].T

    for j in range(nb):
        def body(ci, states):
            r = pl.ds(pl.multiple_of(ci * c, c), c)
            new_states = []
            for g in range(H_B):
                cs = slice(g * DK_B, (g + 1) * DK_B)
                f = f_ref[j, r, cs]
                o, st = _hgrn_chunk(q_ref[j, r, cs], 1.0 - f, v_ref[j, r, cs], jnp.log(f), states[g], c)
                o_ref[j, r, cs] = o
                new_states.append(st)
            return tuple(new_states)
        states = lax.fori_loop(0, tb // c, body, tuple(st_sc[j, g] for g in range(H_B)))
        for g in range(H_B):
            st_sc[j, g] = states[g]

    @pl.when(tstep == pl.num_programs(1) - 1)
    def _():
        for j in range(nb):
            for g in range(H_B):
                s_ref[j, g] = st_sc[j, g].T


def _hgrn(q, f, v, s0, nb, c):
    nseq, t_len, _ = q.shape
    tb = min(t_len, _HT)
    tok = pl.BlockSpec((nb, tb, W_HGRN), lambda b, t: (b, t, 0))
    st = pl.BlockSpec((nb, H_B, DK_B, DV_B), lambda b, t: (b, 0, 0, 0))
    return pl.pallas_call(
        functools.partial(_hgrn_kernel, nb=nb, tb=tb, c=c),
        out_shape=[jax.ShapeDtypeStruct((nseq, t_len, W_HGRN), F32),
                   jax.ShapeDtypeStruct((nseq, H_B, DK_B, DV_B), F32)],
        grid=(nseq // nb, t_len // tb),
        in_specs=[tok, tok, tok, st],
        out_specs=[tok, st],
        scratch_shapes=[pltpu.VMEM((nb, H_B, DV_B, DK_B), F32)],
        compiler_params=_cp(("arbitrary", "arbitrary")),
        name="hgrn",
    )(q, f, v, s0)


def _pad_tokens(a, value=0.0, to=_TP):
    return jnp.pad(a, ((0, 0), (0, to - a.shape[1]), (0, 0)), constant_values=value)


def kernel(x_prompt, x_sample, cache_k, cache_v, cache_kidx, state_hgrn, page_table, c_prompt, c_sample,
           w_ada, b_ada, g_ffn1, ffn1_w_gate, ffn1_w_up, ffn1_w_down, g_mix, w_in, g_q, g_k,
           beta_attn, g_hgrn, w_out, g_ffn2, ffn2_w_gate, ffn2_w_up, ffn2_w_down, rel_bias, lb_logits):
    assert w_ada.shape[0] == 1, "single-layer problem"
    nb_p, seq, _ = x_prompt.shape
    nb_s, t_dec, _ = x_sample.shape
    n_phys = cache_k.shape[1]
    n_pages = page_table.shape[1]
    past = n_pages * PAGE
    assert t_dec <= _TP and n_pages % _PPS == 0 and (past + LANES) % _SEL_CW == 0

    row2 = lambda a: a.reshape(1, -1)
    w_in0 = w_in[0]
    pad_w = jnp.zeros((D_MODEL, LANES - H_I), F32)
    wpack = jnp.concatenate(
        [w_in0[:, :2048], w_in0[:, 2048:2112], w_in0[:, 2048:2112], w_in0[:, 2112:2120], pad_w,
         w_in0[:, 2120:]], axis=1).astype(BF16)
    assert wpack.shape[1] == _N_PACK
    gq = row2(jnp.tile(g_q[0], H_A))
    gk = row2(jnp.tile(g_k[0], H_A))
    gidx = jnp.arange(W_ATTN) // D_HA
    bd = jnp.where(gidx[:, None] == gidx[None, :], 1.0 / D_HA, 0.0).astype(BF16)
    ffn1 = (row2(g_ffn1[0]), ffn1_w_gate[0].astype(BF16), ffn1_w_up[0].astype(BF16), ffn1_w_down[0].astype(BF16))
    ffn2 = (row2(g_ffn2[0]), ffn2_w_gate[0].astype(BF16), ffn2_w_up[0].astype(BF16), ffn2_w_down[0].astype(BF16))
    w_out16 = w_out[0].astype(BF16)

    n_c = nb_p + nb_s
    c_pad = (-n_c) % SUBLANES
    c_all = jnp.concatenate([c_prompt, c_sample, jnp.zeros((c_pad, D_MODEL), F32)], axis=0)
    mods = _ada(c_all, w_ada[0], row2(b_ada[0]))
    bias_p, bias_s = _bias_tiles(rel_bias)

    def front(x, mod, tm, tf, rows_per_group):
        x1 = _ffn(x, mod, 0, *ffn1, tm, tf)
        return x1, _mix(x1, mod, row2(g_mix[0]), wpack, gq, gk, lb_logits, bd, tm, rows_per_group)

    def back(x1, mod, tm, tf, oa, oh, sg):
        return _merge_ffn(oa, oh, sg, x1, mod, row2(beta_attn[0]), row2(g_hgrn[0]), bd, w_out16, *ffn2, tm, tf)

    tm_p, tf = 512, D_FF // 2
    n_tok = nb_s * t_dec
    mod_p = _Mod(mods[:, :nb_p].reshape(N_MOD, nb_p, 1, D_MODEL), False, seq // tm_p)
    mod_s = _Mod(jnp.repeat(mods[:, nb_p:nb_p + nb_s], t_dec, axis=1), True)
    x1p, proj_p = front(x_prompt.reshape(nb_p * seq, D_MODEL), mod_p, tm_p, tf, seq)
    x1s, proj_s = front(x_sample.reshape(n_tok, D_MODEL), mod_s, n_tok, tf, n_tok)
    qa16, kp_t, ka16, vp_t, va16_t, qi16, ki_p, ki16, wi, qb, f, vb, sg_p = proj_p
    qa16s, ks_t, ka16s, vs_t, va16s_t, qi16s, ki_s, ki16s, wis, qbs, fs, vbs, sg_s = proj_s
    va16s = va16s_t[0].T

    ck = jnp.transpose(cache_k[0], (0, 2, 3, 1)).reshape(n_phys, W_ATTN, PAGE)
    cv = jnp.transpose(cache_v[0], (0, 2, 3, 1)).reshape(n_phys, W_ATTN, PAGE)
    cki = jnp.transpose(cache_kidx[0], (0, 2, 1))
    seq3 = lambda a: a.reshape(nb_s, t_dec, a.shape[-1])
    q_st = _pad_tokens(seq3(qi16s)).reshape(nb_s, _TP, H_I, D_I).transpose(0, 2, 1, 3).reshape(nb_s, H_I * _TP, D_I)
    w_st = _pad_tokens(seq3(wis[:, :H_I])).transpose(0, 2, 1).reshape(nb_s, H_I * _TP, 1)
    w_b = jnp.broadcast_to(w_st, (nb_s, H_I * _TP, LANES))
    scores = _sample_scores(page_table, q_st, w_b, _pad_tokens(seq3(ki16s[:, :D_I]), to=_TK), cki, n_pages, t_dec)
    mask_t = _sample_select(scores.reshape(n_tok, -1).T, past, t_dec)
    maskb = _pad_tokens(mask_t.T.reshape(nb_s, t_dec, -1), NEG).reshape(nb_s * _TP, -1)

    oa_p, oa8 = _attention(qa16, qi16, wi[:, :H_I].T, ki16, ka16, va16_t, bias_p, nb_p, seq,
                           page_table, _pad_tokens(seq3(qa16s.astype(F32))), maskb,
                           _pad_tokens(seq3(ka16s), to=_TK), _pad_tokens(seq3(va16s), to=_TK), bias_s, ck, cv, n_pages)
    r3 = lambda a: a.reshape(nb_p, seq, W_HGRN)
    oh_p, sp = _hgrn(r3(qb), r3(f), r3(vb), jnp.zeros((nb_p, H_B, DK_B, DV_B), F32), 1, 64)
    oh8, ss = _hgrn(_pad_tokens(seq3(qbs)), _pad_tokens(seq3(fs), 1.0), _pad_tokens(seq3(vbs)), state_hgrn[0], 8, _TP)

    yp = back(x1p, mod_p, tm_p, tf, oa_p, oh_p.reshape(nb_p * seq, W_HGRN), sg_p)
    ys = back(x1s, mod_s, n_tok, tf, oa8[:, :t_dec].reshape(n_tok, W_ATTN),
              oh8[:, :t_dec].reshape(n_tok, W_HGRN), sg_s)

    def rows_out(a_t, nseq, t_len):
        a = a_t.reshape(-1, H_A, D_HA, a_t.shape[-1]).transpose(0, 3, 1, 2)
        return a.reshape(1, nseq, t_len, H_A, D_HA)

    return (yp.reshape(nb_p, seq, D_MODEL).astype(x_prompt.dtype),
            ys.reshape(nb_s, t_dec, D_MODEL).astype(x_sample.dtype),
            rows_out(kp_t, nb_p, seq), rows_out(vp_t, nb_p, seq),
            ki_p[:, :D_I].reshape(1, nb_p, seq, D_I), sp[None],
            rows_out(ks_t, nb_s, t_dec), rows_out(vs_t, nb_s, t_dec),
            ki_s[:, :D_I].reshape(1, nb_s, t_dec, D_I), ss[None])
```

```python
import functools
import math

import jax
import jax.numpy as jnp
from jax import lax
from jax.experimental import pallas as pl
from jax.experimental.pallas import tpu as pltpu

F32 = jnp.float32
BF16 = jnp.bfloat16
I32 = jnp.int32

D_MODEL = 1024
W_ATTN = 512
W_HGRN = 512
D_HA = 64
H_A = 8
H_I = 8
D_I = 64
TOPK_MAX = 256
H_B = 4
DK_B = 128
DV_B = 128
D_FF = 2816
T5_BUCKETS = 32
T5_MAX_DIST = 128
RMS_EPS = 1e-6
N_MOD = 9
PAGE = 128

LANES = 128
SUBLANES = 8
NEG = -1e30
INT_MIN = -(2 ** 31)
VMEM_LIMIT = 56 * 1024 * 1024

_NT = (((1,), (1,)), ((), ()))
_TN = (((0,), (0,)), ((), ()))


def _cp(sem):
    return pltpu.CompilerParams(dimension_semantics=sem, vmem_limit_bytes=VMEM_LIMIT)


def _silu(x):
    return x / (1.0 + jnp.exp(-x))


def _sigmoid(x):
    return 1.0 / (1.0 + jnp.exp(-x))


def _rms(x, g):
    return x * lax.rsqrt(jnp.mean(x * x, axis=-1, keepdims=True) + RMS_EPS) * g


def _group_mean_sq(x, bd):
    xx = x * x
    hi = xx.astype(BF16)
    lo = (xx - hi.astype(F32)).astype(BF16)
    return (jnp.dot(hi, bd, preferred_element_type=F32)
            + jnp.dot(lo, bd, preferred_element_type=F32))


def _ada_kernel(c_ref, w_ref, b_ref, o_ref):
    a = _silu(c_ref[...]).astype(BF16)
    o_ref[...] = jnp.dot(a, w_ref[...].astype(BF16), preferred_element_type=F32) + b_ref[...]


def _ada(c_all, w, b):
    rows = c_all.shape[0]
    return pl.pallas_call(
        _ada_kernel,
        out_shape=jax.ShapeDtypeStruct((N_MOD, rows, D_MODEL), F32),
        grid=(N_MOD,),
        in_specs=[pl.BlockSpec((rows, D_MODEL), lambda j: (0, 0)),
                  pl.BlockSpec((D_MODEL, D_MODEL), lambda j: (0, j)),
                  pl.BlockSpec((1, D_MODEL), lambda j: (0, j))],
        out_specs=pl.BlockSpec((None, rows, D_MODEL), lambda j: (j, 0, 0)),
        compiler_params=_cp(("arbitrary",)),
        name="ada",
    )(c_all, w, b)


class _Mod:
    def __init__(self, arr, per_token, tiles_per_seq=1):
        self.arr = arr
        self.per_token = per_token
        self.tiles_per_seq = tiles_per_seq

    def spec(self, j, tm, ngrid):
        if self.per_token:
            if ngrid == 1:
                return pl.BlockSpec((None, tm, D_MODEL), lambda i: (j, i, 0))
            return pl.BlockSpec((None, tm, D_MODEL), lambda i, k: (j, i, 0))
        tps = self.tiles_per_seq
        if ngrid == 1:
            return pl.BlockSpec((None, None, 1, D_MODEL), lambda i: (j, i // tps, 0, 0))
        return pl.BlockSpec((None, None, 1, D_MODEL), lambda i, k: (j, i // tps, 0, 0))


def _ffn_steps(load_x, x_res, sh_ref, sc_ref, gt_ref, g_ref, wg_ref, wu_ref, wd_ref, o_ref, h_sc, acc_sc):
    k = pl.program_id(1)

    @pl.when(k == 0)
    def _():
        h = _rms(load_x(), g_ref[...]) * (1.0 + sc_ref[...]) + sh_ref[...]
        h_sc[...] = h.astype(BF16)
        acc_sc[...] = jnp.zeros_like(acc_sc)

    h = h_sc[...]
    a = jnp.dot(h, wg_ref[...], preferred_element_type=F32)
    b = jnp.dot(h, wu_ref[...], preferred_element_type=F32)
    act = (_silu(a) * b).astype(BF16)
    acc_sc[...] += jnp.dot(act, wd_ref[...], preferred_element_type=F32)

    @pl.when(k == pl.num_programs(1) - 1)
    def _():
        o_ref[...] = x_res[...] + 0.5 * gt_ref[...] * acc_sc[...]


def _ffn_kernel(x_ref, *rest):
    _ffn_steps(lambda: x_ref[...], x_ref, *rest)


def _merge_ffn_kernel(oa_ref, oh_ref, sg_ref, x_ref, gtm_ref, beta_ref, gh_ref, bd_ref, wo_ref, *rest):
    x_sc = rest[-1]

    def load_x():
        oa = oa_ref[...]
        a = oa * lax.rsqrt(_group_mean_sq(oa, bd_ref[...]) + RMS_EPS) * beta_ref[...]
        oh = oh_ref[...]
        parts = []
        for h in range(H_B):
            sl = slice(h * DV_B, (h + 1) * DV_B)
            parts.append(_rms(oh[:, sl], gh_ref[:, sl]))
        r = jnp.concatenate(parts, axis=1) * sg_ref[...]
        y = (jnp.dot(a.astype(BF16), wo_ref[:W_ATTN, :], preferred_element_type=F32)
             + jnp.dot(r.astype(BF16), wo_ref[W_ATTN:, :], preferred_element_type=F32))
        x_sc[...] = x_ref[...] + gtm_ref[...] * y
        return x_sc[...]

    _ffn_steps(load_x, x_sc, *rest[:-1])


def _ffn_specs(mod, j0, tm, tf):
    return [mod.spec(j0, tm, 2), mod.spec(j0 + 1, tm, 2), mod.spec(j0 + 2, tm, 2),
            pl.BlockSpec((1, D_MODEL), lambda i, k: (0, 0)),
            pl.BlockSpec((D_MODEL, tf), lambda i, k: (0, k)),
            pl.BlockSpec((D_MODEL, tf), lambda i, k: (0, k)),
            pl.BlockSpec((tf, D_MODEL), lambda i, k: (k, 0))]


def _ffn(x, mod, j0, g, wg, wu, wd, tm, tf):
    n = x.shape[0]
    row = pl.BlockSpec((tm, D_MODEL), lambda i, k: (i, 0))
    return pl.pallas_call(
        _ffn_kernel,
        out_shape=jax.ShapeDtypeStruct((n, D_MODEL), F32),
        grid=(n // tm, D_FF // tf),
        in_specs=[row] + _ffn_specs(mod, j0, tm, tf),
        out_specs=row,
        scratch_shapes=[pltpu.VMEM((tm, D_MODEL), BF16), pltpu.VMEM((tm, D_MODEL), F32)],
        compiler_params=_cp(("arbitrary", "arbitrary")),
        name="ffn",
    )(x, mod.arr, mod.arr, mod.arr, g, wg, wu, wd)


def _merge_ffn(oa, oh, sg, x, mod, beta, gh, bd, wo16, g, wg, wu, wd, tm, tf):
    n = x.shape[0]
    row = lambda w: pl.BlockSpec((tm, w), lambda i, k: (i, 0))
    full = lambda a: pl.BlockSpec(a.shape, lambda i, k: (0,) * a.ndim)
    return pl.pallas_call(
        _merge_ffn_kernel,
        out_shape=jax.ShapeDtypeStruct((n, D_MODEL), F32),
        grid=(n // tm, D_FF // tf),
        in_specs=[row(W_ATTN), row(W_HGRN), row(W_HGRN), row(D_MODEL), mod.spec(5, tm, 2),
                  full(beta), full(gh), full(bd), full(wo16)] + _ffn_specs(mod, 6, tm, tf),
        out_specs=row(D_MODEL),
        scratch_shapes=[pltpu.VMEM((tm, D_MODEL), BF16), pltpu.VMEM((tm, D_MODEL), F32),
                        pltpu.VMEM((tm, D_MODEL), F32)],
        compiler_params=_cp(("arbitrary", "arbitrary")),
        name="merge_ffn",
    )(oa, oh, sg, x, mod.arr, beta, gh, bd, wo16, mod.arr, mod.arr, mod.arr, g, wg, wu, wd)


_C_QA, _C_KA, _C_VA, _C_QI = 0, 512, 1024, 1536
_C_KI, _C_WI = 2048, 2176
_C_QB, _C_FB, _C_IB, _C_GB = 2304, 2816, 3328, 3840
_N_PACK = 4352


def _mix_kernel(x_ref, sh_ref, sc_ref, g_ref, w_ref, gq_ref, gk_ref, lbl_ref, bd_ref,
                qa_o, ka_o, ka16_o, va_o, va16_o, qi_o, ki_o, ki16_o, wi_o, qb_o, f_o, vb_o, sg_o):
    h = (_rms(x_ref[...], g_ref[...]) * (1.0 + sc_ref[...]) + sh_ref[...]).astype(BF16)

    def proj(c0, width):
        return jnp.dot(h, w_ref[:, c0:c0 + width], preferred_element_type=F32)

    bd = bd_ref[...]
    qa = proj(_C_QA, W_ATTN)
    qa = qa * lax.rsqrt(_group_mean_sq(qa, bd) + RMS_EPS) * gq_ref[...]
    qa_o[...] = (qa * (D_HA ** -0.5)).astype(BF16)
    ka = proj(_C_KA, W_ATTN)
    ka = ka * lax.rsqrt(_group_mean_sq(ka, bd) + RMS_EPS) * gk_ref[...]
    ka_o[...] = ka.T
    ka16_o[...] = ka.astype(BF16)
    va_t = proj(_C_VA, W_ATTN).T
    va_o[...] = va_t
    va16_o[...] = va_t.astype(BF16)
    qi_o[...] = (proj(_C_QI, H_I * D_I) * (D_I ** -0.5)).astype(BF16)
    ki = proj(_C_KI, LANES)
    ki_o[...] = ki
    ki16_o[...] = ki.astype(BF16)
    wi_o[...] = proj(_C_WI, LANES) * (H_I ** -0.5)
    qb_o[...] = _silu(proj(_C_QB, W_HGRN))
    l0 = lbl_ref[0:1, :]
    l1 = lbl_ref[1:2, :]
    mx = jnp.maximum(l0, l1)
    e0 = jnp.exp(l0 - mx)
    e1 = jnp.exp(l1 - mx)
    lb = e0 / (e0 + e1)
    f_o[...] = lb + (1.0 - lb) * _sigmoid(proj(_C_FB, W_HGRN))
    vb_o[...] = proj(_C_IB, W_HGRN)
    sg_o[...] = _silu(proj(_C_GB, W_HGRN))


def _mix(x, mod, g, wpack, gq, gk, lbl, bd, tm, rows_per_group):
    n = x.shape[0]
    tpg = rows_per_group // tm
    row = lambda w: pl.BlockSpec((tm, w), lambda i: (i, 0))
    full = lambda a: pl.BlockSpec(a.shape, lambda i: (0,) * a.ndim)
    featmajor = pl.BlockSpec((None, W_ATTN, tm), lambda i: (i // tpg, 0, i % tpg))
    outs = [(W_ATTN, BF16), None, (W_ATTN, BF16), None, None,
            (H_I * D_I, BF16), (LANES, F32), (LANES, BF16), (LANES, F32),
            (W_HGRN, F32), (W_HGRN, F32), (W_HGRN, F32), (W_HGRN, F32)]
    fm_dtypes = {1: F32, 3: F32, 4: BF16}
    return pl.pallas_call(
        _mix_kernel,
        out_shape=[jax.ShapeDtypeStruct((n // rows_per_group, W_ATTN, rows_per_group), fm_dtypes[k]) if o is None
                   else jax.ShapeDtypeStruct((n, o[0]), o[1]) for k, o in enumerate(outs)],
        grid=(n // tm,),
        in_specs=[row(D_MODEL), mod.spec(3, tm, 1), mod.spec(4, tm, 1), full(g), full(wpack),
                  full(gq), full(gk), full(lbl), full(bd)],
        out_specs=[featmajor if o is None else row(o[0]) for o in outs],
        compiler_params=_cp(("arbitrary",)),
        name="mix",
    )(x, mod.arr, mod.arr, g, wpack, gq, gk, lbl, bd)


def _t5_bucket(dist):
    n = jnp.maximum(dist, 0)
    max_exact = T5_BUCKETS // 2
    nf = jnp.maximum(n, 1).astype(F32)
    large = max_exact + (jnp.log(nf / max_exact) / math.log(T5_MAX_DIST / max_exact)
                         * (T5_BUCKETS - max_exact)).astype(I32)
    large = jnp.minimum(large, T5_BUCKETS - 1)
    return jnp.where(n < max_exact, n, large)


def _bias_kernel(rb_ref, bp_ref, bs_ref):
    def lookup(bucket, h):
        val = jnp.zeros(bucket.shape, F32)
        for b in range(T5_BUCKETS):
            val = jnp.where(bucket == b, rb_ref[b, h], val)
        return val

    k = lax.broadcasted_iota(I32, (2 * LANES, LANES), 0)
    q = lax.broadcasted_iota(I32, (2 * LANES, LANES), 1)
    far_bucket = _t5_bucket(jnp.full((2 * LANES, LANES), 2 * LANES, I32))
    for v, dist in enumerate((q - k, LANES + q - k)):
        bucket = _t5_bucket(dist)
        for h in range(H_A):
            bp_ref[v, h // 2, :, (h % 2) * LANES:(h % 2 + 1) * LANES] = lookup(bucket, h) - lookup(far_bucket, h)

    t = lax.broadcasted_iota(I32, (SUBLANES, LANES), 0)
    u = lax.broadcasted_iota(I32, (SUBLANES, LANES), 1)
    sdists = (PAGE + t - u, t - u, jnp.full((SUBLANES, LANES), 2 * LANES, I32))
    for m in range(3):
        bucket = _t5_bucket(sdists[m])
        for h in range(H_A):
            bs_ref[m, h * SUBLANES:(h + 1) * SUBLANES, :] = lookup(bucket, h)


def _bias_tiles(rel_bias):
    return pl.pallas_call(
        _bias_kernel,
        out_shape=[jax.ShapeDtypeStruct((2, H_A // 2, 2 * LANES, 2 * LANES), F32),
                   jax.ShapeDtypeStruct((3, H_A * SUBLANES, LANES), F32)],
        in_specs=[pl.BlockSpec(memory_space=pltpu.SMEM)],
        out_specs=[pl.BlockSpec(memory_space=pltpu.VMEM), pl.BlockSpec(memory_space=pltpu.VMEM)],
        name="t5_bias",
    )(rel_bias)


def _sortable_key(score):
    bits = pltpu.bitcast(score, I32)
    return jnp.where(bits < 0, INT_MIN - bits, bits)


def _tree_reduce(x, op):
    while x.shape[0] > 1:
        half = x.shape[0] // 2
        y = op(x[:half], x[half:2 * half])
        x = y if x.shape[0] % 2 == 0 else jnp.concatenate([y, x[2 * half:]], axis=0)
    return x[0]


def _sublane_allreduce(x, op):
    for sh in (4, 2, 1):
        x = op(x, pltpu.roll(x, sh, 0))
    return x


def _topk_maskbias_t(keys_ref, maskb_ref, far_ref, nch, cw, n_sel, idx_bits, far_limit):
    ng = cw // SUBLANES
    sub = lax.broadcasted_iota(I32, (ng, SUBLANES, LANES), 0) * SUBLANES + lax.broadcasted_iota(
        I32, (ng, SUBLANES, LANES), 1)

    def count(indicator):
        def body(c, acc):
            k0 = pl.multiple_of(c * cw, cw)
            kk = keys_ref[pl.ds(k0, cw), :].reshape(ng, SUBLANES, LANES)
            return acc + _tree_reduce(indicator(kk, k0), jnp.add)
        acc = lax.fori_loop(0, nch, body, jnp.zeros((SUBLANES, LANES), F32))
        return _sublane_allreduce(acc, jnp.add)

    def bisect(it, ans):
        cand = ans | jnp.left_shift(jnp.int32(1), 31 - it)
        cs = cand ^ INT_MIN
        cnt = count(lambda kk, k0: jnp.where(kk >= cs[None], 1.0, 0.0))
        return jnp.where(cnt >= n_sel, cand, ans)

    thr = lax.fori_loop(0, 32, bisect, jnp.zeros((SUBLANES, LANES), I32)) ^ INT_MIN
    need = n_sel - count(lambda kk, k0: jnp.where(kk > thr[None], 1.0, 0.0))
    n_eq = count(lambda kk, k0: jnp.where(kk == thr[None], 1.0, 0.0))
    has_tie = jnp.max(jnp.where(thr == INT_MIN, 0.0, jnp.where(n_eq > need, 1.0, 0.0))) > 0.0

    def tie_search(_):
        def step(it, aj):
            cand = aj | jnp.left_shift(jnp.int32(1), idx_bits - 1 - it)
            cnt = count(lambda kk, k0: jnp.where(kk == thr[None], jnp.where((k0 + sub) < cand[None], 1.0, 0.0), 0.0))
            return jnp.where(cnt < need, cand, aj)
        return lax.fori_loop(0, idx_bits, step, jnp.zeros((SUBLANES, LANES), I32))

    cut = lax.cond(has_tie, tie_search, lambda _: jnp.full((SUBLANES, LANES), 2 ** 30, I32), 0)
    cut = jnp.where(thr == INT_MIN, -1, cut)

    def fin(c, _):
        k0 = pl.multiple_of(c * cw, cw)
        kk = keys_ref[pl.ds(k0, cw), :].reshape(ng, SUBLANES, LANES)
        kpos = k0 + sub
        tie = jnp.where(kpos <= cut[None], 0.0, NEG)
        mb = jnp.where(kk > thr[None], 0.0, jnp.where(kk == thr[None], tie, NEG))
        maskb_ref[pl.ds(k0, cw), :] = mb.reshape(cw, LANES)
        if far_ref is not None:
            far_ref[pl.ds(k0, cw), :] = jnp.where(kpos < far_limit, mb, NEG).reshape(cw, LANES)
        return 0

    lax.fori_loop(0, nch, fin, 0)


def _softmax_step(carry, s, v16, v_is_transposed=False):
    m, l, acc = carry
    mn = jnp.maximum(m, jnp.max(s, axis=1, keepdims=True))
    alpha = jnp.exp(m - mn)
    p = jnp.exp(s - mn)
    l = alpha * l + jnp.sum(p, axis=1, keepdims=True)
    if v_is_transposed:
        pv = lax.dot_general(p.astype(BF16), v16, _NT, preferred_element_type=F32)
    else:
        pv = jnp.dot(p.astype(BF16), v16, preferred_element_type=F32)
    return mn, l, alpha * acc + pv


_QB = 128
_KC = 512
_FW = 512


def _prompt_attn_step(step, nsteps, i, qa_ref, qi_ref, wi_ref, ki_ref, ka_ref, vat_ref, bias_ref, o_ref,
                      keys_sc, maskb_sc, far_sc, qs_sc, q2_sc, m_sc, l_sc, acc_sc):
    lane = lax.broadcasted_iota(I32, (_QB, LANES), 1)
    lo = lane < D_I
    nch = i // (_KC // _QB) + 1
    near0 = jnp.maximum(i - 1, 0) * _QB
    npair = H_A // 2
    cols2 = 2 * _QB

    @pl.when(step == 0)
    def _():
        for h in range(H_I):
            pair = qi_ref[:, (h // 2) * LANES:(h // 2 + 1) * LANES].astype(F32)
            keep = jnp.where(lo, pair, 0.0) if h % 2 == 0 else jnp.where(lo, 0.0, pair)
            qs_sc[h * _QB:(h + 1) * _QB, :] = keep.astype(BF16)

        qpos = i * _QB + lax.broadcasted_iota(I32, (_KC, LANES), 1)

        def scores(c, _):
            k0 = pl.multiple_of(c * _KC, _KC)
            s = lax.dot_general(ki_ref[pl.ds(k0, _KC), :], qs_sc[...], _NT, preferred_element_type=F32)
            sc = jnp.zeros((_KC, LANES), F32)
            for h in range(H_I):
                sc = sc + jnp.maximum(s[:, h * _QB:(h + 1) * _QB], 0.0) * wi_ref[h:h + 1, :]
            kpos = k0 + lax.broadcasted_iota(I32, (_KC, LANES), 0)
            keys_sc[pl.ds(k0, _KC), :] = jnp.where(kpos <= qpos, _sortable_key(sc), INT_MIN)
            return 0

        lax.fori_loop(0, nch, scores, 0)
        _topk_maskbias_t(keys_sc, maskb_sc, far_sc, nch, _KC, TOPK_MAX, 12, near0)

        for p in range(npair):
            qp = qa_ref[:, p * LANES:(p + 1) * LANES].astype(F32)
            q2_sc[p * cols2:(p + 1) * cols2, :] = jnp.concatenate(
                [jnp.where(lo, qp, 0.0), jnp.where(lo, 0.0, qp)], axis=0).astype(BF16)
        m_sc[...] = jnp.full(m_sc.shape, NEG, F32)
        l_sc[...] = jnp.zeros_like(l_sc)
        acc_sc[...] = jnp.zeros_like(acc_sc)

    def update_all(k0, width, mask_ref, biases):
        mb = mask_ref[pl.ds(k0, width), :]
        mb2 = jnp.concatenate([mb, mb], axis=1)

        def qk(p):
            s = lax.dot_general(ka_ref[pl.ds(k0, width), p * LANES:(p + 1) * LANES],
                                q2_sc[p * cols2:(p + 1) * cols2, :], _NT, preferred_element_type=F32) + mb2
            if biases is not None:
                s = s + biases[p]
            return s.reshape(width // SUBLANES, SUBLANES, cols2)

        def softmax(p, s):
            m_old = m_sc[p]
            m_new = jnp.maximum(m_old, _sublane_allreduce(_tree_reduce(s, jnp.maximum), jnp.maximum))
            alpha = jnp.exp(m_old - m_new)
            pexp = jnp.exp(s - m_new[None])
            m_sc[p] = m_new
            l_sc[p] = alpha * l_sc[p] + _tree_reduce(pexp, jnp.add)
            return pexp.reshape(width, cols2).astype(BF16), alpha

        def pv(p, p16, alpha):
            out = jnp.dot(vat_ref[p * LANES:(p + 1) * LANES, pl.ds(k0, width)], p16,
                          preferred_element_type=F32)
            acc = acc_sc[p].reshape(LANES // SUBLANES, SUBLANES, cols2) * alpha[None]
            acc_sc[p] = acc.reshape(LANES, cols2) + out

        logits = [qk(p) for p in range(npair)]
        probs = [softmax(p, logits[p]) for p in range(npair)]
        for p in range(npair):
            pv(p, *probs[p])

    def far(j, _):
        update_all(pl.multiple_of((step + j * nsteps) * _FW, _FW), _FW, far_sc, None)
        return 0

    nfar = (near0 + _FW - 1) // _FW
    lax.fori_loop(0, (nfar - step + nsteps - 1) // nsteps, far, 0)

    @pl.when(step == nsteps - 1)
    def _():
        first = jnp.minimum(i, 1)
        update_all(pl.multiple_of(near0, _QB), 2 * _QB, maskb_sc, [bias_ref[first, p] for p in range(npair)])
        for p in range(npair):
            l = _sublane_allreduce(l_sc[p], jnp.add)
            o = (acc_sc[p].reshape(LANES // SUBLANES, SUBLANES, cols2) / l[None]).reshape(LANES, cols2)
            ot = jnp.concatenate([o[:D_HA, :_QB], o[D_HA:, _QB:]], axis=0)
            o_ref[:, p * LANES:(p + 1) * LANES] = ot.T


_PPS = 16
_TP = SUBLANES
_TK = 2 * SUBLANES


class _PageStream:
    def __init__(self, pt_ref, hbm_refs, bufs, sems, pages_per_group, split_priorities=False):
        self.pt_ref, self.hbm_refs, self.bufs, self.sems, self.ppg = pt_ref, hbm_refs, bufs, sems, pages_per_group
        self.split_priorities = split_priorities

    def _copies(self, seq, group, slot):
        for hbm, buf, sem in zip(self.hbm_refs, self.bufs, self.sems):
            for p in range(self.ppg):
                yield pltpu.make_async_copy(hbm.at[self.pt_ref[seq, group * self.ppg + p]], buf.at[slot, p],
                                            sem.at[slot])

    def start(self, seq, group, slot):
        for n, cp in enumerate(self._copies(seq, group, slot)):
            cp.start(priority=n % 2 if self.split_priorities else 0)

    def wait(self, seq, group, slot):
        for cp in self._copies(seq, group, slot):
            cp.wait()

    def advance(self, step, nsteps, groups_per_seq):
        slot = lax.rem(step, 2)

        @pl.when(step == 0)
        def _():
            self.start(0, 0, 0)

        @pl.when(step + 1 < nsteps)
        def _():
            nxt = step + 1
            self.start(nxt // groups_per_seq, lax.rem(nxt, groups_per_seq), 1 - slot)

        self.wait(step // groups_per_seq, lax.rem(step, groups_per_seq), slot)
        return slot


def _sample_scores_kernel(pt_ref, q_ref, w_ref, kn_ref, cki_ref, o_ref, buf, sem):
    n_pages = buf.shape[1]
    stream = _PageStream(pt_ref, [cki_ref], [buf], [sem], n_pages, split_priorities=True)
    slot = stream.advance(pl.program_id(0), pl.num_programs(0), 1)
    q = q_ref[...]

    t_dec = o_ref.shape[0]

    def head_sum(s):
        n = s.shape[1]
        out = jnp.zeros((_TP, n), F32)
        for h in range(H_I):
            wh = w_ref[h * _TP:(h + 1) * _TP, :]
            out = out + jnp.maximum(s[h * _TP:(h + 1) * _TP, :], 0.0) * jnp.concatenate([wh] * (n // LANES), axis=1)
        return out[:t_dec]

    for c in range(n_pages // _PPS):
        kc = jnp.concatenate([buf[slot, c * _PPS + p].astype(BF16) for p in range(_PPS)], axis=1)
        s = jnp.dot(q, kc, preferred_element_type=F32)
        o_ref[:, c * _PPS * PAGE:(c + 1) * _PPS * PAGE] = head_sum(s)

    kn = jnp.concatenate([kn_ref[...], jnp.zeros((LANES - _TK, D_I), BF16)], axis=0)
    sn = lax.dot_general(q, kn, _NT, preferred_element_type=F32)
    o_ref[:, o_ref.shape[1] - LANES:] = head_sum(sn)


def _sample_scores(page_table, q_st, w_b, ki_new, cache_kidx, n_pages, t_dec):
    nseq = q_st.shape[0]
    lpad = n_pages * PAGE + LANES
    return pl.pallas_call(
        _sample_scores_kernel,
        out_shape=jax.ShapeDtypeStruct((nseq, t_dec, lpad), F32),
        grid_spec=pltpu.PrefetchScalarGridSpec(
            num_scalar_prefetch=1, grid=(nseq,),
            in_specs=[pl.BlockSpec((None, H_I * _TP, D_I), lambda b, pt: (b, 0, 0)),
                      pl.BlockSpec((None, H_I * _TP, LANES), lambda b, pt: (b, 0, 0)),
                      pl.BlockSpec((None, _TK, D_I), lambda b, pt: (b, 0, 0)),
                      pl.BlockSpec(memory_space=pl.ANY)],
            out_specs=pl.BlockSpec((None, t_dec, lpad), lambda b, pt: (b, 0, 0)),
            scratch_shapes=[pltpu.VMEM((2, n_pages, D_I, PAGE), F32), pltpu.SemaphoreType.DMA((2,))]),
        compiler_params=_cp(("arbitrary",)),
        name="sample_scores",
    )(page_table, q_st, w_b, ki_new, cache_kidx)


_SEL_CW = 640


def _sample_select_kernel(s_ref, o_ref, keys_sc, *, past, t_dec):
    lpad = s_ref.shape[0]
    assert LANES % t_dec == 0
    limit = past + lax.broadcasted_iota(I32, (_SEL_CW, LANES), 1) % t_dec
    for c in range(lpad // _SEL_CW):
        sl = slice(c * _SEL_CW, (c + 1) * _SEL_CW)
        kpos = c * _SEL_CW + lax.broadcasted_iota(I32, (_SEL_CW, LANES), 0)
        keys_sc[sl, :] = jnp.where(kpos <= limit, _sortable_key(s_ref[sl, :]), INT_MIN)
    _topk_maskbias_t(keys_sc, o_ref, None, lpad // _SEL_CW, _SEL_CW, TOPK_MAX, 14, None)


def _sample_select(scores_t, past, t_dec):
    lpad, cols = scores_t.shape
    return pl.pallas_call(
        functools.partial(_sample_select_kernel, past=past, t_dec=t_dec),
        out_shape=jax.ShapeDtypeStruct((lpad, cols), F32),
        grid=(cols // LANES,),
        in_specs=[pl.BlockSpec((lpad, LANES), lambda g: (0, g))],
        out_specs=pl.BlockSpec((lpad, LANES), lambda g: (0, g)),
        scratch_shapes=[pltpu.VMEM((lpad, LANES), I32)],
        compiler_params=_cp(("arbitrary",)),
        name="sample_select",
    )(scores_t)


def _sample_attn_step(s_id, last, q_ref, mb_ref, kn_ref, vn_ref, bias_ref, kp, vp, o_ref,
                      qblk_sc, m_sc, l_sc, acc_sc):
    rows = H_A * _TP
    span = _PPS * PAGE

    @pl.when(s_id == 0)
    def _():
        q = jnp.concatenate([q_ref[...]] * H_A, axis=0)
        r = lax.broadcasted_iota(I32, (rows, W_ATTN), 0) // _TP
        c = lax.broadcasted_iota(I32, (rows, W_ATTN), 1) // D_HA
        qblk_sc[...] = jnp.where(r == c, q, 0.0).astype(BF16)
        m_sc[...] = jnp.full(m_sc.shape, NEG, F32)
        l_sc[...] = jnp.zeros_like(l_sc)
        acc_sc[...] = jnp.zeros_like(acc_sc)

    def update(s, v16, v_is_transposed):
        carry = (m_sc[:, 0:1], l_sc[:, 0:1], acc_sc[...])
        m, l, acc = _softmax_step(carry, s, v16, v_is_transposed)
        m_sc[...] = jnp.broadcast_to(m, m_sc.shape)
        l_sc[...] = jnp.broadcast_to(l, l_sc.shape)
        acc_sc[...] = acc

    k16 = jnp.concatenate([pg[...].astype(BF16) for pg in kp], axis=1)
    v16 = jnp.concatenate([pg[...].astype(BF16) for pg in vp], axis=1)
    s = jnp.dot(qblk_sc[...], k16, preferred_element_type=F32)
    mb = mb_ref[:, pl.ds(pl.multiple_of(s_id * span, span), span)]
    last_tile = jnp.where(s_id == last, bias_ref[0], bias_ref[2])
    bias = jnp.concatenate([bias_ref[2]] * (_PPS - 1) + [last_tile], axis=1)
    update(s + jnp.concatenate([mb] * H_A, axis=0) + bias, v16, True)

    @pl.when(s_id == last)
    def _():
        pad = jnp.zeros((LANES - _TK, W_ATTN), BF16)
        kn = jnp.concatenate([kn_ref[...], pad], axis=0)
        vn = jnp.concatenate([vn_ref[...], pad], axis=0)
        sn = lax.dot_general(qblk_sc[...], kn, _NT, preferred_element_type=F32)
        mbn = mb_ref[:, mb_ref.shape[1] - LANES:]
        update(sn + jnp.concatenate([mbn] * H_A, axis=0) + bias_ref[1], vn, False)
        o = acc_sc[...] / l_sc[:, 0:1]
        c = lax.broadcasted_iota(I32, (_TP, W_ATTN), 1) // D_HA
        out = jnp.zeros((_TP, W_ATTN), F32)
        for h in range(H_A):
            out = jnp.where(c == h, o[h * _TP:(h + 1) * _TP, :], out)
        o_ref[...] = out


def _attn_kernel(pt_ref, qa_ref, qi_ref, wi_ref, ki_ref, ka_ref, vat_ref, bias_p_ref,
                 sq_ref, smb_ref, skn_ref, svn_ref, bias_s_ref, ck_ref, cv_ref, o_ref, so_ref, *scratch,
                 nq, nsteps, nseq):
    prompt_scratch, sample_scratch = scratch[:8], scratch[8:12]
    kbuf, vbuf, ksem, vsem = scratch[12:]
    step = pl.program_id(2)
    seq_id = pl.program_id(0) * nq + pl.program_id(1)
    stream = _PageStream(pt_ref, [ck_ref, cv_ref], [kbuf, vbuf], [ksem, vsem], _PPS)
    slot = stream.advance(seq_id * nsteps + step, nseq * nsteps, nsteps)
    _prompt_attn_step(step, nsteps, pl.program_id(1), qa_ref, qi_ref, wi_ref, ki_ref, ka_ref, vat_ref, bias_p_ref,
                      o_ref, *prompt_scratch)
    _sample_attn_step(step, nsteps - 1, sq_ref, smb_ref, skn_ref, svn_ref, bias_s_ref,
                      [kbuf.at[slot, p] for p in range(_PPS)], [vbuf.at[slot, p] for p in range(_PPS)], so_ref,
                      *sample_scratch)


def _attention(qa16, qi16, wi_t, ki16, ka16, va16_t, bias_p, nbatch, seq,
               page_table, q8, maskb, k_new, v_new, bias_s, cache_k, cache_v, n_pages):
    nq = seq // _QB
    nseq = q8.shape[0]
    steps = n_pages // _PPS
    lpad = maskb.shape[1]
    npair = H_A // 2
    assert nseq == nbatch * nq, "one decode sequence per prompt query block"
    blk = lambda w: pl.BlockSpec((_QB, w), lambda b, i, s, pt: (b * nq + i, 0))
    once = pl.Buffered(1)
    per_b = lambda w: pl.BlockSpec((seq, w), lambda b, i, s, pt: (b, 0), pipeline_mode=once)
    seq3 = lambda t: pl.BlockSpec((None, t, W_ATTN), lambda b, i, s, pt: (b * nq + i, 0, 0))
    hbm = pl.BlockSpec(memory_space=pl.ANY)
    return pl.pallas_call(
        functools.partial(_attn_kernel, nq=nq, nsteps=steps, nseq=nseq),
        out_shape=[jax.ShapeDtypeStruct((nbatch * seq, W_ATTN), F32),
                   jax.ShapeDtypeStruct((nseq, _TP, W_ATTN), F32)],
        grid_spec=pltpu.PrefetchScalarGridSpec(
            num_scalar_prefetch=1, grid=(nbatch, nq, steps),
            in_specs=[blk(W_ATTN), blk(H_I * D_I),
                      pl.BlockSpec((H_I, _QB), lambda b, i, s, pt: (0, b * nq + i)),
                      per_b(LANES), per_b(W_ATTN),
                      pl.BlockSpec((None, W_ATTN, seq), lambda b, i, s, pt: (b, 0, 0), pipeline_mode=once),
                      pl.BlockSpec(bias_p.shape, lambda b, i, s, pt: (0, 0, 0, 0), pipeline_mode=once),
                      seq3(_TP), pl.BlockSpec((_TP, lpad), lambda b, i, s, pt: (b * nq + i, 0)),
                      seq3(_TK), seq3(_TK),
                      pl.BlockSpec(bias_s.shape, lambda b, i, s, pt: (0, 0, 0), pipeline_mode=once),
                      hbm, hbm],
            out_specs=[blk(W_ATTN), seq3(_TP)],
            scratch_shapes=[pltpu.VMEM((seq, LANES), I32), pltpu.VMEM((seq, LANES), F32),
                            pltpu.VMEM((seq, LANES), F32),
                            pltpu.VMEM((H_I * _QB, LANES), BF16), pltpu.VMEM((H_A * _QB, LANES), BF16),
                            pltpu.VMEM((npair, SUBLANES, 2 * _QB), F32), pltpu.VMEM((npair, SUBLANES, 2 * _QB), F32),
                            pltpu.VMEM((npair, LANES, 2 * _QB), F32),
                            pltpu.VMEM((H_A * _TP, W_ATTN), BF16), pltpu.VMEM((H_A * _TP, LANES), F32),
                            pltpu.VMEM((H_A * _TP, LANES), F32), pltpu.VMEM((H_A * _TP, W_ATTN), F32),
                            pltpu.VMEM((2, _PPS, W_ATTN, PAGE), F32), pltpu.VMEM((2, _PPS, W_ATTN, PAGE), F32),
                            pltpu.SemaphoreType.DMA((2,)), pltpu.SemaphoreType.DMA((2,))]),
        compiler_params=_cp(("arbitrary", "arbitrary", "arbitrary")),
        name="attention",
    )(page_table, qa16, qi16, wi_t, ki16, ka16, va16_t, bias_p, q8, maskb, k_new, v_new, bias_s, cache_k, cache_v)


def _hgrn_chunk(q, k, v, logf, st, c):
    blk = SUBLANES // 2 if c > SUBLANES else SUBLANES
    row = lax.broadcasted_iota(I32, (c, DK_B), 0)
    g = logf
    sh = 1
    while sh < c:
        g = g + jnp.where(row >= sh, pltpu.roll(g, sh, 0), 0.0)
        sh *= 2

    o = lax.dot_general((q * jnp.exp(g)).astype(BF16), st.astype(BF16), _NT, preferred_element_type=F32)

    if c > SUBLANES:
        rr = lax.broadcasted_iota(I32, (c, c), 0)
        cc = lax.broadcasted_iota(I32, (c, c), 1)
        a = jnp.zeros((c, c), F32)
        w = blk
        while w < c:
            nb = c // (2 * w)
            gb = jnp.broadcast_to(g.reshape(nb, 2 * w, DK_B)[:, w - 1:w, :], (nb, 2 * w, DK_B)).reshape(c, DK_B)
            right = (row % (2 * w)) >= w
            qt = jnp.where(right, q * jnp.exp(jnp.where(right, g - gb, 0.0)), 0.0)
            kt = jnp.where(right, 0.0, k * jnp.exp(jnp.where(right, 0.0, gb - g)))
            aw = lax.dot_general(qt.astype(BF16), kt.astype(BF16), _NT, preferred_element_type=F32)
            a = a + jnp.where((rr // (2 * w)) == (cc // (2 * w)), aw, 0.0)
            w *= 2
        o = o + jnp.dot(a.astype(BF16), v.astype(BF16), preferred_element_type=F32)

    o = o + jnp.sum(q * k, axis=1, keepdims=True) * v
    for d in range(1, blk):
        ok = (row % blk) >= d
        e = jnp.exp(jnp.where(ok, g - pltpu.roll(g, d, 0), 0.0))
        coef = jnp.sum(jnp.where(ok, q * pltpu.roll(k, d, 0) * e, 0.0), axis=1, keepdims=True)
        o = o + coef * pltpu.roll(v, d, 0)

    g_end = g[c - 1:c, :]
    kd = k * jnp.exp(g_end - g)
    st = st * jnp.exp(g_end) + lax.dot_general(v.astype(BF16), kd.astype(BF16), _TN, preferred_element_type=F32)
    return o, st


_HT = 1024


def _hgrn_kernel(q_ref, f_ref, v_ref, s0_ref, o_ref, s_ref, st_sc, *, nb, tb, c):
    tstep = pl.program_id(1)

    @pl.when(tstep == 0)
    def _():
        for j in range(nb):
            for g in range(H_B):
                st_sc[j, g] = s0_ref[j, g].T

    for j in range(nb):
        def body(ci, states):
            r = pl.ds(pl.multiple_of(ci * c, c), c)
            new_states = []
            for g in range(H_B):
                cs = slice(g * DK_B, (g + 1) * DK_B)
                f = f_ref[j, r, cs]
                o, st = _hgrn_chunk(q_ref[j, r, cs], 1.0 - f, v_ref[j, r, cs], jnp.log(f), states[g], c)
                o_ref[j, r, cs] = o
                new_states.append(st)
            return tuple(new_states)
        states = lax.fori_loop(0, tb // c, body, tuple(st_sc[j, g] for g in range(H_B)))
        for g in range(H_B):
            st_sc[j, g] = states[g]

    @pl.when(tstep == pl.num_programs(1) - 1)
    def _():
        for j in range(nb):
            for g in range(H_B):
                s_ref[j, g] = st_sc[j, g].T


def _hgrn(q, f, v, s0, nb, c):
    nseq, t_len, _ = q.shape
    tb = min(t_len, _HT)
    tok = pl.BlockSpec((nb, tb, W_HGRN), lambda b, t: (b, t, 0))
    st = pl.BlockSpec((nb, H_B, DK_B, DV_B), lambda b, t: (b, 0, 0, 0))
    return pl.pallas_call(
        functools.partial(_hgrn_kernel, nb=nb, tb=tb, c=c),
        out_shape=[jax.ShapeDtypeStruct((nseq, t_len, W_HGRN), F32),
                   jax.ShapeDtypeStruct((nseq, H_B, DK_B, DV_B), F32)],
        grid=(nseq // nb, t_len // tb),
        in_specs=[tok, tok, tok, st],
        out_specs=[tok, st],
        scratch_shapes=[pltpu.VMEM((nb, H_B, DV_B, DK_B), F32)],
        compiler_params=_cp(("arbitrary", "arbitrary")),
        name="hgrn",
    )(q, f, v, s0)


def _pad_tokens(a, value=0.0, to=_TP):
    return jnp.pad(a, ((0, 0), (0, to - a.shape[1]), (0, 0)), constant_values=value)


def kernel(x_prompt, x_sample, cache_k, cache_v, cache_kidx, state_hgrn, page_table, c_prompt, c_sample,
           w_ada, b_ada, g_ffn1, ffn1_w_gate, ffn1_w_up, ffn1_w_down, g_mix, w_in, g_q, g_k,
           beta_attn, g_hgrn, w_out, g_ffn2, ffn2_w_gate, ffn2_w_up, ffn2_w_down, rel_bias, lb_logits):
    assert w_ada.shape[0] == 1, "single-layer problem"
    nb_p, seq, _ = x_prompt.shape
    nb_s, t_dec, _ = x_sample.shape
    n_phys = cache_k.shape[1]
    n_pages = page_table.shape[1]
    past = n_pages * PAGE
    assert t_dec <= _TP and n_pages % _PPS == 0 and (past + LANES) % _SEL_CW == 0

    row2 = lambda a: a.reshape(1, -1)
    w_in0 = w_in[0]
    pad_w = jnp.zeros((D_MODEL, LANES - H_I), F32)
    wpack = jnp.concatenate(
        [w_in0[:, :2048], w_in0[:, 2048:2112], w_in0[:, 2048:2112], w_in0[:, 2112:2120], pad_w,
         w_in0[:, 2120:]], axis=1).astype(BF16)
    assert wpack.shape[1] == _N_PACK
    gq = row2(jnp.tile(g_q[0], H_A))
    gk = row2(jnp.tile(g_k[0], H_A))
    gidx = jnp.arange(W_ATTN) // D_HA
    bd = jnp.where(gidx[:, None] == gidx[None, :], 1.0 / D_HA, 0.0).astype(BF16)
    ffn1 = (row2(g_ffn1[0]), ffn1_w_gate[0].astype(BF16), ffn1_w_up[0].astype(BF16), ffn1_w_down[0].astype(BF16))
    ffn2 = (row2(g_ffn2[0]), ffn2_w_gate[0].astype(BF16), ffn2_w_up[0].astype(BF16), ffn2_w_down[0].astype(BF16))
    w_out16 = w_out[0].astype(BF16)

    n_c = nb_p + nb_s
    c_pad = (-n_c) % SUBLANES
    c_all = jnp.concatenate([c_prompt, c_sample, jnp.zeros((c_pad, D_MODEL), F32)], axis=0)
    mods = _ada(c_all, w_ada[0], row2(b_ada[0]))
    bias_p, bias_s = _bias_tiles(rel_bias)

    def front(x, mod, tm, tf, rows_per_group):
        x1 = _ffn(x, mod, 0, *ffn1, tm, tf)
        return x1, _mix(x1, mod, row2(g_mix[0]), wpack, gq, gk, lb_logits, bd, tm, rows_per_group)

    def back(x1, mod, tm, tf, oa, oh, sg):
        return _merge_ffn(oa, oh, sg, x1, mod, row2(beta_attn[0]), row2(g_hgrn[0]), bd, w_out16, *ffn2, tm, tf)

    tm_p, tf = 512, D_FF // 2
    n_tok = nb_s * t_dec
    mod_p = _Mod(mods[:, :nb_p].reshape(N_MOD, nb_p, 1, D_MODEL), False, seq // tm_p)
    mod_s = _Mod(jnp.repeat(mods[:, nb_p:nb_p + nb_s], t_dec, axis=1), True)
    x1p, proj_p = front(x_prompt.reshape(nb_p * seq, D_MODEL), mod_p, tm_p, tf, seq)
    x1s, proj_s = front(x_sample.reshape(n_tok, D_MODEL), mod_s, n_tok, tf, n_tok)
    qa16, kp_t, ka16, vp_t, va16_t, qi16, ki_p, ki16, wi, qb, f, vb, sg_p = proj_p
    qa16s, ks_t, ka16s, vs_t, va16s_t, qi16s, ki_s, ki16s, wis, qbs, fs, vbs, sg_s = proj_s
    va16s = va16s_t[0].T

    ck = jnp.transpose(cache_k[0], (0, 2, 3, 1)).reshape(n_phys, W_ATTN, PAGE)
    cv = jnp.transpose(cache_v[0], (0, 2, 3, 1)).reshape(n_phys, W_ATTN, PAGE)
    cki = jnp.transpose(cache_kidx[0], (0, 2, 1))
    seq3 = lambda a: a.reshape(nb_s, t_dec, a.shape[-1])
    q_st = _pad_tokens(seq3(qi16s)).reshape(nb_s, _TP, H_I, D_I).transpose(0, 2, 1, 3).reshape(nb_s, H_I * _TP, D_I)
    w_st = _pad_tokens(seq3(wis[:, :H_I])).transpose(0, 2, 1).reshape(nb_s, H_I * _TP, 1)
    w_b = jnp.broadcast_to(w_st, (nb_s, H_I * _TP, LANES))
    scores = _sample_scores(page_table, q_st, w_b, _pad_tokens(seq3(ki16s[:, :D_I]), to=_TK), cki, n_pages, t_dec)
    mask_t = _sample_select(scores.reshape(n_tok, -1).T, past, t_dec)
    maskb = _pad_tokens(mask_t.T.reshape(nb_s, t_dec, -1), NEG).reshape(nb_s * _TP, -1)

    oa_p, oa8 = _attention(qa16, qi16, wi[:, :H_I].T, ki16, ka16, va16_t, bias_p, nb_p, seq,
                           page_table, _pad_tokens(seq3(qa16s.astype(F32))), maskb,
                           _pad_tokens(seq3(ka16s), to=_TK), _pad_tokens(seq3(va16s), to=_TK), bias_s, ck, cv, n_pages)
    r3 = lambda a: a.reshape(nb_p, seq, W_HGRN)
    oh_p, sp = _hgrn(r3(qb), r3(f), r3(vb), jnp.zeros((nb_p, H_B, DK_B, DV_B), F32), 1, 64)
    oh8, ss = _hgrn(_pad_tokens(seq3(qbs)), _pad_tokens(seq3(fs), 1.0), _pad_tokens(seq3(vbs)), state_hgrn[0], 8, _TP)

    yp = back(x1p, mod_p, tm_p, tf, oa_p, oh_p.reshape(nb_p * seq, W_HGRN), sg_p)
    ys = back(x1s, mod_s, n_tok, tf, oa8[:, :t_dec].reshape(n_tok, W_ATTN),
              oh8[:, :t_dec].reshape(n_tok, W_HGRN), sg_s)

    def rows_out(a_t, nseq, t_len):
        a = a_t.reshape(-1, H_A, D_HA, a_t.shape[-1]).transpose(0, 3, 1, 2)
        return a.reshape(1, nseq, t_len, H_A, D_HA)

    return (yp.reshape(nb_p, seq, D_MODEL).astype(x_prompt.dtype),
            ys.reshape(nb_s, t_dec, D_MODEL).astype(x_sample.dtype),
            rows_out(kp_t, nb_p, seq), rows_out(vp_t, nb_p, seq),
            ki_p[:, :D_I].reshape(1, nb_p, seq, D_I), sp[None],
            rows_out(ks_t, nb_s, t_dec), rows_out(vs_t, nb_s, t_dec),
            ki_s[:, :D_I].reshape(1, nb_s, t_dec, D_I), ss[None])
```
